```python
import functools
import jax
import jax.numpy as jnp
from jax import lax
import numpy as np

D_MODEL = 2048
BATCH = 2
SEQ = 4096
DEPTH = 4
DEC_BATCH = 8
DEC_SEQ = 1
PAST_LEN = 16384
PAGE_SIZE = 128

EPS = 1e-6
D_HEAD = 128
N_HEADS = D_MODEL // D_HEAD
H_NSA = N_HEADS // 2
G_NSA = H_NSA // 4
HPG = H_NSA // G_NSA
H_SB = N_HEADS // 4
H_GLA = N_HEADS - H_NSA - H_SB
GLA_DK = D_HEAD // 2
GLA_DV = D_HEAD
GLA_RANK = 16
GLA_TAU = 16.0
GLA_CHUNK = 64
CMP_BLOCK = 32
CMP_STRIDE = 16
SEL_BLOCK = 64
N_SELECT = 16
N_LOCAL = 2
WINDOW = 512
FORCE_SCORE = 1e6
NSA_Q_BLOCK = 64
SB_Q_BLOCK = 128
D_FF = ((8 * D_MODEL // 3 + 255) // 256) * 256
SCALE = D_HEAD ** -0.5
IN_SPLITS = (
    H_NSA * D_HEAD,
    3 * H_NSA,
    6 * G_NSA * D_HEAD,
    3 * H_SB * D_HEAD,
    H_GLA * GLA_DK,
    H_GLA * GLA_DK,
    H_GLA * GLA_DV,
    GLA_RANK,
    H_GLA * GLA_DV,
)
IN_OFFSETS = tuple(sum(IN_SPLITS[:i + 1]) for i in range(len(IN_SPLITS) - 1))
N_IN = sum(IN_SPLITS)

kernel_name = 'hybrid_nsa_stickbreak_gla_decoder_step'


def alibi_slopes(n):
    return jnp.exp2(-8.0 * (jnp.arange(n, dtype=jnp.float32) + 1.0) / n)


def rmsnorm(x, g):
    xf = x.astype(jnp.float32)
    y = xf * lax.rsqrt(jnp.mean(xf * xf, axis=-1, keepdims=True) + EPS)
    return (y * g.astype(jnp.float32)).astype(x.dtype)


def head_rms(o):
    of = o.astype(jnp.float32)
    return of * lax.rsqrt(jnp.mean(of * of, axis=-1, keepdims=True) + EPS)


def masked_softmax(s, mask):
    s = jnp.where(mask, s, -jnp.inf)
    m = jnp.max(s, axis=-1, keepdims=True)
    m = jnp.where(jnp.isfinite(m), m, 0.0)
    e = jnp.exp(s - m)
    d = jnp.sum(e, axis=-1, keepdims=True)
    return e / jnp.where(d > 0.0, d, 1.0)


def to_blocks(a, qb):
    b, t = a.shape[:2]
    return jnp.moveaxis(a.reshape(b, t // qb, qb, *a.shape[2:]), 1, 0)


def from_blocks(a):
    n, b, qb = a.shape[:3]
    return jnp.moveaxis(a, 0, 1).reshape(b, n * qb, *a.shape[3:])


def modulation(c, w_ada, b_ada):
    m = jax.nn.silu(c) @ w_ada + b_ada
    return jnp.split(m[:, None, :], 6, axis=-1)


def project_in(h, w_in, gla_wa2, gla_ba):
    b, t, _ = h.shape
    z = h @ w_in
    q_n, gate_n, kv_n, qkv_sb, q_g, k_g, v_g, a_lr, og = jnp.split(z, IN_OFFSETS, axis=-1)
    f32 = jnp.float32
    q_n = q_n.reshape(b, t, H_NSA, D_HEAD)
    gate_n = jax.nn.sigmoid(gate_n.astype(f32)).reshape(b, t, H_NSA, 3)
    kv_n = kv_n.reshape(b, t, 6, G_NSA, D_HEAD)
    qkv_sb = qkv_sb.reshape(b, t, 3, H_SB, D_HEAD)
    q_g = q_g.astype(f32).reshape(b, t, H_GLA, GLA_DK) * (GLA_DK ** -0.5)
    k_g = k_g.astype(f32).reshape(b, t, H_GLA, GLA_DK)
    v_g = v_g.astype(f32).reshape(b, t, H_GLA, GLA_DV)
    loga = jax.nn.log_sigmoid((a_lr @ gla_wa2 + gla_ba).astype(f32)).reshape(b, t, H_GLA, GLA_DK) / GLA_TAU
    og = og.reshape(b, t, H_GLA, GLA_DV)
    return q_n, gate_n, kv_n, qkv_sb, q_g, k_g, v_g, loga, og


def compress(kv, pe, w1, w2):
    b, l = kv.shape[:2]
    n_seg = l // CMP_STRIDE
    seg = kv[:, :n_seg * CMP_STRIDE].reshape(b, n_seg, CMP_STRIDE, 2, G_NSA, D_HEAD)
    blocks = jnp.concatenate([seg[:, :-1], seg[:, 1:]], axis=2)
    blocks = blocks + jnp.transpose(pe, (1, 0, 2))[:, :, None, :]
    nc = n_seg - 1
    flat = jnp.transpose(blocks, (0, 1, 3, 4, 2, 5)).reshape(b, nc, 2, G_NSA, CMP_BLOCK * D_HEAD)
    hid = jax.nn.silu(jnp.einsum('bncgf,cfe->bncge', flat, w1))
    return jnp.einsum('bncge,cef->bncgf', hid, w2)


def to_sel_blocks(kv):
    b, l = kv.shape[:2]
    ns = -(-l // SEL_BLOCK)
    kv = jnp.pad(kv, ((0, 0), (0, ns * SEL_BLOCK - l), (0, 0), (0, 0), (0, 0)))
    return kv.reshape(b, ns, SEL_BLOCK, 2, G_NSA, D_HEAD)


def nsa_attend(q, gates, pos, cmp_kv, sel_blocks, win_kv, win_pos0, slopes):
    b, nq = q.shape[:2]
    nc = cmp_kv.shape[1]
    ns = sel_blocks.shape[1]
    wk = win_kv.shape[1]
    f32 = jnp.float32
    qg = (q * SCALE).reshape(b, nq, G_NSA, HPG, D_HEAD)
    m = slopes.reshape(G_NSA, HPG)[:, :, None]

    c_end = jnp.arange(nc) * CMP_STRIDE + (CMP_BLOCK - 1)
    c_dist = (pos[:, None] - c_end[None, :]).astype(f32)
    s = jnp.einsum('bqghd,bngd->bqghn', qg, cmp_kv[:, :, 0]).astype(f32)
    s = s - m * c_dist[:, None, None, :]
    p_cmp = masked_softmax(s, (c_dist >= 0.0)[:, None, None, :])
    o_cmp = jnp.einsum('bqghn,bngd->bqghd', p_cmp.astype(q.dtype), cmp_kv[:, :, 1])

    imp = jnp.sum(p_cmp, axis=3)
    r = SEL_BLOCK // CMP_STRIDE
    imp = jnp.pad(imp, ((0, 0), (0, 0), (0, 0), (1, r * (ns + 1) - nc - 1)))
    rows = imp.reshape(b, nq, G_NSA, ns + 1, r)
    p_slc = jnp.sum(rows[..., :ns, :], axis=-1) + rows[..., 1:, 0]
    blk = jnp.arange(ns)
    cur = pos // SEL_BLOCK
    causal = blk[None, :] <= cur[:, None]
    forced = (blk[None, :] == 0) | (blk[None, :] > cur[:, None] - N_LOCAL)
    score = jnp.where(forced[:, None, :], FORCE_SCORE, p_slc)
    score = jnp.where(causal[:, None, :], score, -jnp.inf)
    top_val, top_idx = lax.top_k(score, min(N_SELECT, ns))

    kb = jnp.moveaxis(sel_blocks, 4, 1)
    idx_t = jnp.moveaxis(top_idx, 2, 1)
    gath = jax.vmap(jax.vmap(lambda t, i: t[i]))(kb, idx_t)
    tok = top_idx[..., None] * SEL_BLOCK + jnp.arange(SEL_BLOCK)
    s_dist = (pos[None, :, None, None, None] - tok).astype(f32)
    s_mask = jnp.isfinite(top_val)[..., None] & (s_dist >= 0.0)
    n_sel = top_idx.shape[-1]
    s = jnp.einsum('bqghd,bgqkld->bqghkl', qg, gath[..., 0, :]).astype(f32)
    s = s - m[..., None] * s_dist[:, :, :, None]
    p = masked_softmax(s.reshape(b, nq, G_NSA, HPG, n_sel * SEL_BLOCK),
                       s_mask.reshape(b, nq, G_NSA, 1, n_sel * SEL_BLOCK))
    p = p.reshape(b, nq, G_NSA, HPG, n_sel, SEL_BLOCK)
    o_sel = jnp.einsum('bqghkl,bgqkld->bqghd', p.astype(q.dtype), gath[..., 1, :])

    w_pos = win_pos0 + jnp.arange(wk)
    w_dist = pos[:, None] - w_pos[None, :]
    w_mask = (w_dist >= 0) & (w_dist <= WINDOW) & (w_pos[None, :] >= 0)
    s = jnp.einsum('bqghd,bwgd->bqghw', qg, win_kv[:, :, 0]).astype(f32)
    s = s - m * w_dist.astype(f32)[:, None, None, :]
    p = masked_softmax(s, w_mask[:, None, None, :])
    o_win = jnp.einsum('bqghw,bwgd->bqghd', p.astype(q.dtype), win_kv[:, :, 1])

    g = gates.reshape(b, nq, G_NSA, HPG, 3).astype(q.dtype)
    o = g[..., 0:1] * o_cmp + g[..., 1:2] * o_sel + g[..., 2:3] * o_win
    return o.reshape(b, nq, H_NSA, D_HEAD)


def sb_attend(q, pos, kv):
    l = kv.shape[1]
    z = jnp.einsum('bqhd,blhd->bhql', q * SCALE, kv[:, :, 0]).astype(jnp.float32)
    mask = jnp.arange(l)[None, :] < pos[:, None]
    log_beta = jax.nn.log_sigmoid(z)
    log_rest = jnp.where(mask, jax.nn.log_sigmoid(-z), 0.0)
    after = lax.cumsum(log_rest, axis=3, reverse=True) - log_rest
    a = jnp.where(mask, jnp.exp(log_beta + after), 0.0)
    return jnp.einsum('bhql,blhd->bqhd', a.astype(q.dtype), kv[:, :, 1])


def gla_scan(q, k, v, loga, s0):
    b, t = q.shape[:2]
    c = GLA_CHUNK if t % GLA_CHUNK == 0 else t
    n = t // c
    tri = jnp.tril(jnp.ones((c, c), dtype=bool))[None, :, :, None, None]

    def chunks(a):
        return jnp.moveaxis(a.reshape(b, n, c, *a.shape[2:]), 1, 0)

    def step(s, inp):
        qc, kc, vc, ac = inp
        cb = jnp.cumsum(ac, axis=1)
        inter = jnp.einsum('bthk,bhkv->bthv', qc * jnp.exp(cb), s)
        decay = jnp.exp(jnp.where(tri, cb[:, :, None] - cb[:, None, :], -jnp.inf))
        att = jnp.einsum('bthk,bshk,btshk->bhts', qc, kc, decay)
        intra = jnp.einsum('bhts,bshv->bthv', att, vc)
        c_last = cb[:, -1]
        s_new = jnp.exp(c_last)[..., None] * s + jnp.einsum(
            'bshk,bshv->bhkv', kc * jnp.exp(c_last[:, None] - cb), vc)
        return s_new, inter + intra

    s_fin, o = lax.scan(step, s0, (chunks(q), chunks(k), chunks(v), chunks(loga)))
    return jnp.moveaxis(o, 0, 1).reshape(b, t, H_GLA, GLA_DV), s_fin


def mix_out(o_nsa, o_sb, o_gla, og, g_mix, w_out, dtype):
    b, t = o_nsa.shape[:2]
    o = jnp.concatenate([head_rms(o_nsa), head_rms(o_sb),
                         head_rms(o_gla) * jax.nn.silu(og.astype(jnp.float32))], axis=2)
    o = (o.reshape(b, t, D_MODEL) * g_mix.astype(jnp.float32)).astype(dtype)
    return o @ w_out


def swiglu(h, w1, w3, w2):
    return (jax.nn.silu(h @ w1) * (h @ w3)) @ w2


def prompt_mixers(q_n, gate_n, kv_n, qkv_sb, q_g, k_g, v_g, loga, *, cmp_pe, cmp_w1, cmp_w2, wbuf):
    b, t = q_n.shape[:2]
    slopes = alibi_slopes(H_NSA)
    cmp_kv = compress(kv_n[:, :, 0:2], cmp_pe, cmp_w1, cmp_w2)
    sel_blocks = to_sel_blocks(kv_n[:, :, 2:4])
    win = kv_n[:, :, 4:6]
    win_pad = jnp.pad(win, ((0, 0), (WINDOW, 0), (0, 0), (0, 0), (0, 0)))

    def nsa_step(args):
        qb, gb, start = args
        pos = start + jnp.arange(NSA_Q_BLOCK)
        wkv = lax.dynamic_slice_in_dim(win_pad, start, WINDOW + NSA_Q_BLOCK, axis=1)
        return nsa_attend(qb, gb, pos, cmp_kv, sel_blocks, wkv, start - WINDOW, slopes)

    nsa_starts = jnp.arange(t // NSA_Q_BLOCK, dtype=jnp.int32) * NSA_Q_BLOCK
    o_nsa = from_blocks(lax.map(nsa_step, (to_blocks(q_n, NSA_Q_BLOCK),
                                           to_blocks(gate_n, NSA_Q_BLOCK), nsa_starts)))

    sb_kv = qkv_sb[:, :, 1:3]

    def sb_step(args):
        qb, start = args
        return sb_attend(qb, start + jnp.arange(SB_Q_BLOCK), sb_kv)

    sb_starts = jnp.arange(t // SB_Q_BLOCK, dtype=jnp.int32) * SB_Q_BLOCK
    o_sb = from_blocks(lax.map(sb_step, (to_blocks(qkv_sb[:, :, 0], SB_Q_BLOCK), sb_starts)))

    o_gla, s_fin = gla_scan(q_g, k_g, v_g, loga, jnp.zeros((b, H_GLA, GLA_DK, GLA_DV), jnp.float32))
    win_state = jnp.pad(win, ((0, 0), (wbuf, 0), (0, 0), (0, 0), (0, 0)))[:, -wbuf:]
    return o_nsa, o_sb, o_gla, (kv_n[:, :, 0:4], sb_kv, win_state, s_fin)


def sample_mixers(q_n, gate_n, kv_n, qkv_sb, q_g, k_g, v_g, loga, *, layer, cache_nsa, cache_sb,
                  state_win, state_gla, page_table, cmp_pe, cmp_w1, cmp_w2):
    b, t = q_n.shape[:2]
    past = page_table.shape[1] * cache_nsa.shape[2]
    wbuf = state_win.shape[2]
    slopes = alibi_slopes(H_NSA)
    pos = past + jnp.arange(t)
    nsa_past = cache_nsa[layer, page_table].reshape(b, past, 4, G_NSA, D_HEAD)
    nsa_all = jnp.concatenate([nsa_past, kv_n[:, :, 0:4]], axis=1)
    cmp_kv = compress(nsa_all[:, :, 0:2], cmp_pe, cmp_w1, cmp_w2)
    sel_blocks = to_sel_blocks(nsa_all[:, :, 2:4])
    win_all = jnp.concatenate([state_win[layer], kv_n[:, :, 4:6]], axis=1)
    o_nsa = nsa_attend(q_n, gate_n, pos, cmp_kv, sel_blocks, win_all, past - wbuf, slopes)
    sb_past = cache_sb[layer, page_table].reshape(b, past, 2, H_SB, D_HEAD)
    sb_new = qkv_sb[:, :, 1:3]
    o_sb = sb_attend(qkv_sb[:, :, 0], pos, jnp.concatenate([sb_past, sb_new], axis=1))
    o_gla, s_new = gla_scan(q_g, k_g, v_g, loga, state_gla[layer].astype(jnp.float32))
    return o_nsa, o_sb, o_gla, (kv_n[:, :, 0:4], sb_new, win_all[:, -wbuf:], s_new.astype(state_gla.dtype))


def run_layer(x, c, mixer_fn, norm1, norm2, w_ada, b_ada, w_in, gla_wa2, gla_ba, g_mix, w_out, w1, w3, w2):
    sh1, sc1, gt1, sh2, sc2, gt2 = modulation(c, w_ada, b_ada)
    h = rmsnorm(x, norm1) * (1.0 + sc1) + sh1
    q_n, gate_n, kv_n, qkv_sb, q_g, k_g, v_g, loga, og = project_in(h, w_in, gla_wa2, gla_ba)
    o_nsa, o_sb, o_gla, new_state = mixer_fn(q_n, gate_n, kv_n, qkv_sb, q_g, k_g, v_g, loga)
    x = x + gt1 * mix_out(o_nsa, o_sb, o_gla, og, g_mix, w_out, x.dtype)
    h = rmsnorm(x, norm2) * (1.0 + sc2) + sh2
    x = x + gt2 * swiglu(h, w1, w3, w2)
    return x, new_state


def setup_inputs(seed: int = 0) -> dict:
    key = jax.random.key(seed)
    ks = jax.random.split(key, 25)
    f32 = jnp.float32
    n_pages = PAST_LEN // PAGE_SIZE
    n_used = DEC_BATCH * n_pages
    n_pool = n_used + max(1, n_used // 4)
    wbuf = min(WINDOW, PAST_LEN)

    def nrm(i, shape, scale=1.0):
        return jax.random.normal(ks[i], shape, f32) * scale

    page_table = jax.random.permutation(ks[8], n_pool)[:n_used].reshape(DEC_BATCH, n_pages).astype(jnp.int32)
    return {
        'x_prompt': nrm(0, (BATCH, SEQ, D_MODEL)),
        'x_sample': nrm(1, (DEC_BATCH, DEC_SEQ, D_MODEL)),
        'cache_nsa': nrm(2, (DEPTH, n_pool, PAGE_SIZE, 4, G_NSA, D_HEAD)),
        'cache_sb': nrm(3, (DEPTH, n_pool, PAGE_SIZE, 2, H_SB, D_HEAD)),
        'state_win': nrm(4, (DEPTH, DEC_BATCH, wbuf, 2, G_NSA, D_HEAD)),
        'state_gla': nrm(5, (DEPTH, DEC_BATCH, H_GLA, GLA_DK, GLA_DV)),
        'page_table': page_table,
        'c_prompt': nrm(6, (BATCH, D_MODEL)),
        'c_sample': nrm(7, (DEC_BATCH, D_MODEL)),
        'norm1': 1.0 + nrm(9, (DEPTH, D_MODEL), 0.1),
        'norm2': 1.0 + nrm(10, (DEPTH, D_MODEL), 0.1),
        'w_ada': nrm(11, (DEPTH, D_MODEL, 6 * D_MODEL), 0.5 * D_MODEL ** -0.5),
        'b_ada': nrm(12, (DEPTH, 6 * D_MODEL), 0.01),
        'w_in': nrm(13, (DEPTH, D_MODEL, N_IN), D_MODEL ** -0.5),
        'gla_wa2': nrm(14, (DEPTH, GLA_RANK, H_GLA * GLA_DK), GLA_RANK ** -0.5),
        'gla_ba': nrm(15, (DEPTH, H_GLA * GLA_DK), 0.1),
        'cmp_pe': nrm(16, (DEPTH, 2, CMP_BLOCK, D_HEAD), 0.1),
        'cmp_w1': nrm(17, (DEPTH, 2, CMP_BLOCK * D_HEAD, D_HEAD), (CMP_BLOCK * D_HEAD) ** -0.5),
        'cmp_w2': nrm(18, (DEPTH, 2, D_HEAD, D_HEAD), D_HEAD ** -0.5),
        'g_mix': 1.0 + nrm(19, (DEPTH, D_MODEL), 0.1),
        'w_out': nrm(20, (DEPTH, D_MODEL, D_MODEL), D_MODEL ** -0.5),
        'ffn_w1': nrm(21, (DEPTH, D_MODEL, D_FF), D_MODEL ** -0.5),
        'ffn_w3': nrm(22, (DEPTH, D_MODEL, D_FF), D_MODEL ** -0.5),
        'ffn_w2': nrm(23, (DEPTH, D_FF, D_MODEL), D_FF ** -0.5),
        'final_norm': 1.0 + nrm(24, (D_MODEL,), 0.1),
    }


def reference(x_prompt, x_sample, cache_nsa, cache_sb, state_win, state_gla, page_table, c_prompt, c_sample,
              norm1, norm2, w_ada, b_ada, w_in, gla_wa2, gla_ba, cmp_pe, cmp_w1, cmp_w2, g_mix, w_out,
              ffn_w1, ffn_w3, ffn_w2, final_norm):
    wbuf = state_win.shape[2]
    xp, xs = x_prompt, x_sample
    nsa_p, nsa_s, sb_p, sb_s, win_p, win_s, gla_p, gla_s = [], [], [], [], [], [], [], []
    for l in range(DEPTH):
        lw = (norm1[l], norm2[l], w_ada[l], b_ada[l], w_in[l], gla_wa2[l], gla_ba[l], g_mix[l], w_out[l],
              ffn_w1[l], ffn_w3[l], ffn_w2[l])
        p_fn = functools.partial(prompt_mixers, cmp_pe=cmp_pe[l], cmp_w1=cmp_w1[l], cmp_w2=cmp_w2[l], wbuf=wbuf)
        xp, (a0, a1, a2, a3) = run_layer(xp, c_prompt, p_fn, *lw)
        s_fn = functools.partial(sample_mixers, layer=l, cache_nsa=cache_nsa, cache_sb=cache_sb,
                                 state_win=state_win, state_gla=state_gla, page_table=page_table,
                                 cmp_pe=cmp_pe[l], cmp_w1=cmp_w1[l], cmp_w2=cmp_w2[l])
        xs, (b0, b1, b2, b3) = run_layer(xs, c_sample, s_fn, *lw)
        nsa_p.append(a0)
        sb_p.append(a1)
        win_p.append(a2)
        gla_p.append(a3)
        nsa_s.append(b0)
        sb_s.append(b1)
        win_s.append(b2)
        gla_s.append(b3)
    y_prompt = rmsnorm(xp, final_norm)
    y_sample = rmsnorm(xs, final_norm)
    return (y_prompt, y_sample, jnp.stack(nsa_p), jnp.stack(nsa_s), jnp.stack(sb_p), jnp.stack(sb_s),
            jnp.stack(win_p), jnp.stack(win_s), jnp.stack(gla_p), jnp.stack(gla_s))
```

```python
import functools

import numpy as np
import jax
import jax.numpy as jnp
from jax import lax
from jax.experimental import pallas as pl
from jax.experimental.pallas import tpu as pltpu

F32 = jnp.float32
BF16 = jnp.bfloat16

EPS = 1e-6
D_HEAD = 128
H_NSA = 8
G_NSA = 2
HPG = 4
H_SB = 4
H_GLA = 4
GLA_DK = 64
GLA_DV = 128
GLA_RANK = 16
GLA_TAU = 16.0
GLA_CHUNK = 64
CMP_BLOCK = 32
CMP_STRIDE = 16
SEL_BLOCK = 64
N_SELECT = 16
N_LOCAL = 2
WINDOW = 512
FORCE_SCORE = 1e6
SCALE = D_HEAD ** -0.5
LANE = 128
MS = 16

C_Q = 0
C_KV = 1024
C_SB = 2560
C_QG = 4096
C_KG = 4352
C_VG = 4608
C_OG = 5120
C_SM = 5632
N_Z = 5760
SM_GATE = 0
SM_ALR = 24

NEG = -1e30
PEN = -(2.0 ** 100)
PEN_TEST = -(2.0 ** 90)

VMEM_BIG = 56 * 1024 * 1024


def _cp(sem, vmem=None):
    return pltpu.CompilerParams(dimension_semantics=sem, vmem_limit_bytes=vmem)


def _log_sigmoid(x):
    return jnp.minimum(x, 0.0) - jnp.log(1.0 + jnp.exp(-jnp.abs(x)))


def _silu(x):
    return x * (1.0 / (1.0 + jnp.exp(-x)))


def _split2(x):
    hi = x.astype(BF16)
    lo = (x - hi.astype(F32)).astype(BF16)
    return hi, lo


def _split3(x):
    hi = x.astype(BF16)
    r = x - hi.astype(F32)
    mid = r.astype(BF16)
    lo = (r - mid.astype(F32)).astype(BF16)
    return hi, mid, lo


def _dot(a, b):
    return jnp.dot(a, b, preferred_element_type=F32)


def _dot_nt(a, b):
    return lax.dot_general(a, b, (((1,), (1,)), ((), ())), preferred_element_type=F32)


def _dot_tn(a, b):
    return lax.dot_general(a, b, (((0,), (0,)), ((), ())), preferred_element_type=F32)


def _mm_plain_kernel(a_ref, w_ref, b_ref, o_ref, *, silu_a):
    a = a_ref[...]
    if silu_a:
        a = _silu(a.astype(F32))
    o_ref[...] = _dot(a.astype(BF16), w_ref[...].astype(BF16)) + b_ref[...]


def matmul_bias(a, w, b, *, tm, tn, silu_a=False):
    nl, m, k = a.shape
    n = w.shape[2]
    return pl.pallas_call(
        functools.partial(_mm_plain_kernel, silu_a=silu_a),
        grid=(nl, m // tm, n // tn),
        in_specs=[pl.BlockSpec((None, tm, k), lambda l, i, j: (l, i, 0)),
                  pl.BlockSpec((None, k, tn), lambda l, i, j: (l, 0, j)),
                  pl.BlockSpec((None, 1, tn), lambda l, i, j: (l, 0, j))],
        out_specs=pl.BlockSpec((None, tm, tn), lambda l, i, j: (l, i, j)),
        out_shape=jax.ShapeDtypeStruct((nl, m, n), F32),
        compiler_params=_cp(("arbitrary", "arbitrary", "arbitrary"), VMEM_BIG),
        name="matmul_bias",
    )(a, w, b)


def _mm_swiglu_kernel(a_ref, w1_ref, w3_ref, o_ref):
    a = a_ref[...]
    u = _dot(a, w1_ref[...].astype(BF16))
    v = _dot(a, w3_ref[...].astype(BF16))
    o_ref[...] = (_silu(u) * v).astype(o_ref.dtype)


def matmul_swiglu(a, w1, w3, *, tm, tn):
    m, k = a.shape
    n = w1.shape[1]
    return pl.pallas_call(
        _mm_swiglu_kernel,
        grid=(m // tm, n // tn),
        in_specs=[pl.BlockSpec((tm, k), lambda i, j: (i, 0)),
                  pl.BlockSpec((k, tn), lambda i, j: (0, j)),
                  pl.BlockSpec((k, tn), lambda i, j: (0, j))],
        out_specs=pl.BlockSpec((tm, tn), lambda i, j: (i, j)),
        out_shape=jax.ShapeDtypeStruct((m, n), BF16),
        compiler_params=_cp(("arbitrary", "arbitrary"), VMEM_BIG),
        name="matmul_swiglu",
    )(a, w1, w3)


def _mm_resid_kernel(*refs, a_widths, nk):
    n_a = len(a_widths)
    a_refs = refs[:n_a]
    w_ref, x_ref, g_ref, o_ref = refs[n_a:n_a + 4]
    part = None
    off = 0
    for a_ref, kw in zip(a_refs, a_widths):
        d = _dot(a_ref[...], w_ref[off:off + kw, :].astype(BF16))
        part = d if part is None else part + d
        off += kw
    if nk == 1:
        o_ref[...] = x_ref[...] + g_ref[...] * part
        return
    acc_ref = refs[n_a + 4]
    k = pl.program_id(2)

    @pl.when(k == 0)
    def _():
        acc_ref[...] = part

    @pl.when(k > 0)
    def _():
        acc_ref[...] += part

    @pl.when(k == nk - 1)
    def _():
        o_ref[...] = x_ref[...] + g_ref[...] * acc_ref[...]


def matmul_resid(a_list, w, x, gate, *, tm, tn, tk=None):
    m, n = x.shape
    k_total = w.shape[0]
    a_widths = tuple(a.shape[1] for a in a_list)
    if tk is None:
        tk = k_total
    nk = k_total // tk
    assert nk == 1 or len(a_list) == 1
    g_cnt, g_rows, _ = gate.shape
    rows_per_gate = m // g_cnt
    if nk == 1:
        a_specs = [pl.BlockSpec((tm, kw), lambda i, j, k: (i, 0)) for kw in a_widths]
        kernel_widths = a_widths
    else:
        a_specs = [pl.BlockSpec((tm, tk), lambda i, j, k: (i, k))]
        kernel_widths = (tk,)
    g_blk = 1 if g_rows == 1 else tm
    return pl.pallas_call(
        functools.partial(_mm_resid_kernel, a_widths=kernel_widths, nk=nk),
        grid=(m // tm, n // tn, nk),
        in_specs=a_specs + [
            pl.BlockSpec((tk, tn), lambda i, j, k: (k, j)),
            pl.BlockSpec((tm, tn), lambda i, j, k: (i, j)),
            pl.BlockSpec((None, g_blk, tn), lambda i, j, k: ((i * tm) // rows_per_gate, 0, j)),
        ],
        out_specs=pl.BlockSpec((tm, tn), lambda i, j, k: (i, j)),
        out_shape=jax.ShapeDtypeStruct((m, n), F32),
        scratch_shapes=[pltpu.VMEM((tm, tn), F32)] if nk > 1 else [],
        compiler_params=_cp(("arbitrary", "arbitrary", "arbitrary"), VMEM_BIG),
        name="matmul_resid",
    )(*a_list, w, x, gate)


def _norm_mod_kernel(x_ref, g_ref, sc_ref, sh_ref, o_ref):
    x = x_ref[...]
    y = x * lax.rsqrt(jnp.mean(x * x, axis=-1, keepdims=True) + EPS) * g_ref[...]
    o_ref[...] = (y * (1.0 + sc_ref[...]) + sh_ref[...]).astype(o_ref.dtype)


def norm_mod(x, g, sc, sh, *, tm):
    m, d = x.shape
    g_cnt, g_rows, _ = sc.shape
    rows_per_gate = m // g_cnt
    g_blk = 1 if g_rows == 1 else tm
    mod_spec = pl.BlockSpec((None, g_blk, d), lambda i: ((i * tm) // rows_per_gate, 0, 0))
    return pl.pallas_call(
        _norm_mod_kernel,
        grid=(m // tm,),
        in_specs=[pl.BlockSpec((tm, d), lambda i: (i, 0)),
                  pl.BlockSpec((1, d), lambda i: (0, 0)), mod_spec, mod_spec],
        out_specs=pl.BlockSpec((tm, d), lambda i: (i, 0)),
        out_shape=jax.ShapeDtypeStruct((m, d), BF16),
        compiler_params=_cp(("arbitrary",)),
        name="norm_mod",
    )(x, g, sc, sh)


def _rmsnorm_kernel(x_ref, g_ref, o_ref):
    x = x_ref[...]
    o_ref[...] = x * lax.rsqrt(jnp.mean(x * x, axis=-1, keepdims=True) + EPS) * g_ref[...]


def rmsnorm_rows(x, g, *, tm):
    m, d = x.shape
    return pl.pallas_call(
        _rmsnorm_kernel,
        grid=(m // tm,),
        in_specs=[pl.BlockSpec((tm, d), lambda i: (i, 0)), pl.BlockSpec((1, d), lambda i: (0, 0))],
        out_specs=pl.BlockSpec((tm, d), lambda i: (i, 0)),
        out_shape=jax.ShapeDtypeStruct((m, d), F32),
        compiler_params=_cp(("arbitrary",)),
        name="rmsnorm_rows",
    )(x, g)


def _compress_kernel(s_ref, w1_ref, pe_ref, w2_ref, o_ref):
    n_seg = s_ref.shape[0]
    half = s_ref.shape[1]
    s = s_ref[...].astype(BF16)
    w_lo = w1_ref[0].astype(BF16)
    w_hi = w1_ref[1].astype(BF16)
    lo = _dot(s, w_lo)
    hi = _dot(s, w_hi)
    pe = pe_ref[...].astype(BF16)
    bias = _dot(pe[:, :half], w_lo) + _dot(pe[:, half:], w_hi)
    pre = lo + pltpu.roll(hi, n_seg - 1, 0) + bias[0:1, :]
    out = _dot(_silu(pre).astype(BF16), w2_ref[...].astype(BF16))
    row = lax.broadcasted_iota(jnp.int32, out.shape, 0)
    o_ref[...] = jnp.where(row < n_seg - 1, out, 0.0)


def compress_segments(seg, w1, pe, w2):
    b, _, n_seg, half = seg.shape
    w1r = w1.reshape(2, 2, half, D_HEAD)
    pe8 = jnp.broadcast_to(pe.reshape(2, 1, 2 * half), (2, 8, 2 * half))
    return pl.pallas_call(
        _compress_kernel,
        grid=(b, 4),
        in_specs=[pl.BlockSpec((None, None, n_seg, half), lambda i, j: (i, j, 0, 0)),
                  pl.BlockSpec((None, 2, half, D_HEAD), lambda i, j: (j // 2, 0, 0, 0)),
                  pl.BlockSpec((None, 8, 2 * half), lambda i, j: (j // 2, 0, 0)),
                  pl.BlockSpec((None, D_HEAD, D_HEAD), lambda i, j: (j // 2, 0, 0))],
        out_specs=pl.BlockSpec((None, None, n_seg, D_HEAD), lambda i, j: (i, j, 0, 0)),
        out_shape=jax.ShapeDtypeStruct((b, 4, n_seg, D_HEAD), F32),
        compiler_params=_cp(("arbitrary", "arbitrary"), VMEM_BIG),
        name="compress_segments",
    )(seg, w1r, pe8, w2)


def _slope_column(g, rows_per_head):
    n = HPG * rows_per_head
    h = lax.broadcasted_iota(jnp.int32, (n, 1), 0) // rows_per_head + g * HPG
    out = jnp.zeros((n, 1), F32)
    for hh in range(H_NSA):
        out = jnp.where(h == hh, 2.0 ** (-8.0 * (hh + 1) / H_NSA), out)
    return out


def _masked_softmax_rows(s, valid):
    s = jnp.where(valid, s, NEG)
    m = jnp.max(s, axis=-1, keepdims=True)
    e = jnp.where(valid, jnp.exp(s - m), 0.0)
    d = jnp.sum(e, axis=-1, keepdims=True)
    return e / jnp.where(d > 0.0, d, 1.0)


def _select_blocks(score, n_blocks):
    lane = lax.broadcasted_iota(jnp.int32, score.shape, 1)
    rank = jnp.zeros(score.shape, jnp.int32)
    for i in range(n_blocks):
        c = score[:, i:i + 1]
        ahead = (c > score) | ((c == score) & (lane > i))
        rank = rank + ahead.astype(jnp.int32)
    return (rank < N_SELECT) & (score > -jnp.inf)


def _slc_matrix(n_cmp_rows, nc, n_lanes):
    i = np.arange(n_cmp_rows)[:, None]
    j = np.arange(n_lanes)[None, :]
    m = (i >= 4 * j - 1) & (i <= 4 * j + 3) & (i < nc)
    return jnp.asarray(m.astype(np.float32), dtype=BF16)


def _head_rms(o):
    return o * lax.rsqrt(jnp.mean(o * o, axis=-1, keepdims=True) + EPS)


def _nsa_prompt_kernel(q_ref, sm_ref, selk_ref, selv_ref, wink_ref, winv_ref, cmpk_ref, cmpv_ref,
                       kaug_ref, mslc_ref, gm_ref, o_ref, *, qb, tk):
    g = pl.program_id(1)
    qi = pl.program_id(2)
    q0 = qi * qb
    rows = HPG * qb
    n_seg = cmpk_ref.shape[0]
    nc = n_seg - 1
    t_len = selk_ref.shape[0]
    ns = t_len // SEL_BLOCK

    q = q_ref[...] * SCALE
    qs = jnp.concatenate([q[:, h * D_HEAD:(h + 1) * D_HEAD] for h in range(HPG)], axis=0)
    qs_b = qs.astype(BF16)
    slope = _slope_column(g, qb)
    rpos = q0 + lax.broadcasted_iota(jnp.int32, (rows, 1), 0) % qb

    ci = lax.broadcasted_iota(jnp.int32, (1, n_seg), 1)
    c_dist = rpos - (ci * CMP_STRIDE + (CMP_BLOCK - 1))
    c_valid = (c_dist >= 0) & (ci < nc)
    s = _dot_nt(qs_b, cmpk_ref[...].astype(BF16)) - slope * c_dist.astype(F32)
    p_cmp = _masked_softmax_rows(s, c_valid)
    o_cmp = _dot(p_cmp.astype(BF16), cmpv_ref[...].astype(BF16))

    imp = p_cmp[0:qb]
    for h in range(1, HPG):
        imp = imp + p_cmp[h * qb:(h + 1) * qb]
    p_slc = None
    for part in _split3(imp):
        d = _dot(part, mslc_ref[...])
        p_slc = d if p_slc is None else p_slc + d
    blk = lax.broadcasted_iota(jnp.int32, (qb, LANE), 1)
    cur = (q0 + lax.broadcasted_iota(jnp.int32, (qb, 1), 0)) // SEL_BLOCK
    forced = (blk == 0) | (blk > cur - N_LOCAL)
    score = jnp.where(forced, FORCE_SCORE, p_slc)
    score = jnp.where((blk <= cur) & (blk < ns), score, -jnp.inf)
    sel = _select_blocks(score, ns)
    pen = jnp.where(sel, 0.0, PEN)
    pen4 = jnp.concatenate([pen] * HPG, axis=0)
    lane = lax.broadcasted_iota(jnp.int32, (rows, LANE), 1)
    q_aug = jnp.where(lane < SEL_BLOCK, pen4,
                      jnp.where(lane == SEL_BLOCK, slope * SEL_BLOCK,
                                jnp.where(lane == SEL_BLOCK + 1, slope, 0.0)))
    qa = jnp.concatenate([qs_b, q_aug.astype(BF16)], axis=1)

    def sel_tile(kt, carry):
        m_run, l_run, acc = carry
        k0 = pl.multiple_of(kt * tk, tk)
        ka = jnp.concatenate([selk_ref[pl.ds(k0, tk), :].astype(BF16), kaug_ref[pl.ds(k0, tk), :]], axis=1)
        st = _dot_nt(qa, ka)
        kpos = k0 + lax.broadcasted_iota(jnp.int32, (1, tk), 1)
        valid = (st > PEN_TEST) & (kpos <= rpos)
        st = jnp.where(valid, st, NEG)
        m_new = jnp.maximum(m_run, jnp.max(st, axis=-1, keepdims=True))
        p = jnp.where(valid, jnp.exp(st - m_new), 0.0)
        alpha = jnp.exp(m_run - m_new)
        l_new = alpha * l_run + jnp.sum(p, axis=-1, keepdims=True)
        acc = alpha * acc + _dot(p.astype(BF16), selv_ref[pl.ds(k0, tk), :].astype(BF16))
        return m_new, l_new, acc

    n_tiles = (q0 + qb + tk - 1) // tk
    init = (jnp.full((rows, 1), NEG, F32), jnp.zeros((rows, 1), F32), jnp.zeros((rows, D_HEAD), F32))
    _, l_sel, acc_sel = lax.fori_loop(0, n_tiles, sel_tile, init)
    o_sel = acc_sel / jnp.where(l_sel > 0.0, l_sel, 1.0)

    wk = WINDOW + qb
    ks = pl.multiple_of(jnp.maximum(q0 - WINDOW, 0), qb)
    kpos = ks + lax.broadcasted_iota(jnp.int32, (1, wk), 1)
    w_dist = rpos - kpos
    w_valid = (w_dist >= 0) & (w_dist <= WINDOW)
    s = _dot_nt(qs_b, wink_ref[pl.ds(ks, wk), :].astype(BF16)) - slope * w_dist.astype(F32)
    p_win = _masked_softmax_rows(s, w_valid)
    o_win = _dot(p_win.astype(BF16), winv_ref[pl.ds(ks, wk), :].astype(BF16))

    gates = 1.0 / (1.0 + jnp.exp(-sm_ref[...]))

    def gate_col(branch):
        cols = []
        for h in range(HPG):
            c0 = SM_GATE + h * 3 + branch
            c1 = SM_GATE + (HPG + h) * 3 + branch
            cols.append(jnp.where(g == 0, gates[:, c0:c0 + 1], gates[:, c1:c1 + 1]))
        return jnp.concatenate(cols, axis=0)

    o = gate_col(0) * o_cmp + gate_col(1) * o_sel + gate_col(2) * o_win
    o = _head_rms(o)
    for h in range(HPG):
        o_ref[:, h * D_HEAD:(h + 1) * D_HEAD] = (
            o[h * qb:(h + 1) * qb] * gm_ref[:, h * D_HEAD:(h + 1) * D_HEAD]).astype(o_ref.dtype)


def _key_aug(t_len):
    pos = np.arange(t_len)[:, None]
    lane = np.arange(LANE)[None, :]
    a = np.where(lane < SEL_BLOCK, (pos // SEL_BLOCK == lane).astype(np.float32),
                 np.where(lane == SEL_BLOCK, (pos // SEL_BLOCK).astype(np.float32),
                          np.where(lane == SEL_BLOCK + 1, (pos % SEL_BLOCK).astype(np.float32), 0.0)))
    return jnp.asarray(a, dtype=BF16)


def nsa_prompt(z, cmp_kv, g_mix, *, qb=128, tk=512):
    b, t_len, _ = z.shape
    assert t_len // SEL_BLOCK <= SEL_BLOCK and t_len % tk == 0 and tk % qb == 0
    n_seg = cmp_kv.shape[2]
    gw = HPG * D_HEAD
    kv_blk = C_KV // D_HEAD

    def kv_spec(slot):
        return pl.BlockSpec((None, t_len, D_HEAD), lambda i, g, q: (i, 0, kv_blk + 2 * slot + g))

    return pl.pallas_call(
        functools.partial(_nsa_prompt_kernel, qb=qb, tk=tk),
        grid=(b, G_NSA, t_len // qb),
        in_specs=[pl.BlockSpec((None, qb, gw), lambda i, g, q: (i, q, g)),
                  pl.BlockSpec((None, qb, LANE), lambda i, g, q: (i, q, C_SM // LANE)),
                  kv_spec(2), kv_spec(3), kv_spec(4), kv_spec(5),
                  pl.BlockSpec((None, None, n_seg, D_HEAD), lambda i, g, q: (i, g, 0, 0)),
                  pl.BlockSpec((None, None, n_seg, D_HEAD), lambda i, g, q: (i, 2 + g, 0, 0)),
                  pl.BlockSpec((t_len, LANE), lambda i, g, q: (0, 0)),
                  pl.BlockSpec((n_seg, LANE), lambda i, g, q: (0, 0)),
                  pl.BlockSpec((1, gw), lambda i, g, q: (0, g))],
        out_specs=pl.BlockSpec((None, qb, gw), lambda i, g, q: (i, q, g)),
        out_shape=jax.ShapeDtypeStruct((b, t_len, H_NSA * D_HEAD), BF16),
        compiler_params=_cp(("arbitrary", "arbitrary", "arbitrary"), VMEM_BIG),
        name="nsa_prompt",
    )(z, z, z, z, z, z, cmp_kv, cmp_kv, _key_aug(t_len), _slc_matrix(n_seg, n_seg - 1, LANE), g_mix)


def _sb_prompt_kernel(q_ref, k_ref, v_ref, gm_ref, o_ref, *, qb):
    qi = pl.program_id(2)
    q0 = qi * qb
    q = (q_ref[...] * SCALE).astype(BF16)
    rpos = q0 + lax.broadcasted_iota(jnp.int32, (qb, 1), 0)
    tri = (lax.broadcasted_iota(jnp.int32, (qb, qb), 0) > lax.broadcasted_iota(jnp.int32, (qb, qb), 1))
    tri = jnp.where(tri, 1.0, 0.0).astype(BF16)

    def tile(it, carry):
        run, acc = carry
        k0 = pl.multiple_of((qi - it) * qb, qb)
        z = _dot_nt(q, k_ref[pl.ds(k0, qb), :].astype(BF16))
        kpos = k0 + lax.broadcasted_iota(jnp.int32, (1, qb), 1)
        mask = kpos < rpos
        lb = _log_sigmoid(z)
        lr = jnp.where(mask, lb - z, 0.0)
        hi, lo = _split2(lr)
        after = _dot(hi, tri) + _dot(lo, tri) + run
        a = jnp.where(mask, jnp.exp(lb + after), 0.0)
        acc = acc + _dot(a.astype(BF16), v_ref[pl.ds(k0, qb), :].astype(BF16))
        return after[:, 0:1] + lr[:, 0:1], acc

    init = (jnp.zeros((qb, 1), F32), jnp.zeros((qb, D_HEAD), F32))
    _, acc = lax.fori_loop(0, qi + 1, tile, init)
    o_ref[...] = (_head_rms(acc) * gm_ref[...]).astype(o_ref.dtype)


def sb_prompt(z, g_mix, *, qb=256):
    b, t_len, _ = z.shape
    blk = C_SB // D_HEAD
    gm_blk = (H_NSA * D_HEAD) // D_HEAD
    return pl.pallas_call(
        functools.partial(_sb_prompt_kernel, qb=qb),
        grid=(b, H_SB, t_len // qb),
        in_specs=[pl.BlockSpec((None, qb, D_HEAD), lambda i, h, q: (i, q, blk + h)),
                  pl.BlockSpec((None, t_len, D_HEAD), lambda i, h, q: (i, 0, blk + H_SB + h)),
                  pl.BlockSpec((None, t_len, D_HEAD), lambda i, h, q: (i, 0, blk + 2 * H_SB + h)),
                  pl.BlockSpec((1, D_HEAD), lambda i, h, q: (0, gm_blk + h))],
        out_specs=pl.BlockSpec((None, qb, D_HEAD), lambda i, h, q: (i, q, h)),
        out_shape=jax.ShapeDtypeStruct((b, t_len, H_SB * D_HEAD), BF16),
        compiler_params=_cp(("arbitrary", "arbitrary", "arbitrary")),
        name="sb_prompt",
    )(z, z, z, g_mix)


N_LEVELS = 6


def _gla_level_tables():
    c = GLA_CHUNK
    idx = np.arange(c)
    sums = np.zeros(((N_LEVELS + 1) * c, c), np.float32)
    pair = np.zeros((N_LEVELS + 1, c, c), np.float32)
    sums[:c] = (idx[None, :] <= idx[:, None])
    pair[0] = np.eye(c)
    for lv in range(N_LEVELS):
        mid = ((idx >> (lv + 1)) << (lv + 1)) + (1 << lv)
        upper = idx >= mid
        j = idx[None, :]
        in_up = upper[:, None] & (j >= mid[:, None]) & (j <= idx[:, None])
        in_lo = (~upper)[:, None] & (j > idx[:, None]) & (j <= mid[:, None] - 1)
        sums[(lv + 1) * c:(lv + 2) * c] = in_up | in_lo
        same = (idx[:, None] >> (lv + 1)) == (idx[None, :] >> (lv + 1))
        pair[lv + 1] = same & upper[:, None] & (~upper)[None, :]
    return jnp.asarray(sums, dtype=BF16), jnp.asarray(pair, dtype=F32)


def _gla_prompt_kernel(qg_ref, kg_ref, vg_ref, og_ref, sm_ref, wa2_ref, ba_ref, sums_ref, pair_ref,
                       gm_ref, o_ref, st_ref, state_ref, *, tb):
    nt = pl.program_id(1)
    c = GLA_CHUNK

    @pl.when(nt == 0)
    def _():
        state_ref[...] = jnp.zeros_like(state_ref)

    wa2 = wa2_ref[...].astype(BF16)
    sums = sums_ref[...]

    def chunk(ci, _):
        r0 = pl.multiple_of(ci * c, c)
        rows = pl.ds(r0, c)
        x = _dot(sm_ref[rows, :].astype(BF16), wa2) + ba_ref[...]
        loga = _log_sigmoid(x) * (1.0 / GLA_TAU)
        hi, lo = _split2(loga)
        dsum = _dot(sums, hi) + _dot(sums, lo)
        cb = dsum[0:c]
        q = qg_ref[rows, :] * (GLA_DK ** -0.5)
        k = kg_ref[rows, :]
        v = vg_ref[rows, :].astype(BF16)
        qf = [q.astype(BF16)]
        kf = [k.astype(BF16)]
        for lv in range(N_LEVELS):
            e = jnp.exp(dsum[(lv + 1) * c:(lv + 2) * c])
            qf.append((q * e).astype(BF16))
            kf.append((k * e).astype(BF16))
        c_last = cb[c - 1:c, :]
        q_in = (q * jnp.exp(cb)).astype(BF16)
        k_out = (k * jnp.exp(c_last - cb)).astype(BF16)
        decay = jnp.exp(c_last)
        for h in range(H_GLA):
            ks = slice(h * GLA_DK, (h + 1) * GLA_DK)
            vs = slice(h * GLA_DV, (h + 1) * GLA_DV)
            att = None
            for lv in range(N_LEVELS + 1):
                term = pair_ref[lv] * _dot_nt(qf[lv][:, ks], kf[lv][:, ks])
                att = term if att is None else att + term
            s_t = state_ref[h]
            o_h = _dot(att.astype(BF16), v[:, vs]) + _dot_nt(q_in[:, ks], s_t.astype(BF16))
            state_ref[h] = s_t * decay[:, ks] + _dot_tn(v[:, vs], k_out[:, ks])
            og = og_ref[rows, vs]
            o_ref[rows, vs] = (_head_rms(o_h) * _silu(og) * gm_ref[:, vs]).astype(o_ref.dtype)
        return 0

    lax.fori_loop(0, tb // c, chunk, 0)

    @pl.when(nt == pl.num_programs(1) - 1)
    def _():
        st_ref[...] = state_ref[...]


def _wa2_padded(gla_wa2):
    w = jnp.zeros((LANE, H_GLA * GLA_DK), F32)
    return w.at[SM_ALR:SM_ALR + GLA_RANK].set(gla_wa2)


def gla_prompt(z, gla_wa2, gla_ba, g_mix, *, tb=512):
    b, t_len, _ = z.shape
    kw = H_GLA * GLA_DK
    vw = H_GLA * GLA_DV
    sums, pair = _gla_level_tables()
    n_rows = sums.shape[0]
    return pl.pallas_call(
        functools.partial(_gla_prompt_kernel, tb=tb),
        grid=(b, t_len // tb),
        in_specs=[pl.BlockSpec((None, tb, kw), lambda i, n: (i, n, C_QG // kw)),
                  pl.BlockSpec((None, tb, kw), lambda i, n: (i, n, C_KG // kw)),
                  pl.BlockSpec((None, tb, vw), lambda i, n: (i, n, C_VG // vw)),
                  pl.BlockSpec((None, tb, vw), lambda i, n: (i, n, C_OG // vw)),
                  pl.BlockSpec((None, tb, LANE), lambda i, n: (i, n, C_SM // LANE)),
                  pl.BlockSpec((LANE, kw), lambda i, n: (0, 0)),
                  pl.BlockSpec((1, kw), lambda i, n: (0, 0)),
                  pl.BlockSpec((n_rows, GLA_CHUNK), lambda i, n: (0, 0)),
                  pl.BlockSpec((N_LEVELS + 1, GLA_CHUNK, GLA_CHUNK), lambda i, n: (0, 0, 0)),
                  pl.BlockSpec((1, vw), lambda i, n: (0, (H_NSA + H_SB) * D_HEAD // vw))],
        out_specs=[pl.BlockSpec((None, tb, vw), lambda i, n: (i, n, 0)),
                   pl.BlockSpec((None, H_GLA, GLA_DV, GLA_DK), lambda i, n: (i, 0, 0, 0))],
        out_shape=[jax.ShapeDtypeStruct((b, t_len, vw), BF16),
                   jax.ShapeDtypeStruct((b, H_GLA, GLA_DV, GLA_DK), F32)],
        scratch_shapes=[pltpu.VMEM((H_GLA, GLA_DV, GLA_DK), F32)],
        compiler_params=_cp(("arbitrary", "arbitrary")),
        name="gla_prompt",
    )(z, z, z, z, z, _wa2_padded(gla_wa2), gla_ba.reshape(1, kw), sums, pair, g_mix)


PAGES_PER_STEP = 8


def _gather_segments_kernel(pt_ref, *refs):
    page_refs = refs[:PAGES_PER_STEP]
    o_ref = refs[PAGES_PER_STEP]
    col_ref = refs[PAGES_PER_STEP + 1]
    seg_per_page = page_refs[0].shape[0] // CMP_STRIDE
    for pair in range(PAGES_PER_STEP // 2):
        r0 = pair * 2 * seg_per_page
        for u in range(2):
            for cg in range(4):
                col_ref[u, cg] = page_refs[2 * pair + u][:, cg * D_HEAD:(cg + 1) * D_HEAD]
        for cg in range(4):
            for p in range(CMP_STRIDE):
                rows = [col_ref[u, cg, pl.ds(p, seg_per_page, stride=CMP_STRIDE), :] for u in range(2)]
                o_ref[cg, r0:r0 + 2 * seg_per_page, p * D_HEAD:(p + 1) * D_HEAD] = (
                    jnp.concatenate(rows, axis=0).astype(o_ref.dtype))


def gather_segments(cache_nsa, page_table, layer):
    depth, n_pool, page, _, _, _ = cache_nsa.shape
    b, n_pages = page_table.shape
    cache = cache_nsa.reshape(depth, n_pool, page, 4 * G_NSA * D_HEAD)
    half = 2 * G_NSA * D_HEAD
    seg_per_page = page // CMP_STRIDE
    n_seg = n_pages * seg_per_page
    steps = n_pages // PAGES_PER_STEP

    def page_spec(u):
        return pl.BlockSpec((None, None, page, half),
                            lambda i, s, pt: (layer, pt[i * n_pages + s * PAGES_PER_STEP + u], 0, 0))

    return pl.pallas_call(
        _gather_segments_kernel,
        grid_spec=pltpu.PrefetchScalarGridSpec(
            num_scalar_prefetch=1,
            grid=(b, steps),
            in_specs=[page_spec(u) for u in range(PAGES_PER_STEP)],
            out_specs=pl.BlockSpec((None, 4, PAGES_PER_STEP * seg_per_page, CMP_STRIDE * D_HEAD),
                                   lambda i, s, pt: (i, 0, s, 0)),
            scratch_shapes=[pltpu.VMEM((2, 4, page, D_HEAD), F32)],
        ),
        out_shape=jax.ShapeDtypeStruct((b, 4, n_seg, CMP_STRIDE * D_HEAD), BF16),
        compiler_params=_cp(("arbitrary", "arbitrary")),
        name="gather_segments",
    )(page_table.reshape(-1), *([cache] * PAGES_PER_STEP))


def _nsa_sample_select_kernel(q_ref, cmp_ref, mslc_ref, ocmp_ref, idx_ref, *, pos, n_blk_lanes):
    n_seg = cmp_ref.shape[1]
    nc = n_seg - 1
    ns = pos // SEL_BLOCK + 1
    cur = pos // SEL_BLOCK
    q = (q_ref[...] * SCALE).astype(BF16)
    row = lax.broadcasted_iota(jnp.int32, (H_NSA, 1), 0)
    slope = _slope_column(0, 1)
    slope = jnp.concatenate([slope, _slope_column(1, 1)], axis=0)
    s = jnp.where(row < HPG, _dot_nt(q, cmp_ref[0].astype(BF16)), _dot_nt(q, cmp_ref[1].astype(BF16)))
    ci = lax.broadcasted_iota(jnp.int32, (1, n_seg), 1)
    c_dist = pos - (ci * CMP_STRIDE + (CMP_BLOCK - 1))
    p = _masked_softmax_rows(s - slope * c_dist.astype(F32), (c_dist >= 0) & (ci < nc))
    pb = p.astype(BF16)
    ocmp_ref[...] = jnp.where(row < HPG, _dot(pb, cmp_ref[2].astype(BF16)), _dot(pb, cmp_ref[3].astype(BF16)))

    imp = jnp.concatenate([jnp.sum(p[g * HPG:(g + 1) * HPG], axis=0, keepdims=True) for g in range(G_NSA)]
                          + [jnp.zeros((H_NSA - G_NSA, n_seg), F32)], axis=0)
    p_slc = None
    for part in _split3(imp):
        d = _dot(part, mslc_ref[...])
        p_slc = d if p_slc is None else p_slc + d
    blk = lax.broadcasted_iota(jnp.int32, p_slc.shape, 1)
    forced = (blk == 0) | (blk > cur - N_LOCAL)
    score = jnp.where(forced, FORCE_SCORE, p_slc)
    score = jnp.where(blk <= cur, score, -jnp.inf)
    sel = _select_blocks(score, ns)
    upper = (lax.broadcasted_iota(jnp.int32, (n_blk_lanes, n_blk_lanes), 0)
             < lax.broadcasted_iota(jnp.int32, (n_blk_lanes, n_blk_lanes), 1))
    before = _dot(jnp.where(sel, 1.0, 0.0).astype(BF16), jnp.where(upper, 1.0, 0.0).astype(BF16))
    blk_f = blk.astype(F32)
    out_lane = lax.broadcasted_iota(jnp.int32, (H_NSA, LANE), 1)
    out = jnp.zeros((H_NSA, LANE), F32)
    for n in range(N_SELECT):
        v = jnp.sum(jnp.where(sel & (before == float(n)), blk_f, 0.0), axis=-1, keepdims=True)
        out = jnp.where(out_lane == n, v, out)
    idx_ref[...] = out.astype(jnp.int32)


def nsa_sample_select(q8, cmp_kv, *, pos):
    b = q8.shape[0]
    n_seg = cmp_kv.shape[2]
    ns = pos // SEL_BLOCK + 1
    assert ns >= N_SELECT
    n_blk_lanes = -(-ns // LANE) * LANE
    return pl.pallas_call(
        functools.partial(_nsa_sample_select_kernel, pos=pos, n_blk_lanes=n_blk_lanes),
        grid=(b,),
        in_specs=[pl.BlockSpec((None, H_NSA, D_HEAD), lambda i: (i, 0, 0)),
                  pl.BlockSpec((None, 4, n_seg, D_HEAD), lambda i: (i, 0, 0, 0)),
                  pl.BlockSpec((n_seg, n_blk_lanes), lambda i: (0, 0))],
        out_specs=[pl.BlockSpec((None, H_NSA, D_HEAD), lambda i: (i, 0, 0)),
                   pl.BlockSpec((None, H_NSA, LANE), lambda i: (i, 0, 0))],
        out_shape=[jax.ShapeDtypeStruct((b, H_NSA, D_HEAD), F32),
                   jax.ShapeDtypeStruct((b, H_NSA, LANE), jnp.int32)],
        compiler_params=_cp(("arbitrary",)),
        name="nsa_sample_select",
    )(q8, cmp_kv, _slc_matrix(n_seg, n_seg - 1, n_blk_lanes))


def _softmax_with_new(s, valid, s_new, v_mat, v_new):
    s = jnp.where(valid, s, NEG)
    m = jnp.maximum(jnp.max(s, axis=-1, keepdims=True), s_new)
    e = jnp.where(valid, jnp.exp(s - m), 0.0)
    e_new = jnp.exp(s_new - m)
    d = jnp.sum(e, axis=-1, keepdims=True) + e_new
    return (_dot(e.astype(BF16), v_mat) + e_new.astype(BF16).astype(F32) * v_new) / d


def _nsa_sample_attend_kernel(idx_ref, pt_ref, *refs, pos):
    selk_refs = refs[:N_SELECT]
    selv_refs = refs[N_SELECT:2 * N_SELECT]
    (q_ref, nk_ref, nv_ref, nwk_ref, nwv_ref, sm_ref, wk_ref, wv_ref, ocmp_ref, gm_ref, o_ref) = refs[2 * N_SELECT:]
    b = pl.program_id(0)
    g = pl.program_id(1)
    n_past_blocks = pos // SEL_BLOCK
    q = (q_ref[...] * SCALE).astype(BF16)
    qf = q.astype(F32)
    slope = jnp.concatenate([_slope_column(0, 1), _slope_column(1, 1)], axis=0)

    def new_row(ref):
        return ref[pl.ds(b, 1), :].astype(BF16)

    k_all = jnp.concatenate([r[...].astype(BF16) for r in selk_refs], axis=0)
    v_all = jnp.concatenate([r[...].astype(BF16) for r in selv_refs], axis=0)
    width = N_SELECT * SEL_BLOCK
    lane = lax.broadcasted_iota(jnp.int32, (1, width), 1)
    tok = jnp.zeros((1, width), jnp.int32)
    in_cache = jnp.zeros((1, width), jnp.bool_)
    for n in range(N_SELECT):
        blk_id = idx_ref[(b * G_NSA + g) * N_SELECT + n]
        here = (lane // SEL_BLOCK) == n
        tok = jnp.where(here, blk_id * SEL_BLOCK + lane % SEL_BLOCK, tok)
        in_cache = in_cache | (here & (blk_id < n_past_blocks))
    dist = pos - tok
    s = _dot_nt(q, k_all) - slope * dist.astype(F32)
    k_new = new_row(nk_ref)
    s_new = jnp.sum(qf * k_new.astype(F32), axis=-1, keepdims=True)
    o_sel = _softmax_with_new(s, in_cache & (dist >= 0), s_new, v_all, new_row(nv_ref).astype(F32))

    n_win = wk_ref.shape[0]
    wi = lax.broadcasted_iota(jnp.int32, (1, n_win), 1)
    w_dist = n_win - wi
    s = _dot_nt(q, wk_ref[...].astype(BF16)) - slope * w_dist.astype(F32)
    s_new = jnp.sum(qf * new_row(nwk_ref).astype(F32), axis=-1, keepdims=True)
    o_win = _softmax_with_new(s, w_dist <= WINDOW, s_new, wv_ref[...].astype(BF16), new_row(nwv_ref).astype(F32))

    gates = 1.0 / (1.0 + jnp.exp(-sm_ref[pl.ds(b, 1), :]))
    hrow = lax.broadcasted_iota(jnp.int32, (H_NSA, LANE), 0)
    glane = lax.broadcasted_iota(jnp.int32, (H_NSA, LANE), 1)

    def gate_col(branch):
        return jnp.sum(jnp.where(glane == SM_GATE + 3 * hrow + branch, gates, 0.0), axis=-1, keepdims=True)

    o = gate_col(0) * ocmp_ref[...] + gate_col(1) * o_sel + gate_col(2) * o_win
    o = _head_rms(o) * gm_ref[...]
    o_ref[...] = jnp.where(g == 0, o[0:HPG], o[HPG:H_NSA])


def nsa_sample_attend(zs, q8, o_cmp, idx, cache_nsa, state_win, page_table, gm8, layer, *, pos):
    depth, n_pool, page, _, _, _ = cache_nsa.shape
    b, n_pages = page_table.shape
    halves = page // SEL_BLOCK
    cache = cache_nsa.reshape(depth, n_pool * halves, SEL_BLOCK, 4 * G_NSA * D_HEAD)
    n_win = state_win.shape[2]
    win = state_win.reshape(depth, b, n_win, 2 * G_NSA * D_HEAD)
    last_blk = pos // SEL_BLOCK - 1
    kv_blk = C_KV // D_HEAD

    def sel_spec(n, col):
        def index_map(i, g, idx_ref, pt_ref):
            blk_id = jnp.minimum(idx_ref[(i * G_NSA + g) * N_SELECT + n], last_blk)
            return (layer, pt_ref[i * n_pages + blk_id // halves] * halves + blk_id % halves, 0, col + g)
        return pl.BlockSpec((None, None, SEL_BLOCK, D_HEAD), index_map)

    def zs_spec(slot):
        return pl.BlockSpec((MS, D_HEAD), lambda i, g, a, c: (0, kv_blk + 2 * slot + g))

    in_specs = ([sel_spec(n, 2 * G_NSA) for n in range(N_SELECT)]
                + [sel_spec(n, 3 * G_NSA) for n in range(N_SELECT)]
                + [pl.BlockSpec((None, H_NSA, D_HEAD), lambda i, g, a, c: (i, 0, 0)),
                   zs_spec(2), zs_spec(3), zs_spec(4), zs_spec(5),
                   pl.BlockSpec((MS, LANE), lambda i, g, a, c: (0, C_SM // LANE)),
                   pl.BlockSpec((None, None, n_win, D_HEAD), lambda i, g, a, c: (layer, i, 0, g)),
                   pl.BlockSpec((None, None, n_win, D_HEAD), lambda i, g, a, c: (layer, i, 0, G_NSA + g)),
                   pl.BlockSpec((None, H_NSA, D_HEAD), lambda i, g, a, c: (i, 0, 0)),
                   pl.BlockSpec((H_NSA, D_HEAD), lambda i, g, a, c: (0, 0))])
    return pl.pallas_call(
        functools.partial(_nsa_sample_attend_kernel, pos=pos),
        grid_spec=pltpu.PrefetchScalarGridSpec(
            num_scalar_prefetch=2,
            grid=(b, G_NSA),
            in_specs=in_specs,
            out_specs=pl.BlockSpec((None, None, HPG, D_HEAD), lambda i, g, a, c: (i, g, 0, 0)),
        ),
        out_shape=jax.ShapeDtypeStruct((b, G_NSA, HPG, D_HEAD), F32),
        compiler_params=_cp(("arbitrary", "arbitrary")),
        name="nsa_sample_attend",
    )(idx, page_table.reshape(-1), *([cache] * (2 * N_SELECT)), q8, zs, zs, zs, zs, zs, win, win, o_cmp, gm8)


def _sb_sample_kernel(pt_ref, *refs):
    page_refs = refs[:PAGES_PER_STEP]
    q_ref, gm_ref, o_ref, run_ref, acc_ref = refs[PAGES_PER_STEP:]
    b = pl.program_id(0)
    s_idx = pl.program_id(1)
    page = page_refs[0].shape[0]
    kw = H_SB * D_HEAD

    @pl.when(s_idx == 0)
    def _():
        run_ref[...] = jnp.zeros_like(run_ref)
        acc_ref[...] = jnp.zeros_like(acc_ref)

    qrow = q_ref[pl.ds(b, 1), :] * SCALE
    hrow = lax.broadcasted_iota(jnp.int32, (8, kw), 0)
    hlane = lax.broadcasted_iota(jnp.int32, (8, kw), 1) // D_HEAD
    qm = jnp.where(hrow == hlane, qrow, 0.0).astype(BF16)
    tri = (lax.broadcasted_iota(jnp.int32, (page, page), 0) > lax.broadcasted_iota(jnp.int32, (page, page), 1))
    tri = jnp.where(tri, 1.0, 0.0).astype(BF16)
    run = run_ref[:, 0:1]
    acc = acc_ref[...]
    for u in range(PAGES_PER_STEP - 1, -1, -1):
        z = _dot_nt(qm, page_refs[u][:, 0:kw].astype(BF16))
        lb = _log_sigmoid(z)
        lr = lb - z
        hi, lo = _split2(lr)
        after = _dot(hi, tri) + _dot(lo, tri) + run
        a = jnp.exp(lb + after)
        acc = acc + _dot(a.astype(BF16), page_refs[u][:, kw:2 * kw].astype(BF16))
        run = after[:, 0:1] + lr[:, 0:1]
    run_ref[...] = jnp.broadcast_to(run, run_ref.shape)
    acc_ref[...] = acc

    @pl.when(s_idx == pl.num_programs(1) - 1)
    def _():
        o = jnp.concatenate([acc[h:h + 1, h * D_HEAD:(h + 1) * D_HEAD] for h in range(H_SB)], axis=0)
        o_ref[...] = _head_rms(o) * gm_ref[...]


def sb_sample(zs, cache_sb, page_table, gm4, layer):
    depth, n_pool, page, _, _, _ = cache_sb.shape
    b, n_pages = page_table.shape
    cache = cache_sb.reshape(depth, n_pool, page, 2 * H_SB * D_HEAD)
    steps = n_pages // PAGES_PER_STEP

    def page_spec(u):
        return pl.BlockSpec(
            (None, None, page, 2 * H_SB * D_HEAD),
            lambda i, s, pt: (layer, pt[i * n_pages + (steps - 1 - s) * PAGES_PER_STEP + u], 0, 0))

    return pl.pallas_call(
        _sb_sample_kernel,
        grid_spec=pltpu.PrefetchScalarGridSpec(
            num_scalar_prefetch=1,
            grid=(b, steps),
            in_specs=[page_spec(u) for u in range(PAGES_PER_STEP)]
            + [pl.BlockSpec((MS, H_SB * D_HEAD), lambda i, s, pt: (0, C_SB // (H_SB * D_HEAD))),
               pl.BlockSpec((H_SB, D_HEAD), lambda i, s, pt: (0, 0))],
            out_specs=pl.BlockSpec((None, H_SB, D_HEAD), lambda i, s, pt: (i, 0, 0)),
            scratch_shapes=[pltpu.VMEM((8, LANE), F32), pltpu.VMEM((8, H_SB * D_HEAD), F32)],
        ),
        out_shape=jax.ShapeDtypeStruct((b, H_SB, D_HEAD), F32),
        compiler_params=_cp(("arbitrary", "arbitrary")),
        name="sb_sample",
    )(page_table.reshape(-1), *([cache] * PAGES_PER_STEP), zs, gm4)


def _gla_sample_kernel(qg_ref, kg_ref, vg_ref, og_ref, sm_ref, wa2_ref, ba_ref, st_ref, gm_ref,
                       o_ref, ns_ref, *, n_b):
    x = _dot(sm_ref[...].astype(BF16), wa2_ref[...].astype(BF16)) + ba_ref[...]
    decay = jnp.exp(_log_sigmoid(x) * (1.0 / GLA_TAU))
    eye = (lax.broadcasted_iota(jnp.int32, (GLA_DK, GLA_DK), 0)
           == lax.broadcasted_iota(jnp.int32, (GLA_DK, GLA_DK), 1))

    def column(row):
        return jnp.sum(jnp.where(eye, row, 0.0), axis=1, keepdims=True)

    o_ref[...] = jnp.zeros_like(o_ref)
    for b in range(n_b):
        for h in range(H_GLA):
            ks = slice(h * GLA_DK, (h + 1) * GLA_DK)
            vs = slice(h * GLA_DV, (h + 1) * GLA_DV)
            s_new = (column(decay[b:b + 1, ks]) * st_ref[b, h]
                     + column(kg_ref[b:b + 1, ks]) * vg_ref[b:b + 1, vs])
            ns_ref[b, h] = s_new
            q_col = column(qg_ref[b:b + 1, ks] * (GLA_DK ** -0.5))
            o = jnp.sum(q_col * s_new, axis=0, keepdims=True)
            o_ref[b:b + 1, vs] = _head_rms(o) * _silu(og_ref[b:b + 1, vs]) * gm_ref[:, vs]


def gla_sample(zs, gla_wa2, gla_ba, state, g_mix):
    n_b = state.shape[0]
    kw = H_GLA * GLA_DK
    vw = H_GLA * GLA_DV
    return pl.pallas_call(
        functools.partial(_gla_sample_kernel, n_b=n_b),
        grid=(1,),
        in_specs=[pl.BlockSpec((MS, kw), lambda i: (0, C_QG // kw)),
                  pl.BlockSpec((MS, kw), lambda i: (0, C_KG // kw)),
                  pl.BlockSpec((MS, vw), lambda i: (0, C_VG // vw)),
                  pl.BlockSpec((MS, vw), lambda i: (0, C_OG // vw)),
                  pl.BlockSpec((MS, LANE), lambda i: (0, C_SM // LANE)),
                  pl.BlockSpec((LANE, kw), lambda i: (0, 0)),
                  pl.BlockSpec((1, kw), lambda i: (0, 0)),
                  pl.BlockSpec(state.shape, lambda i: (0, 0, 0, 0)),
                  pl.BlockSpec((1, vw), lambda i: (0, (H_NSA + H_SB) * D_HEAD // vw))],
        out_specs=[pl.BlockSpec((MS, vw), lambda i: (0, 0)),
                   pl.BlockSpec(state.shape, lambda i: (0, 0, 0, 0))],
        out_shape=[jax.ShapeDtypeStruct((MS, vw), F32), jax.ShapeDtypeStruct(state.shape, F32)],
        compiler_params=_cp(("arbitrary",)),
        name="gla_sample",
    )(zs, zs, zs, zs, zs, _wa2_padded(gla_wa2), gla_ba.reshape(1, kw), state, g_mix)


def _reorder_w_in(w_in):
    o = np.cumsum([0, 1024, 24, 1536, 1536, 256, 256, 512, 16, 512])
    seg = [w_in[..., o[i]:o[i + 1]] for i in range(9)]
    q_n, gate, kv, sb, qg, kg, vg, alr, og = seg
    pad = jnp.zeros(w_in.shape[:-1] + (N_Z - C_SM - 40,), w_in.dtype)
    return jnp.concatenate([q_n, kv, sb, qg, kg, vg, og, gate, alr, pad], axis=-1).astype(BF16)


def kernel(x_prompt, x_sample, cache_nsa, cache_sb, state_win, state_gla, page_table, c_prompt, c_sample, norm1, norm2, w_ada, b_ada, w_in, gla_wa2, gla_ba, cmp_pe, cmp_w1, cmp_w2, g_mix, w_out, ffn_w1, ffn_w3, ffn_w2, final_norm):
    depth = w_in.shape[0]
    bp, t_len, d = x_prompt.shape
    bs = x_sample.shape[0]
    n_pages = page_table.shape[1]
    page = cache_nsa.shape[2]
    past = n_pages * page
    wbuf = state_win.shape[2]
    mp = bp * t_len
    nsa_w = 4 * G_NSA * D_HEAD
    sb_w = 2 * H_SB * D_HEAD
    win_w = 2 * G_NSA * D_HEAD
    assert bs <= MS and x_sample.shape[1] == 1 and wbuf == WINDOW and t_len >= wbuf

    c_all = jnp.zeros((MS, d), F32).at[:bs].set(c_sample).at[bs:bs + bp].set(c_prompt)
    mod = matmul_bias(jnp.broadcast_to(c_all, (depth, MS, d)), w_ada, b_ada[:, None, :],
                      tm=MS, tn=1024, silu_a=True)
    w_in_r = _reorder_w_in(w_in)
    zero_bias = jnp.zeros((1, 1, N_Z), F32)

    xp = x_prompt.reshape(mp, d)
    xs = jnp.zeros((MS, d), F32).at[:bs].set(x_sample[:, 0])
    outs = {k: [] for k in ("nsa_p", "nsa_s", "sb_p", "sb_s", "win_p", "win_s", "gla_p", "gla_s")}

    for l in range(depth):
        def mod_p(k):
            return mod[l, bs:bs + bp, None, k * d:(k + 1) * d]

        def mod_s(k):
            return mod[l, None, :, k * d:(k + 1) * d]

        gm = g_mix[l][None, :]
        h = norm_mod(xp, norm1[l][None, :], mod_p(1), mod_p(0), tm=512)
        z = matmul_bias(h[None], w_in_r[l][None], zero_bias, tm=2048, tn=640)[0]
        z3 = z.reshape(bp, t_len, N_Z)
        seg = z3[..., C_KV:C_KV + 4 * D_HEAD].reshape(bp, t_len // CMP_STRIDE, CMP_STRIDE, 4, D_HEAD)
        seg = jnp.transpose(seg, (0, 3, 1, 2, 4)).reshape(bp, 4, t_len // CMP_STRIDE, CMP_STRIDE * D_HEAD)
        cmp_kv = compress_segments(seg, cmp_w1[l], cmp_pe[l], cmp_w2[l])
        o_nsa = nsa_prompt(z3, cmp_kv, gm)
        o_sb = sb_prompt(z3, gm)
        o_gla, st_t = gla_prompt(z3, gla_wa2[l], gla_ba[l], gm)
        xp = matmul_resid([o_nsa.reshape(mp, -1), o_sb.reshape(mp, -1), o_gla.reshape(mp, -1)],
                          w_out[l], xp, mod_p(2), tm=2048, tn=512)
        h = norm_mod(xp, norm2[l][None, :], mod_p(4), mod_p(3), tm=512)
        hid = matmul_swiglu(h, ffn_w1[l], ffn_w3[l], tm=2048, tn=512)
        xp = matmul_resid([hid], ffn_w2[l], xp, mod_p(5), tm=1024, tn=512, tk=2816)
        outs["nsa_p"].append(z3[..., C_KV:C_KV + nsa_w].reshape(bp, t_len, 4, G_NSA, D_HEAD))
        outs["sb_p"].append(z3[..., C_SB + H_SB * D_HEAD:C_SB + H_SB * D_HEAD + sb_w]
                            .reshape(bp, t_len, 2, H_SB, D_HEAD))
        outs["win_p"].append(z3[:, t_len - wbuf:, C_KV + nsa_w:C_KV + nsa_w + win_w]
                             .reshape(bp, wbuf, 2, G_NSA, D_HEAD))
        outs["gla_p"].append(jnp.swapaxes(st_t, 2, 3))

        hs = norm_mod(xs, norm1[l][None, :], mod_s(1), mod_s(0), tm=MS)
        zs = matmul_bias(hs[None], w_in_r[l][None], zero_bias, tm=MS, tn=1152)[0]
        seg_s = gather_segments(cache_nsa, page_table, l)
        cmp_s = compress_segments(seg_s, cmp_w1[l], cmp_pe[l], cmp_w2[l])
        q8 = zs[:bs, C_Q:C_Q + H_NSA * D_HEAD].reshape(bs, H_NSA, D_HEAD)
        o_cmp, idx = nsa_sample_select(q8, cmp_s, pos=past)
        idx_flat = idx[:, :G_NSA, :N_SELECT].reshape(-1)
        o_nsa_s = nsa_sample_attend(zs, q8, o_cmp, idx_flat, cache_nsa, state_win, page_table,
                                    gm[0, :H_NSA * D_HEAD].reshape(H_NSA, D_HEAD), l, pos=past)
        o_sb_s = sb_sample(zs, cache_sb, page_table,
                           gm[0, H_NSA * D_HEAD:(H_NSA + H_SB) * D_HEAD].reshape(H_SB, D_HEAD), l)
        o_gla_s, st_new = gla_sample(zs, gla_wa2[l], gla_ba[l], state_gla[l], gm)

        def pad_rows(a):
            return jnp.zeros((MS, a.shape[1]), BF16).at[:bs].set(a.astype(BF16))

        xs = matmul_resid([pad_rows(o_nsa_s.reshape(bs, -1)), pad_rows(o_sb_s.reshape(bs, -1)),
                           o_gla_s.astype(BF16)], w_out[l], xs, mod_s(2), tm=MS, tn=512)
        hs = norm_mod(xs, norm2[l][None, :], mod_s(4), mod_s(3), tm=MS)
        hid_s = matmul_swiglu(hs, ffn_w1[l], ffn_w3[l], tm=MS, tn=512)
        xs = matmul_resid([hid_s], ffn_w2[l], xs, mod_s(5), tm=MS, tn=512, tk=2816)
        outs["nsa_s"].append(zs[:bs, C_KV:C_KV + nsa_w].reshape(bs, 1, 4, G_NSA, D_HEAD))
        outs["sb_s"].append(zs[:bs, C_SB + H_SB * D_HEAD:C_SB + H_SB * D_HEAD + sb_w].reshape(bs, 1, 2, H_SB, D_HEAD))
        win_new = zs[:bs, C_KV + nsa_w:C_KV + nsa_w + win_w].reshape(bs, 1, 2, G_NSA, D_HEAD)
        outs["win_s"].append(jnp.concatenate([state_win[l][:, 1:], win_new], axis=1))
        outs["gla_s"].append(st_new.astype(state_gla.dtype))

    y_prompt = rmsnorm_rows(xp, final_norm[None, :], tm=512).reshape(bp, t_len, d)
    y_sample = rmsnorm_rows(xs, final_norm[None, :], tm=MS)[:bs].reshape(bs, 1, d)
    return (y_prompt, y_sample, jnp.stack(outs["nsa_p"]), jnp.stack(outs["nsa_s"]), jnp.stack(outs["sb_p"]),
            jnp.stack(outs["sb_s"]), jnp.stack(outs["win_p"]), jnp.stack(outs["win_s"]),
            jnp.stack(outs["gla_p"]), jnp.stack(outs["gla_s"]))
```

```python
import functools

import numpy as np
import jax
import jax.numpy as jnp
from jax import lax
from jax.experimental import pallas as pl
from jax.experimental.pallas import tpu as pltpu

F32 = jnp.float32
BF16 = jnp.bfloat16

EPS = 1e-6
D_HEAD = 128
H_NSA = 8
G_NSA = 2
HPG = 4
H_SB = 4
H_GLA = 4
GLA_DK = 64
GLA_DV = 128
GLA_RANK = 16
GLA_TAU = 16.0
GLA_CHUNK = 64
CMP_BLOCK = 32
CMP_STRIDE = 16
SEL_BLOCK = 64
N_SELECT = 16
N_LOCAL = 2
WINDOW = 512
FORCE_SCORE = 1e6
SCALE = D_HEAD ** -0.5
LANE = 128
MS = 16

C_Q = 0
C_KV = 1024
C_SB = 2560
C_QG = 4096
C_KG = 4352
C_VG = 4608
C_OG = 5120
C_SM = 5632
N_Z = 5760
SM_GATE = 0
SM_ALR = 24

NEG = -1e30
PEN = -(2.0 ** 100)
PEN_TEST = -(2.0 ** 90)

VMEM_BIG = 56 * 1024 * 1024


def _cp(sem, vmem=None):
    return pltpu.CompilerParams(dimension_semantics=sem, vmem_limit_bytes=vmem)


def _log_sigmoid(x):
    return jnp.minimum(x, 0.0) - jnp.log(1.0 + jnp.exp(-jnp.abs(x)))


def _silu(x):
    return x * (1.0 / (1.0 + jnp.exp(-x)))


def _split2(x):
    hi = x.astype(BF16)
    lo = (x - hi.astype(F32)).astype(BF16)
    return hi, lo


def _split3(x):
    hi = x.astype(BF16)
    r = x - hi.astype(F32)
    mid = r.astype(BF16)
    lo = (r - mid.astype(F32)).astype(BF16)
    return hi, mid, lo


def _dot(a, b):
    return jnp.dot(a, b, preferred_element_type=F32)


def _dot_nt(a, b):
    return lax.dot_general(a, b, (((1,), (1,)), ((), ())), preferred_element_type=F32)


def _dot_tn(a, b):
    return lax.dot_general(a, b, (((0,), (0,)), ((), ())), preferred_element_type=F32)


def _mm_plain_kernel(a_ref, w_ref, b_ref, o_ref, *, silu_a):
    a = a_ref[...]
    if silu_a:
        a = _silu(a.astype(F32))
    o_ref[...] = _dot(a.astype(BF16), w_ref[...].astype(BF16)) + b_ref[...]


def matmul_bias(a, w, b, *, tm, tn, silu_a=False):
    nl, m, k = a.shape
    n = w.shape[2]
    return pl.pallas_call(
        functools.partial(_mm_plain_kernel, silu_a=silu_a),
        grid=(nl, m // tm, n // tn),
        in_specs=[pl.BlockSpec((None, tm, k), lambda l, i, j: (l, i, 0)),
                  pl.BlockSpec((None, k, tn), lambda l, i, j: (l, 0, j)),
                  pl.BlockSpec((None, 1, tn), lambda l, i, j: (l, 0, j))],
        out_specs=pl.BlockSpec((None, tm, tn), lambda l, i, j: (l, i, j)),
        out_shape=jax.ShapeDtypeStruct((nl, m, n), F32),
        compiler_params=_cp(("arbitrary", "arbitrary", "arbitrary"), VMEM_BIG),
        name="matmul_bias",
    )(a, w, b)


def _mm_swiglu_kernel(a_ref, w1_ref, w3_ref, o_ref):
    a = a_ref[...]
    u = _dot(a, w1_ref[...].astype(BF16))
    v = _dot(a, w3_ref[...].astype(BF16))
    o_ref[...] = (_silu(u) * v).astype(o_ref.dtype)


def matmul_swiglu(a, w1, w3, *, tm, tn):
    m, k = a.shape
    n = w1.shape[1]
    return pl.pallas_call(
        _mm_swiglu_kernel,
        grid=(m // tm, n // tn),
        in_specs=[pl.BlockSpec((tm, k), lambda i, j: (i, 0)),
                  pl.BlockSpec((k, tn), lambda i, j: (0, j)),
                  pl.BlockSpec((k, tn), lambda i, j: (0, j))],
        out_specs=pl.BlockSpec((tm, tn), lambda i, j: (i, j)),
        out_shape=jax.ShapeDtypeStruct((m, n), BF16),
        compiler_params=_cp(("arbitrary", "arbitrary"), VMEM_BIG),
        name="matmul_swiglu",
    )(a, w1, w3)


def _mm_resid_kernel(*refs, a_widths, nk):
    n_a = len(a_widths)
    a_refs = refs[:n_a]
    w_ref, x_ref, g_ref, o_ref = refs[n_a:n_a + 4]
    part = None
    off = 0
    for a_ref, kw in zip(a_refs, a_widths):
        d = _dot(a_ref[...], w_ref[off:off + kw, :].astype(BF16))
        part = d if part is None else part + d
        off += kw
    if nk == 1:
        o_ref[...] = x_ref[...] + g_ref[...] * part
        return
    acc_ref = refs[n_a + 4]
    k = pl.program_id(2)

    @pl.when(k == 0)
    def _():
        acc_ref[...] = part

    @pl.when(k > 0)
    def _():
        acc_ref[...] += part

    @pl.when(k == nk - 1)
    def _():
        o_ref[...] = x_ref[...] + g_ref[...] * acc_ref[...]


def matmul_resid(a_list, w, x, gate, *, tm, tn, tk=None):
    m, n = x.shape
    k_total = w.shape[0]
    a_widths = tuple(a.shape[1] for a in a_list)
    if tk is None:
        tk = k_total
    nk = k_total // tk
    assert nk == 1 or len(a_list) == 1
    g_cnt, g_rows, _ = gate.shape
    rows_per_gate = m // g_cnt
    if nk == 1:
        a_specs = [pl.BlockSpec((tm, kw), lambda i, j, k: (i, 0)) for kw in a_widths]
        kernel_widths = a_widths
    else:
        a_specs = [pl.BlockSpec((tm, tk), lambda i, j, k: (i, k))]
        kernel_widths = (tk,)
    g_blk = 1 if g_rows == 1 else tm
    return pl.pallas_call(
        functools.partial(_mm_resid_kernel, a_widths=kernel_widths, nk=nk),
        grid=(m // tm, n // tn, nk),
        in_specs=a_specs + [
            pl.BlockSpec((tk, tn), lambda i, j, k: (k, j)),
            pl.BlockSpec((tm, tn), lambda i, j, k: (i, j)),
            pl.BlockSpec((None, g_blk, tn), lambda i, j, k: ((i * tm) // rows_per_gate, 0, j)),
        ],
        out_specs=pl.BlockSpec((tm, tn), lambda i, j, k: (i, j)),
        out_shape=jax.ShapeDtypeStruct((m, n), F32),
        scratch_shapes=[pltpu.VMEM((tm, tn), F32)] if nk > 1 else [],
        compiler_params=_cp(("arbitrary", "arbitrary", "arbitrary"), VMEM_BIG),
        name="matmul_resid",
    )(*a_list, w, x, gate)


def _norm_mod_kernel(x_ref, g_ref, sc_ref, sh_ref, o_ref):
    x = x_ref[...]
    y = x * lax.rsqrt(jnp.mean(x * x, axis=-1, keepdims=True) + EPS) * g_ref[...]
    o_ref[...] = (y * (1.0 + sc_ref[...]) + sh_ref[...]).astype(o_ref.dtype)


def norm_mod(x, g, sc, sh, *, tm):
    m, d = x.shape
    g_cnt, g_rows, _ = sc.shape
    rows_per_gate = m // g_cnt
    g_blk = 1 if g_rows == 1 else tm
    mod_spec = pl.BlockSpec((None, g_blk, d), lambda i: ((i * tm) // rows_per_gate, 0, 0))
    return pl.pallas_call(
        _norm_mod_kernel,
        grid=(m // tm,),
        in_specs=[pl.BlockSpec((tm, d), lambda i: (i, 0)),
                  pl.BlockSpec((1, d), lambda i: (0, 0)), mod_spec, mod_spec],
        out_specs=pl.BlockSpec((tm, d), lambda i: (i, 0)),
        out_shape=jax.ShapeDtypeStruct((m, d), BF16),
        compiler_params=_cp(("arbitrary",)),
        name="norm_mod",
    )(x, g, sc, sh)


def _rmsnorm_kernel(x_ref, g_ref, o_ref):
    x = x_ref[...]
    o_ref[...] = x * lax.rsqrt(jnp.mean(x * x, axis=-1, keepdims=True) + EPS) * g_ref[...]


def rmsnorm_rows(x, g, *, tm):
    m, d = x.shape
    return pl.pallas_call(
        _rmsnorm_kernel,
        grid=(m // tm,),
        in_specs=[pl.BlockSpec((tm, d), lambda i: (i, 0)), pl.BlockSpec((1, d), lambda i: (0, 0))],
        out_specs=pl.BlockSpec((tm, d), lambda i: (i, 0)),
        out_shape=jax.ShapeDtypeStruct((m, d), F32),
        compiler_params=_cp(("arbitrary",)),
        name="rmsnorm_rows",
    )(x, g)


def _compress_kernel(s_ref, w1_ref, pe_ref, w2_ref, o_ref):
    n_seg = s_ref.shape[0]
    half = s_ref.shape[1]
    s = s_ref[...].astype(BF16)
    w_lo = w1_ref[0].astype(BF16)
    w_hi = w1_ref[1].astype(BF16)
    lo = _dot(s, w_lo)
    hi = _dot(s, w_hi)
    pe = pe_ref[...].astype(BF16)
    bias = _dot(pe[:, :half], w_lo) + _dot(pe[:, half:], w_hi)
    pre = lo + pltpu.roll(hi, n_seg - 1, 0) + bias[0:1, :]
    out = _dot(_silu(pre).astype(BF16), w2_ref[...].astype(BF16))
    row = lax.broadcasted_iota(jnp.int32, out.shape, 0)
    o_ref[...] = jnp.where(row < n_seg - 1, out, 0.0)


def compress_segments(seg, w1, pe, w2):
    b, _, n_seg, half = seg.shape
    w1r = w1.reshape(2, 2, half, D_HEAD)
    pe8 = jnp.broadcast_to(pe.reshape(2, 1, 2 * half), (2, 8, 2 * half))
    return pl.pallas_call(
        _compress_kernel,
        grid=(b, 4),
        in_specs=[pl.BlockSpec((None, None, n_seg, half), lambda i, j: (i, j, 0, 0)),
                  pl.BlockSpec((None, 2, half, D_HEAD), lambda i, j: (j // 2, 0, 0, 0)),
                  pl.BlockSpec((None, 8, 2 * half), lambda i, j: (j // 2, 0, 0)),
                  pl.BlockSpec((None, D_HEAD, D_HEAD), lambda i, j: (j // 2, 0, 0))],
        out_specs=pl.BlockSpec((None, None, n_seg, D_HEAD), lambda i, j: (i, j, 0, 0)),
        out_shape=jax.ShapeDtypeStruct((b, 4, n_seg, D_HEAD), F32),
        compiler_params=_cp(("arbitrary", "arbitrary"), VMEM_BIG),
        name="compress_segments",
    )(seg, w1r, pe8, w2)


def _slope_column(g, rows_per_head):
    n = HPG * rows_per_head
    h = lax.broadcasted_iota(jnp.int32, (n, 1), 0) // rows_per_head + g * HPG
    out = jnp.zeros((n, 1), F32)
    for hh in range(H_NSA):
        out = jnp.where(h == hh, 2.0 ** (-8.0 * (hh + 1) / H_NSA), out)
    return out


def _masked_softmax_rows(s, valid):
    s = jnp.where(valid, s, NEG)
    m = jnp.max(s, axis=-1, keepdims=True)
    e = jnp.where(valid, jnp.exp(s - m), 0.0)
    d = jnp.sum(e, axis=-1, keepdims=True)
    return e / jnp.where(d > 0.0, d, 1.0)


def _select_blocks(score, n_blocks):
    lane = lax.broadcasted_iota(jnp.int32, score.shape, 1)
    rank = jnp.zeros(score.shape, jnp.int32)
    for i in range(n_blocks):
        c = score[:, i:i + 1]
        ahead = (c > score) | ((c == score) & (lane > i))
        rank = rank + ahead.astype(jnp.int32)
    return (rank < N_SELECT) & (score > -jnp.inf)


def _slc_matrix(n_cmp_rows, nc, n_lanes):
    i = np.arange(n_cmp_rows)[:, None]
    j = np.arange(n_lanes)[None, :]
    m = (i >= 4 * j - 1) & (i <= 4 * j + 3) & (i < nc)
    return jnp.asarray(m.astype(np.float32), dtype=BF16)


def _head_rms(o):
    return o * lax.rsqrt(jnp.mean(o * o, axis=-1, keepdims=True) + EPS)


def _nsa_prompt_kernel(q_ref, sm_ref, selk_ref, selv_ref, wink_ref, winv_ref, cmpk_ref, cmpv_ref,
                       kaug_ref, caug_ref, mslc_ref, wbias_ref, gm_ref, o_ref, *, qb, tk):
    g = pl.program_id(1)
    qi = pl.program_id(2)
    q0 = qi * qb
    rows = HPG * qb
    n_seg = cmpk_ref.shape[0]
    nc = n_seg - 1
    t_len = selk_ref.shape[0]
    ns = t_len // SEL_BLOCK

    q = q_ref[...] * SCALE
    qs = jnp.concatenate([q[:, h * D_HEAD:(h + 1) * D_HEAD] for h in range(HPG)], axis=0)
    qs_b = qs.astype(BF16)
    slope = _slope_column(g, qb)
    rpos = q0 + lax.broadcasted_iota(jnp.int32, (rows, 1), 0) % qb
    lane = lax.broadcasted_iota(jnp.int32, (rows, LANE), 1)
    ones_col = jnp.where(lax.broadcasted_iota(jnp.int32, (tk, LANE), 1) == 0, 1.0, 0.0).astype(BF16)

    qa_c = jnp.concatenate([qs_b, jnp.where(lane == 0, slope * CMP_STRIDE, 0.0).astype(BF16)], axis=1)
    ka_c = jnp.concatenate([cmpk_ref[...].astype(BF16), caug_ref[...]], axis=1)
    ci = lax.broadcasted_iota(jnp.int32, (1, n_seg), 1)
    c_valid = (ci * CMP_STRIDE + (CMP_BLOCK - 1) <= rpos) & (ci < nc)
    p_cmp = _masked_softmax_rows(_dot_nt(qa_c, ka_c), c_valid)
    o_cmp = _dot(p_cmp.astype(BF16), cmpv_ref[...].astype(BF16))

    imp = p_cmp[0:qb]
    for h in range(1, HPG):
        imp = imp + p_cmp[h * qb:(h + 1) * qb]
    p_slc = None
    for part in _split3(imp):
        d = _dot_nt(mslc_ref[...], part)
        p_slc = d if p_slc is None else p_slc + d
    p_slc = p_slc[0:SEL_BLOCK]
    blk = lax.broadcasted_iota(jnp.int32, (SEL_BLOCK, qb), 0)
    cur = (q0 + lax.broadcasted_iota(jnp.int32, (1, qb), 1)) // SEL_BLOCK
    forced = (blk == 0) | (blk > cur - N_LOCAL)
    score = jnp.where(forced, FORCE_SCORE, p_slc)
    score = jnp.where((blk <= cur) & (blk < ns), score, -jnp.inf)
    sub = 8
    pieces = [score[v * sub:(v + 1) * sub] for v in range(SEL_BLOCK // sub)]
    ranks = [jnp.zeros((sub, qb), jnp.int32) for _ in pieces]
    sub_row = lax.broadcasted_iota(jnp.int32, (sub, qb), 0)
    for i in range(ns):
        c = pieces[i // sub][i % sub:i % sub + 1, :]
        for v, piece in enumerate(pieces):
            if v < i // sub:
                ahead = (c > piece).astype(jnp.int32)
            elif v > i // sub:
                ahead = (c >= piece).astype(jnp.int32)
            else:
                ahead = jnp.where(sub_row > i % sub, (c >= piece).astype(jnp.int32), (c > piece).astype(jnp.int32))
            ranks[v] = ranks[v] + ahead
    rank = jnp.concatenate(ranks, axis=0)
    sel = (rank < N_SELECT) & (score > -jnp.inf)
    pen_t = jnp.where(sel, 0.0, PEN)
    pen = jnp.transpose(jnp.concatenate([pen_t, jnp.zeros((LANE - SEL_BLOCK, qb), F32)], axis=0))
    pen4 = jnp.concatenate([pen] * HPG, axis=0)
    q_aug = jnp.where(lane < SEL_BLOCK, pen4,
                      jnp.where(lane == SEL_BLOCK, slope * SEL_BLOCK,
                                jnp.where(lane == SEL_BLOCK + 1, slope, 0.0)))
    qa = jnp.concatenate([qs_b, q_aug.astype(BF16)], axis=1)

    def sel_tile(k0, carry, diagonal):
        m_run, acc = carry
        ka = jnp.concatenate([selk_ref[pl.ds(k0, tk), :].astype(BF16), kaug_ref[pl.ds(k0, tk), :]], axis=1)
        va = jnp.concatenate([selv_ref[pl.ds(k0, tk), :].astype(BF16), ones_col], axis=1)
        st = _dot_nt(qa, ka)
        if diagonal:
            st = jnp.where(k0 + lax.broadcasted_iota(jnp.int32, (1, tk), 1) <= rpos, st, NEG)
        m_new = jnp.maximum(m_run, jnp.max(st, axis=-1, keepdims=True))
        p = jnp.exp(st - m_new).astype(BF16)
        return m_new, jnp.exp(m_run - m_new) * acc + _dot(p, va)

    n_full = q0 // tk
    init = (jnp.full((rows, 1), NEG, F32), jnp.zeros((rows, 2 * LANE), F32))
    carry = lax.fori_loop(0, n_full, lambda kt, c: sel_tile(pl.multiple_of(kt * tk, tk), c, False), init)
    _, acc_sel = sel_tile(pl.multiple_of(n_full * tk, tk), carry, True)
    o_sel = acc_sel[:, 0:D_HEAD] / acc_sel[:, D_HEAD:D_HEAD + 1]

    wk = WINDOW + qb
    ks = pl.multiple_of(jnp.maximum(q0 - WINDOW, 0), qb)
    s = _dot_nt(qs_b, wink_ref[pl.ds(ks, wk), :].astype(BF16)) + wbias_ref[...]
    m = jnp.max(s, axis=-1, keepdims=True)
    p = jnp.exp(s - m).astype(BF16)
    ones_w = jnp.where(lax.broadcasted_iota(jnp.int32, (wk, LANE), 1) == 0, 1.0, 0.0).astype(BF16)
    acc_win = _dot(p, jnp.concatenate([winv_ref[pl.ds(ks, wk), :].astype(BF16), ones_w], axis=1))
    o_win = acc_win[:, 0:D_HEAD] / acc_win[:, D_HEAD:D_HEAD + 1]

    gates = 1.0 / (1.0 + jnp.exp(-sm_ref[...]))

    def gate_col(branch):
        cols = []
        for h in range(HPG):
            c0 = SM_GATE + h * 3 + branch
            c1 = SM_GATE + (HPG + h) * 3 + branch
            cols.append(jnp.where(g == 0, gates[:, c0:c0 + 1], gates[:, c1:c1 + 1]))
        return jnp.concatenate(cols, axis=0)

    o = gate_col(0) * o_cmp + gate_col(1) * o_sel + gate_col(2) * o_win
    o = _head_rms(o)
    for h in range(HPG):
        o_ref[:, h * D_HEAD:(h + 1) * D_HEAD] = (
            o[h * qb:(h + 1) * qb] * gm_ref[:, h * D_HEAD:(h + 1) * D_HEAD]).astype(o_ref.dtype)


def _key_aug(t_len):
    pos = np.arange(t_len)[:, None]
    lane = np.arange(LANE)[None, :]
    a = np.where(lane < SEL_BLOCK, (pos // SEL_BLOCK == lane).astype(np.float32),
                 np.where(lane == SEL_BLOCK, (pos // SEL_BLOCK).astype(np.float32),
                          np.where(lane == SEL_BLOCK + 1, (pos % SEL_BLOCK).astype(np.float32), 0.0)))
    return jnp.asarray(a, dtype=BF16)


def _cmp_aug(n_seg):
    a = np.zeros((n_seg, LANE), np.float32)
    a[:, 0] = np.arange(n_seg)
    return jnp.asarray(a, dtype=BF16)


def _window_bias(qb):
    n_pat = WINDOW // qb + 1
    r = np.arange(HPG * qb)
    head = r // qb
    dist = (np.arange(n_pat)[:, None, None] * qb + (r % qb)[None, :, None]
            - np.arange(WINDOW + qb)[None, None, :])
    valid = (dist >= 0) & (dist <= WINDOW)
    out = np.empty((G_NSA,) + dist.shape, np.float32)
    for g in range(G_NSA):
        slope = 2.0 ** (-8.0 * (g * HPG + head + 1) / H_NSA)
        out[g] = np.where(valid, -slope[None, :, None] * dist, NEG)
    return jnp.asarray(out)


def nsa_prompt(z, cmp_kv, g_mix, *, qb=256, tk=512):
    b, t_len, _ = z.shape
    assert t_len // SEL_BLOCK <= SEL_BLOCK and t_len % tk == 0 and tk % qb == 0 and WINDOW % qb == 0
    n_seg = cmp_kv.shape[2]
    assert n_seg <= 256
    gw = HPG * D_HEAD
    kv_blk = C_KV // D_HEAD
    n_pat = WINDOW // qb
    wbias = _window_bias(qb)

    def kv_spec(slot):
        return pl.BlockSpec((None, t_len, D_HEAD), lambda i, g, q: (i, 0, kv_blk + 2 * slot + g))

    return pl.pallas_call(
        functools.partial(_nsa_prompt_kernel, qb=qb, tk=tk),
        grid=(b, G_NSA, t_len // qb),
        in_specs=[pl.BlockSpec((None, qb, gw), lambda i, g, q: (i, q, g)),
                  pl.BlockSpec((None, qb, LANE), lambda i, g, q: (i, q, C_SM // LANE)),
                  kv_spec(2), kv_spec(3), kv_spec(4), kv_spec(5),
                  pl.BlockSpec((None, None, n_seg, D_HEAD), lambda i, g, q: (i, g, 0, 0)),
                  pl.BlockSpec((None, None, n_seg, D_HEAD), lambda i, g, q: (i, 2 + g, 0, 0)),
                  pl.BlockSpec((t_len, LANE), lambda i, g, q: (0, 0)),
                  pl.BlockSpec((n_seg, LANE), lambda i, g, q: (0, 0)),
                  pl.BlockSpec((LANE, n_seg), lambda i, g, q: (0, 0)),
                  pl.BlockSpec((None, None, HPG * qb, WINDOW + qb),
                               lambda i, g, q: (g, jnp.minimum(q, n_pat), 0, 0)),
                  pl.BlockSpec((1, gw), lambda i, g, q: (0, g))],
        out_specs=pl.BlockSpec((None, qb, gw), lambda i, g, q: (i, q, g)),
        out_shape=jax.ShapeDtypeStruct((b, t_len, H_NSA * D_HEAD), BF16),
        compiler_params=_cp(("arbitrary", "arbitrary", "arbitrary"), VMEM_BIG),
        name="nsa_prompt",
    )(z, z, z, z, z, z, cmp_kv, cmp_kv, _key_aug(t_len), _cmp_aug(n_seg),
      jnp.transpose(_slc_matrix(n_seg, n_seg - 1, LANE)), wbias, g_mix)


def _sb_prompt_kernel(q_ref, k_ref, v_ref, gm_ref, o_ref, *, qb, tk):
    qi = pl.program_id(2)
    q0 = qi * qb
    q = (q_ref[...] * SCALE).astype(BF16)
    rpos = q0 + lax.broadcasted_iota(jnp.int32, (qb, 1), 0)
    tri = (lax.broadcasted_iota(jnp.int32, (tk, tk), 0) > lax.broadcasted_iota(jnp.int32, (tk, tk), 1))
    tri = jnp.where(tri, 1.0, 0.0).astype(BF16)
    n_diag = qb // tk

    def tile(k0, carry, masked):
        run, acc = carry
        z = _dot_nt(q, k_ref[pl.ds(k0, tk), :].astype(BF16))
        lb = _log_sigmoid(z)
        lr = lb - z
        if masked:
            mask = (k0 + lax.broadcasted_iota(jnp.int32, (1, tk), 1)) < rpos
            lr = jnp.where(mask, lr, 0.0)
        hi, lo = _split2(lr)
        after = _dot(hi, tri) + _dot(lo, tri) + run
        a = jnp.exp(lb + after)
        if masked:
            a = jnp.where(mask, a, 0.0)
        acc = acc + _dot(a.astype(BF16), v_ref[pl.ds(k0, tk), :].astype(BF16))
        return after[:, 0:1] + lr[:, 0:1], acc

    carry = (jnp.zeros((qb, 1), F32), jnp.zeros((qb, D_HEAD), F32))
    for it in range(n_diag):
        carry = tile(pl.multiple_of(q0 + (n_diag - 1 - it) * tk, tk), carry, True)

    def full_tiles(it, carry):
        for u in range(n_diag):
            carry = tile(pl.multiple_of(q0 - (it * n_diag + u + 1) * tk, tk), carry, False)
        return carry

    _, acc = lax.fori_loop(0, qi, full_tiles, carry)
    o_ref[...] = (_head_rms(acc) * gm_ref[...]).astype(o_ref.dtype)


def sb_prompt(z, g_mix, *, qb=512, tk=256):
    b, t_len, _ = z.shape
    assert qb % tk == 0 and t_len % qb == 0
    blk = C_SB // D_HEAD
    gm_blk = (H_NSA * D_HEAD) // D_HEAD
    return pl.pallas_call(
        functools.partial(_sb_prompt_kernel, qb=qb, tk=tk),
        grid=(b, H_SB, t_len // qb),
        in_specs=[pl.BlockSpec((None, qb, D_HEAD), lambda i, h, q: (i, q, blk + h)),
                  pl.BlockSpec((None, t_len, D_HEAD), lambda i, h, q: (i, 0, blk + H_SB + h)),
                  pl.BlockSpec((None, t_len, D_HEAD), lambda i, h, q: (i, 0, blk + 2 * H_SB + h)),
                  pl.BlockSpec((1, D_HEAD), lambda i, h, q: (0, gm_blk + h))],
        out_specs=pl.BlockSpec((None, qb, D_HEAD), lambda i, h, q: (i, q, h)),
        out_shape=jax.ShapeDtypeStruct((b, t_len, H_SB * D_HEAD), BF16),
        compiler_params=_cp(("arbitrary", "arbitrary", "arbitrary")),
        name="sb_prompt",
    )(z, z, z, g_mix)


N_LEVELS = 6


def _gla_level_tables():
    c = GLA_CHUNK
    idx = np.arange(c)
    sums = np.zeros(((N_LEVELS + 1) * c, c), np.float32)
    pair = np.zeros((N_LEVELS + 1, c, c), np.float32)
    sums[:c] = (idx[None, :] <= idx[:, None])
    pair[0] = np.eye(c)
    for lv in range(N_LEVELS):
        mid = ((idx >> (lv + 1)) << (lv + 1)) + (1 << lv)
        upper = idx >= mid
        j = idx[None, :]
        in_up = upper[:, None] & (j >= mid[:, None]) & (j <= idx[:, None])
        in_lo = (~upper)[:, None] & (j > idx[:, None]) & (j <= mid[:, None] - 1)
        sums[(lv + 1) * c:(lv + 2) * c] = in_up | in_lo
        same = (idx[:, None] >> (lv + 1)) == (idx[None, :] >> (lv + 1))
        pair[lv + 1] = same & upper[:, None] & (~upper)[None, :]
    return jnp.asarray(sums, dtype=BF16), jnp.asarray(pair, dtype=F32)


def _gla_prompt_kernel(qg_ref, kg_ref, vg_ref, og_ref, sm_ref, wa2_ref, ba_ref, sums_ref, pair_ref,
                       gm_ref, o_ref, st_ref, state_ref, *, tb):
    nt = pl.program_id(1)
    c = GLA_CHUNK

    @pl.when(nt == 0)
    def _():
        state_ref[...] = jnp.zeros_like(state_ref)

    wa2 = wa2_ref[...].astype(BF16)
    sums = sums_ref[...]

    def chunk(ci, _):
        r0 = pl.multiple_of(ci * c, c)
        rows = pl.ds(r0, c)
        x = _dot(sm_ref[rows, :].astype(BF16), wa2) + ba_ref[...]
        loga = _log_sigmoid(x) * (1.0 / GLA_TAU)
        hi, lo = _split2(loga)
        dsum = _dot(sums, hi) + _dot(sums, lo)
        cb = dsum[0:c]
        q = qg_ref[rows, :] * (GLA_DK ** -0.5)
        k = kg_ref[rows, :]
        v = vg_ref[rows, :].astype(BF16)
        qf = [q.astype(BF16)]
        kf = [k.astype(BF16)]
        for lv in range(N_LEVELS):
            e = jnp.exp(dsum[(lv + 1) * c:(lv + 2) * c])
            qf.append((q * e).astype(BF16))
            kf.append((k * e).astype(BF16))
        c_last = cb[c - 1:c, :]
        q_in = (q * jnp.exp(cb)).astype(BF16)
        k_out = (k * jnp.exp(c_last - cb)).astype(BF16)
        decay = jnp.exp(c_last)
        for h in range(H_GLA):
            ks = slice(h * GLA_DK, (h + 1) * GLA_DK)
            vs = slice(h * GLA_DV, (h + 1) * GLA_DV)
            att = None
            for lv in range(N_LEVELS + 1):
                term = pair_ref[lv] * _dot_nt(qf[lv][:, ks], kf[lv][:, ks])
                att = term if att is None else att + term
            s_t = state_ref[h]
            o_h = _dot(att.astype(BF16), v[:, vs]) + _dot_nt(q_in[:, ks], s_t.astype(BF16))
            state_ref[h] = s_t * decay[:, ks] + _dot_tn(v[:, vs], k_out[:, ks])
            og = og_ref[rows, vs]
            o_ref[rows, vs] = (_head_rms(o_h) * _silu(og) * gm_ref[:, vs]).astype(o_ref.dtype)
        return 0

    lax.fori_loop(0, tb // c, chunk, 0)

    @pl.when(nt == pl.num_programs(1) - 1)
    def _():
        st_ref[...] = state_ref[...]


def _wa2_padded(gla_wa2):
    w = jnp.zeros((LANE, H_GLA * GLA_DK), F32)
    return w.at[SM_ALR:SM_ALR + GLA_RANK].set(gla_wa2)


def gla_prompt(z, gla_wa2, gla_ba, g_mix, *, tb=512):
    b, t_len, _ = z.shape
    kw = H_GLA * GLA_DK
    vw = H_GLA * GLA_DV
    sums, pair = _gla_level_tables()
    n_rows = sums.shape[0]
    return pl.pallas_call(
        functools.partial(_gla_prompt_kernel, tb=tb),
        grid=(b, t_len // tb),
        in_specs=[pl.BlockSpec((None, tb, kw), lambda i, n: (i, n, C_QG // kw)),
                  pl.BlockSpec((None, tb, kw), lambda i, n: (i, n, C_KG // kw)),
                  pl.BlockSpec((None, tb, vw), lambda i, n: (i, n, C_VG // vw)),
                  pl.BlockSpec((None, tb, vw), lambda i, n: (i, n, C_OG // vw)),
                  pl.BlockSpec((None, tb, LANE), lambda i, n: (i, n, C_SM // LANE)),
                  pl.BlockSpec((LANE, kw), lambda i, n: (0, 0)),
                  pl.BlockSpec((1, kw), lambda i, n: (0, 0)),
                  pl.BlockSpec((n_rows, GLA_CHUNK), lambda i, n: (0, 0)),
                  pl.BlockSpec((N_LEVELS + 1, GLA_CHUNK, GLA_CHUNK), lambda i, n: (0, 0, 0)),
                  pl.BlockSpec((1, vw), lambda i, n: (0, (H_NSA + H_SB) * D_HEAD // vw))],
        out_specs=[pl.BlockSpec((None, tb, vw), lambda i, n: (i, n, 0)),
                   pl.BlockSpec((None, H_GLA, GLA_DV, GLA_DK), lambda i, n: (i, 0, 0, 0))],
        out_shape=[jax.ShapeDtypeStruct((b, t_len, vw), BF16),
                   jax.ShapeDtypeStruct((b, H_GLA, GLA_DV, GLA_DK), F32)],
        scratch_shapes=[pltpu.VMEM((H_GLA, GLA_DV, GLA_DK), F32)],
        compiler_params=_cp(("arbitrary", "arbitrary")),
        name="gla_prompt",
    )(z, z, z, z, z, _wa2_padded(gla_wa2), gla_ba.reshape(1, kw), sums, pair, g_mix)


PAGES_PER_STEP = 8


def _gather_segments_kernel(pt_ref, *refs):
    page_refs = refs[:PAGES_PER_STEP]
    o_ref = refs[PAGES_PER_STEP]
    slots = 4 * G_NSA
    seg_per_page = page_refs[0].shape[0] // (slots * CMP_STRIDE)
    for pair in range(PAGES_PER_STEP // 2):
        r0 = pair * 2 * seg_per_page
        for cg in range(4):
            for p in range(CMP_STRIDE):
                rows = [page_refs[2 * pair + u][pl.ds(p * slots + cg, seg_per_page, stride=CMP_STRIDE * slots), :]
                        for u in range(2)]
                o_ref[cg, r0:r0 + 2 * seg_per_page, p * D_HEAD:(p + 1) * D_HEAD] = (
                    jnp.concatenate(rows, axis=0).astype(o_ref.dtype))


def _rows_view(cache):
    depth, n_pool, page, a, b, d = cache.shape
    return cache.reshape(depth, n_pool, page * a * b, d)


def gather_segments(cache_nsa, page_table, layer):
    page = cache_nsa.shape[2]
    b, n_pages = page_table.shape
    cache = _rows_view(cache_nsa)
    seg_per_page = page // CMP_STRIDE
    n_seg = n_pages * seg_per_page
    steps = n_pages // PAGES_PER_STEP

    def page_spec(u):
        return pl.BlockSpec((None, None, cache.shape[2], D_HEAD),
                            lambda i, s, pt: (layer, pt[i * n_pages + s * PAGES_PER_STEP + u], 0, 0))

    return pl.pallas_call(
        _gather_segments_kernel,
        grid_spec=pltpu.PrefetchScalarGridSpec(
            num_scalar_prefetch=1,
            grid=(b, steps),
            in_specs=[page_spec(u) for u in range(PAGES_PER_STEP)],
            out_specs=pl.BlockSpec((None, 4, PAGES_PER_STEP * seg_per_page, CMP_STRIDE * D_HEAD),
                                   lambda i, s, pt: (i, 0, s, 0)),
        ),
        out_shape=jax.ShapeDtypeStruct((b, 4, n_seg, CMP_STRIDE * D_HEAD), BF16),
        compiler_params=_cp(("arbitrary", "arbitrary")),
        name="gather_segments",
    )(page_table.reshape(-1), *([cache] * PAGES_PER_STEP))


def _nsa_sample_select_kernel(q_ref, cmp_ref, mslc_ref, ocmp_ref, idx_ref, *, pos, n_blk_lanes):
    n_seg = cmp_ref.shape[1]
    nc = n_seg - 1
    ns = pos // SEL_BLOCK + 1
    cur = pos // SEL_BLOCK
    q = (q_ref[...] * SCALE).astype(BF16)
    row = lax.broadcasted_iota(jnp.int32, (H_NSA, 1), 0)
    slope = _slope_column(0, 1)
    slope = jnp.concatenate([slope, _slope_column(1, 1)], axis=0)
    s = jnp.where(row < HPG, _dot_nt(q, cmp_ref[0].astype(BF16)), _dot_nt(q, cmp_ref[1].astype(BF16)))
    ci = lax.broadcasted_iota(jnp.int32, (1, n_seg), 1)
    c_dist = pos - (ci * CMP_STRIDE + (CMP_BLOCK - 1))
    p = _masked_softmax_rows(s - slope * c_dist.astype(F32), (c_dist >= 0) & (ci < nc))
    pb = p.astype(BF16)
    ocmp_ref[...] = jnp.where(row < HPG, _dot(pb, cmp_ref[2].astype(BF16)), _dot(pb, cmp_ref[3].astype(BF16)))

    imp = jnp.concatenate([jnp.sum(p[g * HPG:(g + 1) * HPG], axis=0, keepdims=True) for g in range(G_NSA)]
                          + [jnp.zeros((H_NSA - G_NSA, n_seg), F32)], axis=0)
    p_slc = None
    for part in _split3(imp):
        d = _dot(part, mslc_ref[...])
        p_slc = d if p_slc is None else p_slc + d
    blk = lax.broadcasted_iota(jnp.int32, p_slc.shape, 1)
    forced = (blk == 0) | (blk > cur - N_LOCAL)
    score = jnp.where(forced, FORCE_SCORE, p_slc)
    score = jnp.where(blk <= cur, score, -jnp.inf)
    sel = _select_blocks(score, ns)
    upper = (lax.broadcasted_iota(jnp.int32, (n_blk_lanes, n_blk_lanes), 0)
             < lax.broadcasted_iota(jnp.int32, (n_blk_lanes, n_blk_lanes), 1))
    before = _dot(jnp.where(sel, 1.0, 0.0).astype(BF16), jnp.where(upper, 1.0, 0.0).astype(BF16))
    blk_f = blk.astype(F32)
    out_lane = lax.broadcasted_iota(jnp.int32, (H_NSA, LANE), 1)
    out = jnp.zeros((H_NSA, LANE), F32)
    for n in range(N_SELECT):
        v = jnp.sum(jnp.where(sel & (before == float(n)), blk_f, 0.0), axis=-1, keepdims=True)
        out = jnp.where(out_lane == n, v, out)
    idx_ref[...] = out.astype(jnp.int32)


def nsa_sample_select(q8, cmp_kv, *, pos):
    b = q8.shape[0]
    n_seg = cmp_kv.shape[2]
    ns = pos // SEL_BLOCK + 1
    assert ns >= N_SELECT
    n_blk_lanes = -(-ns // LANE) * LANE
    return pl.pallas_call(
        functools.partial(_nsa_sample_select_kernel, pos=pos, n_blk_lanes=n_blk_lanes),
        grid=(b,),
        in_specs=[pl.BlockSpec((None, H_NSA, D_HEAD), lambda i: (i, 0, 0)),
                  pl.BlockSpec((None, 4, n_seg, D_HEAD), lambda i: (i, 0, 0, 0)),
                  pl.BlockSpec((n_seg, n_blk_lanes), lambda i: (0, 0))],
        out_specs=[pl.BlockSpec((None, H_NSA, D_HEAD), lambda i: (i, 0, 0)),
                   pl.BlockSpec((None, H_NSA, LANE), lambda i: (i, 0, 0))],
        out_shape=[jax.ShapeDtypeStruct((b, H_NSA, D_HEAD), F32),
                   jax.ShapeDtypeStruct((b, H_NSA, LANE), jnp.int32)],
        compiler_params=_cp(("arbitrary",)),
        name="nsa_sample_select",
    )(q8, cmp_kv, _slc_matrix(n_seg, n_seg - 1, n_blk_lanes))


def _softmax_with_new(s, valid, s_new, v_mat, v_new):
    s = jnp.where(valid, s, NEG)
    m = jnp.maximum(jnp.max(s, axis=-1, keepdims=True), s_new)
    e = jnp.where(valid, jnp.exp(s - m), 0.0)
    e_new = jnp.exp(s_new - m)
    d = jnp.sum(e, axis=-1, keepdims=True) + e_new
    return (_dot(e.astype(BF16), v_mat) + e_new.astype(BF16).astype(F32) * v_new) / d


def _nsa_sample_attend_kernel(idx_ref, pt_ref, *refs, pos):
    sel_refs = refs[:N_SELECT]
    (q_ref, nk_ref, nv_ref, nwk_ref, nwv_ref, sm_ref, win_ref, ocmp_ref, gm_ref, o_ref) = refs[N_SELECT:]
    b = pl.program_id(0)
    g = pl.program_id(1)
    n_past_blocks = pos // SEL_BLOCK
    q = (q_ref[...] * SCALE).astype(BF16)
    qf = q.astype(F32)
    slope = jnp.concatenate([_slope_column(0, 1), _slope_column(1, 1)], axis=0)

    def new_row(ref):
        return ref[pl.ds(b, 1), :].astype(BF16)

    def slot_rows(ref, slot, n_tok, n_slots):
        both = [ref[pl.ds(slot * G_NSA + gg, n_tok, stride=n_slots), :] for gg in range(G_NSA)]
        return jnp.where(g == 0, both[0], both[1]).astype(BF16)

    k_all = jnp.concatenate([slot_rows(r, 2, SEL_BLOCK, 4 * G_NSA) for r in sel_refs], axis=0)
    v_all = jnp.concatenate([slot_rows(r, 3, SEL_BLOCK, 4 * G_NSA) for r in sel_refs], axis=0)
    width = N_SELECT * SEL_BLOCK
    lane = lax.broadcasted_iota(jnp.int32, (1, width), 1)
    tok = jnp.zeros((1, width), jnp.int32)
    in_cache = jnp.zeros((1, width), jnp.bool_)
    for n in range(N_SELECT):
        blk_id = idx_ref[(b * G_NSA + g) * N_SELECT + n]
        here = (lane // SEL_BLOCK) == n
        tok = jnp.where(here, blk_id * SEL_BLOCK + lane % SEL_BLOCK, tok)
        in_cache = in_cache | (here & (blk_id < n_past_blocks))
    dist = pos - tok
    s = _dot_nt(q, k_all) - slope * dist.astype(F32)
    k_new = new_row(nk_ref)
    s_new = jnp.sum(qf * k_new.astype(F32), axis=-1, keepdims=True)
    o_sel = _softmax_with_new(s, in_cache & (dist >= 0), s_new, v_all, new_row(nv_ref).astype(F32))

    n_win = win_ref.shape[0] // (2 * G_NSA)
    wi = lax.broadcasted_iota(jnp.int32, (1, n_win), 1)
    w_dist = n_win - wi
    s = _dot_nt(q, slot_rows(win_ref, 0, n_win, 2 * G_NSA)) - slope * w_dist.astype(F32)
    s_new = jnp.sum(qf * new_row(nwk_ref).astype(F32), axis=-1, keepdims=True)
    o_win = _softmax_with_new(s, w_dist <= WINDOW, s_new, slot_rows(win_ref, 1, n_win, 2 * G_NSA),
                              new_row(nwv_ref).astype(F32))

    gates = 1.0 / (1.0 + jnp.exp(-sm_ref[pl.ds(b, 1), :]))
    hrow = lax.broadcasted_iota(jnp.int32, (H_NSA, LANE), 0)
    glane = lax.broadcasted_iota(jnp.int32, (H_NSA, LANE), 1)

    def gate_col(branch):
        return jnp.sum(jnp.where(glane == SM_GATE + 3 * hrow + branch, gates, 0.0), axis=-1, keepdims=True)

    o = gate_col(0) * ocmp_ref[...] + gate_col(1) * o_sel + gate_col(2) * o_win
    o = _head_rms(o) * gm_ref[...]
    o_ref[...] = jnp.where(g == 0, o[0:HPG], o[HPG:H_NSA])


def nsa_sample_attend(zs, q8, o_cmp, idx, cache_nsa, state_win, page_table, gm8, layer, *, pos):
    page = cache_nsa.shape[2]
    b, n_pages = page_table.shape
    halves = page // SEL_BLOCK
    cache = _rows_view(cache_nsa)
    blk_rows = cache.shape[2] // halves
    win = _rows_view(state_win)
    last_blk = pos // SEL_BLOCK - 1
    kv_blk = C_KV // D_HEAD

    def sel_spec(n):
        def index_map(i, g, idx_ref, pt_ref):
            blk_id = jnp.minimum(idx_ref[(i * G_NSA + g) * N_SELECT + n], last_blk)
            return (layer, pt_ref[i * n_pages + blk_id // halves], blk_id % halves, 0)
        return pl.BlockSpec((None, None, blk_rows, D_HEAD), index_map)

    def zs_spec(slot):
        return pl.BlockSpec((MS, D_HEAD), lambda i, g, a, c: (0, kv_blk + 2 * slot + g))

    in_specs = ([sel_spec(n) for n in range(N_SELECT)]
                + [pl.BlockSpec((None, H_NSA, D_HEAD), lambda i, g, a, c: (i, 0, 0)),
                   zs_spec(2), zs_spec(3), zs_spec(4), zs_spec(5),
                   pl.BlockSpec((MS, LANE), lambda i, g, a, c: (0, C_SM // LANE)),
                   pl.BlockSpec((None, None, win.shape[2], D_HEAD), lambda i, g, a, c: (layer, i, 0, 0)),
                   pl.BlockSpec((None, H_NSA, D_HEAD), lambda i, g, a, c: (i, 0, 0)),
                   pl.BlockSpec((H_NSA, D_HEAD), lambda i, g, a, c: (0, 0))])
    return pl.pallas_call(
        functools.partial(_nsa_sample_attend_kernel, pos=pos),
        grid_spec=pltpu.PrefetchScalarGridSpec(
            num_scalar_prefetch=2,
            grid=(b, G_NSA),
            in_specs=in_specs,
            out_specs=pl.BlockSpec((None, None, HPG, D_HEAD), lambda i, g, a, c: (i, g, 0, 0)),
        ),
        out_shape=jax.ShapeDtypeStruct((b, G_NSA, HPG, D_HEAD), F32),
        compiler_params=_cp(("arbitrary", "arbitrary")),
        name="nsa_sample_attend",
    )(idx, page_table.reshape(-1), *([cache] * N_SELECT), q8, zs, zs, zs, zs, zs, win, o_cmp, gm8)


SB_PAGES_PER_STEP = 16


def _sb_sample_kernel(pt_ref, *refs):
    page_refs = refs[:SB_PAGES_PER_STEP]
    q_ref, gm_ref, o_ref, run_ref, acc_ref = refs[SB_PAGES_PER_STEP:]
    b = pl.program_id(0)
    s_idx = pl.program_id(1)
    slots = 2 * H_SB
    page = page_refs[0].shape[0] // slots
    kw = H_SB * D_HEAD
    n_u = SB_PAGES_PER_STEP

    @pl.when(s_idx == 0)
    def _():
        run_ref[...] = jnp.zeros_like(run_ref)
        acc_ref[...] = jnp.zeros_like(acc_ref)

    def heads_on_lanes(ref, first_slot):
        return jnp.concatenate([ref[pl.ds(first_slot + h, page, stride=slots), :].astype(BF16)
                                for h in range(H_SB)], axis=1)

    qrow = q_ref[pl.ds(b, 1), :] * SCALE
    hrow = lax.broadcasted_iota(jnp.int32, (8, kw), 0)
    hlane = lax.broadcasted_iota(jnp.int32, (8, kw), 1) // D_HEAD
    qm = jnp.where(hrow == hlane, qrow, 0.0).astype(BF16)
    tri = (lax.broadcasted_iota(jnp.int32, (page, page), 0) > lax.broadcasted_iota(jnp.int32, (page, page), 1))
    tri = jnp.where(tri, 1.0, 0.0).astype(BF16)

    k_all = jnp.concatenate([heads_on_lanes(r, 0) for r in page_refs], axis=0)
    z = _dot_nt(qm, k_all)
    lb = _log_sigmoid(z)
    lr = lb - z
    lr_rows = jnp.concatenate([lr[:, u * page:(u + 1) * page] for u in range(n_u)], axis=0)
    hi, lo = _split2(lr_rows)
    local = _dot(hi, tri) + _dot(lo, tri)
    total = local[:, 0:1] + lr_rows[:, 0:1]
    run = run_ref[:, 0:1]
    offs = [None] * n_u
    for u in range(n_u - 1, -1, -1):
        offs[u] = run
        run = run + total[8 * u:8 * u + 8]
    run_ref[...] = jnp.broadcast_to(run, run_ref.shape)
    after = jnp.concatenate([local[8 * u:8 * u + 8] + offs[u] for u in range(n_u)], axis=1)
    a = jnp.exp(lb + after)
    v_all = jnp.concatenate([heads_on_lanes(r, H_SB) for r in page_refs], axis=0)
    acc = acc_ref[...] + _dot(a.astype(BF16), v_all)
    acc_ref[...] = acc

    @pl.when(s_idx == pl.num_programs(1) - 1)
    def _():
        o = jnp.concatenate([acc[h:h + 1, h * D_HEAD:(h + 1) * D_HEAD] for h in range(H_SB)], axis=0)
        o_ref[...] = _head_rms(o) * gm_ref[...]


def sb_sample(zs, cache_sb, page_table, gm4, layer):
    b, n_pages = page_table.shape
    cache = _rows_view(cache_sb)
    steps = n_pages // SB_PAGES_PER_STEP

    def page_spec(u):
        return pl.BlockSpec(
            (None, None, cache.shape[2], D_HEAD),
            lambda i, s, pt: (layer, pt[i * n_pages + (steps - 1 - s) * SB_PAGES_PER_STEP + u], 0, 0))

    return pl.pallas_call(
        _sb_sample_kernel,
        grid_spec=pltpu.PrefetchScalarGridSpec(
            num_scalar_prefetch=1,
            grid=(b, steps),
            in_specs=[page_spec(u) for u in range(SB_PAGES_PER_STEP)]
            + [pl.BlockSpec((MS, H_SB * D_HEAD), lambda i, s, pt: (0, C_SB // (H_SB * D_HEAD))),
               pl.BlockSpec((H_SB, D_HEAD), lambda i, s, pt: (0, 0))],
            out_specs=pl.BlockSpec((None, H_SB, D_HEAD), lambda i, s, pt: (i, 0, 0)),
            scratch_shapes=[pltpu.VMEM((8, LANE), F32), pltpu.VMEM((8, H_SB * D_HEAD), F32)],
        ),
        out_shape=jax.ShapeDtypeStruct((b, H_SB, D_HEAD), F32),
        compiler_params=_cp(("arbitrary", "arbitrary"), VMEM_BIG),
        name="sb_sample",
    )(page_table.reshape(-1), *([cache] * SB_PAGES_PER_STEP), zs, gm4)


def _gla_sample_kernel(qg_ref, kg_ref, vg_ref, og_ref, sm_ref, wa2_ref, ba_ref, st_ref, gm_ref,
                       o_ref, ns_ref, *, n_b):
    x = _dot(sm_ref[...].astype(BF16), wa2_ref[...].astype(BF16)) + ba_ref[...]
    decay = jnp.exp(_log_sigmoid(x) * (1.0 / GLA_TAU))
    eye = (lax.broadcasted_iota(jnp.int32, (GLA_DK, GLA_DK), 0)
           == lax.broadcasted_iota(jnp.int32, (GLA_DK, GLA_DK), 1))

    def column(row):
        return jnp.sum(jnp.where(eye, row, 0.0), axis=1, keepdims=True)

    o_ref[...] = jnp.zeros_like(o_ref)
    for b in range(n_b):
        for h in range(H_GLA):
            ks = slice(h * GLA_DK, (h + 1) * GLA_DK)
            vs = slice(h * GLA_DV, (h + 1) * GLA_DV)
            s_new = (column(decay[b:b + 1, ks]) * st_ref[b, h]
                     + column(kg_ref[b:b + 1, ks]) * vg_ref[b:b + 1, vs])
            ns_ref[b, h] = s_new
            q_col = column(qg_ref[b:b + 1, ks] * (GLA_DK ** -0.5))
            o = jnp.sum(q_col * s_new, axis=0, keepdims=True)
            o_ref[b:b + 1, vs] = _head_rms(o) * _silu(og_ref[b:b + 1, vs]) * gm_ref[:, vs]


def gla_sample(zs, gla_wa2, gla_ba, state, g_mix):
    n_b = state.shape[0]
    kw = H_GLA * GLA_DK
    vw = H_GLA * GLA_DV
    return pl.pallas_call(
        functools.partial(_gla_sample_kernel, n_b=n_b),
        grid=(1,),
        in_specs=[pl.BlockSpec((MS, kw), lambda i: (0, C_QG // kw)),
                  pl.BlockSpec((MS, kw), lambda i: (0, C_KG // kw)),
                  pl.BlockSpec((MS, vw), lambda i: (0, C_VG // vw)),
                  pl.BlockSpec((MS, vw), lambda i: (0, C_OG // vw)),
                  pl.BlockSpec((MS, LANE), lambda i: (0, C_SM // LANE)),
                  pl.BlockSpec((LANE, kw), lambda i: (0, 0)),
                  pl.BlockSpec((1, kw), lambda i: (0, 0)),
                  pl.BlockSpec(state.shape, lambda i: (0, 0, 0, 0)),
                  pl.BlockSpec((1, vw), lambda i: (0, (H_NSA + H_SB) * D_HEAD // vw))],
        out_specs=[pl.BlockSpec((MS, vw), lambda i: (0, 0)),
                   pl.BlockSpec(state.shape, lambda i: (0, 0, 0, 0))],
        out_shape=[jax.ShapeDtypeStruct((MS, vw), F32), jax.ShapeDtypeStruct(state.shape, F32)],
        compiler_params=_cp(("arbitrary",)),
        name="gla_sample",
    )(zs, zs, zs, zs, zs, _wa2_padded(gla_wa2), gla_ba.reshape(1, kw), state, g_mix)


def _reorder_w_in(w_in):
    o = np.cumsum([0, 1024, 24, 1536, 1536, 256, 256, 512, 16, 512])
    seg = [w_in[..., o[i]:o[i + 1]] for i in range(9)]
    q_n, gate, kv, sb, qg, kg, vg, alr, og = seg
    pad = jnp.zeros(w_in.shape[:-1] + (N_Z - C_SM - 40,), w_in.dtype)
    return jnp.concatenate([q_n, kv, sb, qg, kg, vg, og, gate, alr, pad], axis=-1).astype(BF16)


def kernel(x_prompt, x_sample, cache_nsa, cache_sb, state_win, state_gla, page_table, c_prompt, c_sample, norm1, norm2, w_ada, b_ada, w_in, gla_wa2, gla_ba, cmp_pe, cmp_w1, cmp_w2, g_mix, w_out, ffn_w1, ffn_w3, ffn_w2, final_norm):
    depth = w_in.shape[0]
    bp, t_len, d = x_prompt.shape
    bs = x_sample.shape[0]
    n_pages = page_table.shape[1]
    page = cache_nsa.shape[2]
    past = n_pages * page
    wbuf = state_win.shape[2]
    mp = bp * t_len
    nsa_w = 4 * G_NSA * D_HEAD
    sb_w = 2 * H_SB * D_HEAD
    win_w = 2 * G_NSA * D_HEAD
    assert bs <= MS and x_sample.shape[1] == 1 and wbuf == WINDOW and t_len >= wbuf

    c_all = jnp.zeros((MS, d), F32).at[:bs].set(c_sample).at[bs:bs + bp].set(c_prompt)
    mod = matmul_bias(jnp.broadcast_to(c_all, (depth, MS, d)), w_ada, b_ada[:, None, :],
                      tm=MS, tn=1024, silu_a=True)
    w_in_r = _reorder_w_in(w_in)
    zero_bias = jnp.zeros((1, 1, N_Z), F32)

    xp = x_prompt.reshape(mp, d)
    xs = jnp.zeros((MS, d), F32).at[:bs].set(x_sample[:, 0])
    outs = {k: [] for k in ("nsa_p", "nsa_s", "sb_p", "sb_s", "win_p", "win_s", "gla_p", "gla_s")}

    for l in range(depth):
        def mod_p(k):
            return mod[l, bs:bs + bp, None, k * d:(k + 1) * d]

        def mod_s(k):
            return mod[l, None, :, k * d:(k + 1) * d]

        gm = g_mix[l][None, :]
        h = norm_mod(xp, norm1[l][None, :], mod_p(1), mod_p(0), tm=512)
        z = matmul_bias(h[None], w_in_r[l][None], zero_bias, tm=2048, tn=640)[0]
        z3 = z.reshape(bp, t_len, N_Z)
        seg = z3[..., C_KV:C_KV + 4 * D_HEAD].reshape(bp, t_len // CMP_STRIDE, CMP_STRIDE, 4, D_HEAD)
        seg = jnp.transpose(seg, (0, 3, 1, 2, 4)).reshape(bp, 4, t_len // CMP_STRIDE, CMP_STRIDE * D_HEAD)
        cmp_kv = compress_segments(seg, cmp_w1[l], cmp_pe[l], cmp_w2[l])
        o_nsa = nsa_prompt(z3, cmp_kv, gm)
        o_sb = sb_prompt(z3, gm)
        o_gla, st_t = gla_prompt(z3, gla_wa2[l], gla_ba[l], gm)
        xp = matmul_resid([o_nsa.reshape(mp, -1), o_sb.reshape(mp, -1), o_gla.reshape(mp, -1)],
                          w_out[l], xp, mod_p(2), tm=2048, tn=512)
        h = norm_mod(xp, norm2[l][None, :], mod_p(4), mod_p(3), tm=512)
        hid = matmul_swiglu(h, ffn_w1[l], ffn_w3[l], tm=2048, tn=512)
        xp = matmul_resid([hid], ffn_w2[l], xp, mod_p(5), tm=1024, tn=512, tk=2816)
        outs["nsa_p"].append(z3[..., C_KV:C_KV + nsa_w].reshape(bp, t_len, 4, G_NSA, D_HEAD))
        outs["sb_p"].append(z3[..., C_SB + H_SB * D_HEAD:C_SB + H_SB * D_HEAD + sb_w]
                            .reshape(bp, t_len, 2, H_SB, D_HEAD))
        outs["win_p"].append(z3[:, t_len - wbuf:, C_KV + nsa_w:C_KV + nsa_w + win_w]
                             .reshape(bp, wbuf, 2, G_NSA, D_HEAD))
        outs["gla_p"].append(jnp.swapaxes(st_t, 2, 3))

        hs = norm_mod(xs, norm1[l][None, :], mod_s(1), mod_s(0), tm=MS)
        zs = matmul_bias(hs[None], w_in_r[l][None], zero_bias, tm=MS, tn=1152)[0]
        seg_s = gather_segments(cache_nsa, page_table, l)
        cmp_s = compress_segments(seg_s, cmp_w1[l], cmp_pe[l], cmp_w2[l])
        q8 = zs[:bs, C_Q:C_Q + H_NSA * D_HEAD].reshape(bs, H_NSA, D_HEAD)
        o_cmp, idx = nsa_sample_select(q8, cmp_s, pos=past)
        idx_flat = idx[:, :G_NSA, :N_SELECT].reshape(-1)
        o_nsa_s = nsa_sample_attend(zs, q8, o_cmp, idx_flat, cache_nsa, state_win, page_table,
                                    gm[0, :H_NSA * D_HEAD].reshape(H_NSA, D_HEAD), l, pos=past)
        o_sb_s = sb_sample(zs, cache_sb, page_table,
                           gm[0, H_NSA * D_HEAD:(H_NSA + H_SB) * D_HEAD].reshape(H_SB, D_HEAD), l)
        o_gla_s, st_new = gla_sample(zs, gla_wa2[l], gla_ba[l], state_gla[l], gm)

        def pad_rows(a):
            return jnp.zeros((MS, a.shape[1]), BF16).at[:bs].set(a.astype(BF16))

        xs = matmul_resid([pad_rows(o_nsa_s.reshape(bs, -1)), pad_rows(o_sb_s.reshape(bs, -1)),
                           o_gla_s.astype(BF16)], w_out[l], xs, mod_s(2), tm=MS, tn=512)
        hs = norm_mod(xs, norm2[l][None, :], mod_s(4), mod_s(3), tm=MS)
        hid_s = matmul_swiglu(hs, ffn_w1[l], ffn_w3[l], tm=MS, tn=512)
        xs = matmul_resid([hid_s], ffn_w2[l], xs, mod_s(5), tm=MS, tn=512, tk=2816)
        outs["nsa_s"].append(zs[:bs, C_KV:C_KV + nsa_w].reshape(bs, 1, 4, G_NSA, D_HEAD))
        outs["sb_s"].append(zs[:bs, C_SB + H_SB * D_HEAD:C_SB + H_SB * D_HEAD + sb_w].reshape(bs, 1, 2, H_SB, D_HEAD))
        win_new = zs[:bs, C_KV + nsa_w:C_KV + nsa_w + win_w].reshape(bs, 1, 2, G_NSA, D_HEAD)
        outs["win_s"].append(jnp.concatenate([state_win[l][:, 1:], win_new], axis=1))
        outs["gla_s"].append(st_new.astype(state_gla.dtype))

    y_prompt = rmsnorm_rows(xp, final_norm[None, :], tm=512).reshape(bp, t_len, d)
    y_sample = rmsnorm_rows(xs, final_norm[None, :], tm=MS)[:bs].reshape(bs, 1, d)
    return (y_prompt, y_sample, jnp.stack(outs["nsa_p"]), jnp.stack(outs["nsa_s"]), jnp.stack(outs["sb_p"]),
            jnp.stack(outs["sb_s"]), jnp.stack(outs["win_p"]), jnp.stack(outs["win_s"]),
            jnp.stack(outs["gla_p"]), jnp.stack(outs["gla_s"]))
```

```python
import functools

import numpy as np
import jax
import jax.numpy as jnp
from jax import lax
from jax.experimental import pallas as pl
from jax.experimental.pallas import tpu as pltpu

F32 = jnp.float32
BF16 = jnp.bfloat16

EPS = 1e-6
D_HEAD = 128
H_NSA = 8
G_NSA = 2
HPG = 4
H_SB = 4
H_GLA = 4
GLA_DK = 64
GLA_DV = 128
GLA_RANK = 16
GLA_TAU = 16.0
GLA_CHUNK = 64
CMP_BLOCK = 32
CMP_STRIDE = 16
SEL_BLOCK = 64
N_SELECT = 16
N_LOCAL = 2
WINDOW = 512
FORCE_SCORE = 1e6
SCALE = D_HEAD ** -0.5
LANE = 128
MS = 16

C_Q = 0
C_KV = 1024
C_SB = 2560
C_QG = 4096
C_KG = 4352
C_VG = 4608
C_OG = 5120
C_SM = 5632
N_Z = 6144
SM_GATE = 0
SM_ALR = 24

NEG = -1e30
PEN = -(2.0 ** 100)
PEN_TEST = -(2.0 ** 90)

VMEM_BIG = 56 * 1024 * 1024


def _cp(sem, vmem=None):
    return pltpu.CompilerParams(dimension_semantics=sem, vmem_limit_bytes=vmem)


def _log_sigmoid(x):
    return jnp.minimum(x, 0.0) - jnp.log(1.0 + jnp.exp(-jnp.abs(x)))


LOG2E = 1.4426950408889634


def _log2_sigmoid(x2):
    return jnp.minimum(x2, 0.0) - jnp.log2(1.0 + jnp.exp2(-jnp.abs(x2)))


def _silu(x):
    return x * (1.0 / (1.0 + jnp.exp(-x)))


def _split2(x):
    hi = x.astype(BF16)
    lo = (x - hi.astype(F32)).astype(BF16)
    return hi, lo


def _split3(x):
    hi = x.astype(BF16)
    r = x - hi.astype(F32)
    mid = r.astype(BF16)
    lo = (r - mid.astype(F32)).astype(BF16)
    return hi, mid, lo


def _dot(a, b):
    return jnp.dot(a, b, preferred_element_type=F32)


def _dot_nt(a, b):
    return lax.dot_general(a, b, (((1,), (1,)), ((), ())), preferred_element_type=F32)


def _dot_tn(a, b):
    return lax.dot_general(a, b, (((0,), (0,)), ((), ())), preferred_element_type=F32)


def _mm_plain_kernel(a_ref, w_ref, b_ref, o_ref, *, silu_a):
    a = a_ref[...]
    if silu_a:
        a = _silu(a.astype(F32))
    o_ref[...] = _dot(a.astype(BF16), w_ref[...].astype(BF16)) + b_ref[...]


def matmul_bias(a, w, b, *, tm, tn, silu_a=False):
    nl, m, k = a.shape
    n = w.shape[2]
    return pl.pallas_call(
        functools.partial(_mm_plain_kernel, silu_a=silu_a),
        grid=(nl, m // tm, n // tn),
        in_specs=[pl.BlockSpec((None, tm, k), lambda l, i, j: (l, i, 0)),
                  pl.BlockSpec((None, k, tn), lambda l, i, j: (l, 0, j)),
                  pl.BlockSpec((None, 1, tn), lambda l, i, j: (l, 0, j))],
        out_specs=pl.BlockSpec((None, tm, tn), lambda l, i, j: (l, i, j)),
        out_shape=jax.ShapeDtypeStruct((nl, m, n), F32),
        compiler_params=_cp(("arbitrary", "arbitrary", "arbitrary"), VMEM_BIG),
        name="matmul_bias",
    )(a, w, b)


PROJ_TN = 1024
ROW_SLOTS = 8


def _project_in_kernel(*refs, emit_rows, tm):
    a_ref, w_ref = refs[:2]
    acc = _dot(a_ref[...], w_ref[...])
    if not emit_rows:
        refs[2][...] = acc
        return
    z_ref, nsa_ref, sb_ref = refs[-3:]
    z_ref[...] = acc
    j = pl.program_id(1)

    def write_rows(rows_ref):
        for s in range(ROW_SLOTS):
            rows_ref[pl.ds(s, tm, stride=ROW_SLOTS), :] = acc[:, s * D_HEAD:(s + 1) * D_HEAD]

    @pl.when(j == C_KV // PROJ_TN)
    def _():
        write_rows(nsa_ref)

    @pl.when(j == (C_SB + H_SB * D_HEAD) // PROJ_TN)
    def _():
        write_rows(sb_ref)


def project_in(a, w_all, layer, rows_prev=None, *, tm, emit_rows):
    m, k = a.shape
    depth, _, n = w_all.shape
    in_specs = [pl.BlockSpec((tm, k), lambda i, j: (i, 0)),
                pl.BlockSpec((None, k, PROJ_TN), lambda i, j: (layer, 0, j))]
    z_spec = pl.BlockSpec((tm, PROJ_TN), lambda i, j: (i, j))
    z_shape = jax.ShapeDtypeStruct((m, n), F32)
    if not emit_rows:
        return pl.pallas_call(
            functools.partial(_project_in_kernel, emit_rows=False, tm=tm),
            grid=(m // tm, n // PROJ_TN), in_specs=in_specs, out_specs=z_spec, out_shape=z_shape,
            compiler_params=_cp(("arbitrary", "arbitrary"), VMEM_BIG), name="project_in_small",
        )(a, w_all)
    rows_shape = jax.ShapeDtypeStruct((depth, m * ROW_SLOTS, D_HEAD), F32)
    rows_spec = pl.BlockSpec((None, tm * ROW_SLOTS, D_HEAD), lambda i, j: (layer, i, 0))
    operands = [a, w_all]
    aliases = {}
    if rows_prev is not None:
        in_specs = in_specs + [pl.BlockSpec(memory_space=pl.ANY)] * 2
        operands += list(rows_prev)
        aliases = {2: 1, 3: 2}
    return pl.pallas_call(
        functools.partial(_project_in_kernel, emit_rows=True, tm=tm),
        grid=(m // tm, n // PROJ_TN), in_specs=in_specs,
        out_specs=[z_spec, rows_spec, rows_spec], out_shape=[z_shape, rows_shape, rows_shape],
        input_output_aliases=aliases,
        compiler_params=_cp(("arbitrary", "arbitrary"), VMEM_BIG), name="project_in",
    )(*operands)


def _mm_swiglu_kernel(a_ref, w1_ref, w3_ref, o_ref):
    a = a_ref[...]
    u = _dot(a, w1_ref[...].astype(BF16))
    v = _dot(a, w3_ref[...].astype(BF16))
    o_ref[...] = (_silu(u) * v).astype(o_ref.dtype)


def matmul_swiglu(a, w1, w3, layer, *, tm, tn):
    m, k = a.shape
    n = w1.shape[2]
    return pl.pallas_call(
        _mm_swiglu_kernel,
        grid=(m // tm, n // tn),
        in_specs=[pl.BlockSpec((tm, k), lambda i, j: (i, 0)),
                  pl.BlockSpec((None, k, tn), lambda i, j: (layer, 0, j)),
                  pl.BlockSpec((None, k, tn), lambda i, j: (layer, 0, j))],
        out_specs=pl.BlockSpec((tm, tn), lambda i, j: (i, j)),
        out_shape=jax.ShapeDtypeStruct((m, n), BF16),
        compiler_params=_cp(("arbitrary", "arbitrary"), VMEM_BIG),
        name="matmul_swiglu",
    )(a, w1, w3)


def _mm_resid_kernel(*refs, a_widths, nk):
    n_a = len(a_widths)
    a_refs = refs[:n_a]
    w_ref, x_ref, g_ref, o_ref = refs[n_a:n_a + 4]
    part = None
    off = 0
    for a_ref, kw in zip(a_refs, a_widths):
        d = _dot(a_ref[...], w_ref[off:off + kw, :].astype(BF16))
        part = d if part is None else part + d
        off += kw
    if nk == 1:
        o_ref[...] = x_ref[...] + g_ref[...] * part
        return
    acc_ref = refs[n_a + 4]
    k = pl.program_id(2)

    @pl.when(k == 0)
    def _():
        acc_ref[...] = part

    @pl.when(k > 0)
    def _():
        acc_ref[...] += part

    @pl.when(k == nk - 1)
    def _():
        o_ref[...] = x_ref[...] + g_ref[...] * acc_ref[...]


def matmul_resid(a_list, w, layer, x, gate, *, tm, tn, tk=None):
    m, n = x.shape
    k_total = w.shape[1]
    a_widths = tuple(a.shape[1] for a in a_list)
    if tk is None:
        tk = k_total
    nk = k_total // tk
    assert nk == 1 or len(a_list) == 1
    g_cnt, g_rows, _ = gate.shape
    rows_per_gate = m // g_cnt
    if nk == 1:
        a_specs = [pl.BlockSpec((tm, kw), lambda i, j, k: (i, 0)) for kw in a_widths]
        kernel_widths = a_widths
    else:
        a_specs = [pl.BlockSpec((tm, tk), lambda i, j, k: (i, k))]
        kernel_widths = (tk,)
    g_blk = 1 if g_rows == 1 else tm
    return pl.pallas_call(
        functools.partial(_mm_resid_kernel, a_widths=kernel_widths, nk=nk),
        grid=(m // tm, n // tn, nk),
        in_specs=a_specs + [
            pl.BlockSpec((None, tk, tn), lambda i, j, k: (layer, k, j)),
            pl.BlockSpec((tm, tn), lambda i, j, k: (i, j)),
            pl.BlockSpec((None, g_blk, tn), lambda i, j, k: ((i * tm) // rows_per_gate, 0, j)),
        ],
        out_specs=pl.BlockSpec((tm, tn), lambda i, j, k: (i, j)),
        out_shape=jax.ShapeDtypeStruct((m, n), F32),
        scratch_shapes=[pltpu.VMEM((tm, tn), F32)] if nk > 1 else [],
        compiler_params=_cp(("arbitrary", "arbitrary", "arbitrary"), VMEM_BIG),
        name="matmul_resid",
    )(*a_list, w, x, gate)


def _norm_mod_kernel(x_ref, g_ref, sc_ref, sh_ref, o_ref):
    x = x_ref[...]
    y = x * lax.rsqrt(jnp.mean(x * x, axis=-1, keepdims=True) + EPS) * g_ref[...]
    o_ref[...] = (y * (1.0 + sc_ref[...]) + sh_ref[...]).astype(o_ref.dtype)


def norm_mod(x, g, sc, sh, *, tm):
    m, d = x.shape
    g_cnt, g_rows, _ = sc.shape
    rows_per_gate = m // g_cnt
    g_blk = 1 if g_rows == 1 else tm
    mod_spec = pl.BlockSpec((None, g_blk, d), lambda i: ((i * tm) // rows_per_gate, 0, 0))
    return pl.pallas_call(
        _norm_mod_kernel,
        grid=(m // tm,),
        in_specs=[pl.BlockSpec((tm, d), lambda i: (i, 0)),
                  pl.BlockSpec((1, d), lambda i: (0, 0)), mod_spec, mod_spec],
        out_specs=pl.BlockSpec((tm, d), lambda i: (i, 0)),
        out_shape=jax.ShapeDtypeStruct((m, d), BF16),
        compiler_params=_cp(("arbitrary",)),
        name="norm_mod",
    )(x, g, sc, sh)


def _rmsnorm_kernel(x_ref, g_ref, o_ref):
    x = x_ref[...]
    o_ref[...] = x * lax.rsqrt(jnp.mean(x * x, axis=-1, keepdims=True) + EPS) * g_ref[...]


def rmsnorm_rows(x, g, *, tm):
    m, d = x.shape
    return pl.pallas_call(
        _rmsnorm_kernel,
        grid=(m // tm,),
        in_specs=[pl.BlockSpec((tm, d), lambda i: (i, 0)), pl.BlockSpec((1, d), lambda i: (0, 0))],
        out_specs=pl.BlockSpec((tm, d), lambda i: (i, 0)),
        out_shape=jax.ShapeDtypeStruct((m, d), F32),
        compiler_params=_cp(("arbitrary",)),
        name="rmsnorm_rows",
    )(x, g)


def _compress_kernel(s_ref, w1_ref, pe_ref, w2_ref, o_ref):
    n_seg = s_ref.shape[0]
    half = s_ref.shape[1]
    s = s_ref[...].astype(BF16)
    w_lo = w1_ref[0].astype(BF16)
    w_hi = w1_ref[1].astype(BF16)
    lo = _dot(s, w_lo)
    hi = _dot(s, w_hi)
    pe = pe_ref[...].astype(BF16)
    bias = _dot(pe[:, :half], w_lo) + _dot(pe[:, half:], w_hi)
    pre = lo + pltpu.roll(hi, n_seg - 1, 0) + bias[0:1, :]
    out = _dot(_silu(pre).astype(BF16), w2_ref[...].astype(BF16))
    row = lax.broadcasted_iota(jnp.int32, out.shape, 0)
    o_ref[...] = jnp.where(row < n_seg - 1, out, 0.0)


def compress_segments(seg, w1, pe, w2):
    b, _, n_seg, half = seg.shape
    w1r = w1.reshape(2, 2, half, D_HEAD)
    pe8 = jnp.broadcast_to(pe.reshape(2, 1, 2 * half), (2, 8, 2 * half))
    return pl.pallas_call(
        _compress_kernel,
        grid=(b, 4),
        in_specs=[pl.BlockSpec((None, None, n_seg, half), lambda i, j: (i, j, 0, 0)),
                  pl.BlockSpec((None, 2, half, D_HEAD), lambda i, j: (j // 2, 0, 0, 0)),
                  pl.BlockSpec((None, 8, 2 * half), lambda i, j: (j // 2, 0, 0)),
                  pl.BlockSpec((None, D_HEAD, D_HEAD), lambda i, j: (j // 2, 0, 0))],
        out_specs=pl.BlockSpec((None, None, n_seg, D_HEAD), lambda i, j: (i, j, 0, 0)),
        out_shape=jax.ShapeDtypeStruct((b, 4, n_seg, D_HEAD), F32),
        compiler_params=_cp(("arbitrary", "arbitrary"), VMEM_BIG),
        name="compress_segments",
    )(seg, w1r, pe8, w2)


def _slope_column(g, rows_per_head):
    n = HPG * rows_per_head
    h = lax.broadcasted_iota(jnp.int32, (n, 1), 0) // rows_per_head + g * HPG
    out = jnp.zeros((n, 1), F32)
    for hh in range(H_NSA):
        out = jnp.where(h == hh, 2.0 ** (-8.0 * (hh + 1) / H_NSA), out)
    return out


def _masked_softmax_rows(s, valid):
    s = jnp.where(valid, s, NEG)
    m = jnp.max(s, axis=-1, keepdims=True)
    e = jnp.where(valid, jnp.exp(s - m), 0.0)
    d = jnp.sum(e, axis=-1, keepdims=True)
    return e / jnp.where(d > 0.0, d, 1.0)


def _select_blocks(score, n_blocks):
    lane = lax.broadcasted_iota(jnp.int32, score.shape, 1)
    rank = jnp.zeros(score.shape, jnp.int32)
    for i in range(n_blocks):
        c = score[:, i:i + 1]
        ahead = (c > score) | ((c == score) & (lane > i))
        rank = rank + ahead.astype(jnp.int32)
    return (rank < N_SELECT) & (score > -jnp.inf)


def _slc_matrix(n_cmp_rows, nc, n_lanes):
    i = np.arange(n_cmp_rows)[:, None]
    j = np.arange(n_lanes)[None, :]
    m = (i >= 4 * j - 1) & (i <= 4 * j + 3) & (i < nc)
    return jnp.asarray(m.astype(np.float32), dtype=BF16)


def _head_rms(o):
    return o * lax.rsqrt(jnp.mean(o * o, axis=-1, keepdims=True) + EPS)


def _nsa_prompt_kernel(q_ref, sm_ref, selk_ref, selv_ref, wink_ref, winv_ref, cmpk_ref, cmpv_ref,
                       kaug_ref, caug_ref, mslc_ref, wbias_ref, gm_ref, o_ref, *, qb, tk):
    g = pl.program_id(1)
    qi = pl.program_id(2)
    q0 = qi * qb
    rows = HPG * qb
    n_seg = cmpk_ref.shape[0]
    nc = n_seg - 1
    t_len = selk_ref.shape[0]
    ns = t_len // SEL_BLOCK

    q = q_ref[...] * SCALE
    qs = jnp.concatenate([q[:, h * D_HEAD:(h + 1) * D_HEAD] for h in range(HPG)], axis=0)
    qs_b = qs.astype(BF16)
    slope = _slope_column(g, qb)
    rpos = q0 + lax.broadcasted_iota(jnp.int32, (rows, 1), 0) % qb
    lane = lax.broadcasted_iota(jnp.int32, (rows, LANE), 1)
    ones_col = jnp.where(lax.broadcasted_iota(jnp.int32, (tk, LANE), 1) == 0, 1.0, 0.0).astype(BF16)

    qa_c = jnp.concatenate([qs_b, jnp.where(lane == 0, slope * CMP_STRIDE, 0.0).astype(BF16)], axis=1)
    ka_c = jnp.concatenate([cmpk_ref[...].astype(BF16), caug_ref[...]], axis=1)
    ci = lax.broadcasted_iota(jnp.int32, (1, n_seg), 1)
    c_valid = (ci * CMP_STRIDE + (CMP_BLOCK - 1) <= rpos) & (ci < nc)
    p_cmp = _masked_softmax_rows(_dot_nt(qa_c, ka_c), c_valid)
    o_cmp = _dot(p_cmp.astype(BF16), cmpv_ref[...].astype(BF16))

    imp = p_cmp[0:qb]
    for h in range(1, HPG):
        imp = imp + p_cmp[h * qb:(h + 1) * qb]
    p_slc = None
    for part in _split3(imp):
        d = _dot_nt(mslc_ref[...], part)
        p_slc = d if p_slc is None else p_slc + d
    p_slc = p_slc[0:SEL_BLOCK]
    blk = lax.broadcasted_iota(jnp.int32, (SEL_BLOCK, qb), 0)
    cur = (q0 + lax.broadcasted_iota(jnp.int32, (1, qb), 1)) // SEL_BLOCK
    forced = (blk == 0) | (blk > cur - N_LOCAL)
    score = jnp.where(forced, FORCE_SCORE, p_slc)
    score = jnp.where((blk <= cur) & (blk < ns), score, -jnp.inf)
    sub = 8
    pieces = [score[v * sub:(v + 1) * sub] for v in range(SEL_BLOCK // sub)]
    ranks = [jnp.zeros((sub, qb), jnp.int32) for _ in pieces]
    sub_row = lax.broadcasted_iota(jnp.int32, (sub, qb), 0)
    for i in range(ns):
        c = pieces[i // sub][i % sub:i % sub + 1, :]
        for v, piece in enumerate(pieces):
            if v < i // sub:
                ahead = (c > piece).astype(jnp.int32)
            elif v > i // sub:
                ahead = (c >= piece).astype(jnp.int32)
            else:
                ahead = jnp.where(sub_row > i % sub, (c >= piece).astype(jnp.int32), (c > piece).astype(jnp.int32))
            ranks[v] = ranks[v] + ahead
    rank = jnp.concatenate(ranks, axis=0)
    sel = (rank < N_SELECT) & (score > -jnp.inf)
    pen_t = jnp.where(sel, 0.0, PEN)
    pen = jnp.transpose(jnp.concatenate([pen_t, jnp.zeros((LANE - SEL_BLOCK, qb), F32)], axis=0))
    pen4 = jnp.concatenate([pen] * HPG, axis=0)
    q_aug = jnp.where(lane < SEL_BLOCK, pen4,
                      jnp.where(lane == SEL_BLOCK, slope * SEL_BLOCK,
                                jnp.where(lane == SEL_BLOCK + 1, slope, 0.0)))
    qa = jnp.concatenate([qs_b, q_aug.astype(BF16)], axis=1)

    def sel_tile(k0, carry, diagonal):
        m_run, acc = carry
        ka = jnp.concatenate([selk_ref[pl.ds(k0, tk), :].astype(BF16), kaug_ref[pl.ds(k0, tk), :]], axis=1)
        va = jnp.concatenate([selv_ref[pl.ds(k0, tk), :].astype(BF16), ones_col], axis=1)
        st = _dot_nt(qa, ka)
        if diagonal:
            st = jnp.where(k0 + lax.broadcasted_iota(jnp.int32, (1, tk), 1) <= rpos, st, NEG)
        m_new = jnp.maximum(m_run, jnp.max(st, axis=-1, keepdims=True))
        p = jnp.exp(st - m_new).astype(BF16)
        return m_new, jnp.exp(m_run - m_new) * acc + _dot(p, va)

    n_full = q0 // tk
    init = (jnp.full((rows, 1), NEG, F32), jnp.zeros((rows, 2 * LANE), F32))
    carry = lax.fori_loop(0, n_full, lambda kt, c: sel_tile(pl.multiple_of(kt * tk, tk), c, False), init)
    _, acc_sel = sel_tile(pl.multiple_of(n_full * tk, tk), carry, True)
    o_sel = acc_sel[:, 0:D_HEAD] / acc_sel[:, D_HEAD:D_HEAD + 1]

    wk = WINDOW + qb
    ks = pl.multiple_of(jnp.maximum(q0 - WINDOW, 0), qb)
    s = _dot_nt(qs_b, wink_ref[pl.ds(ks, wk), :].astype(BF16)) + wbias_ref[...]
    m = jnp.max(s, axis=-1, keepdims=True)
    p = jnp.exp(s - m).astype(BF16)
    ones_w = jnp.where(lax.broadcasted_iota(jnp.int32, (wk, LANE), 1) == 0, 1.0, 0.0).astype(BF16)
    acc_win = _dot(p, jnp.concatenate([winv_ref[pl.ds(ks, wk), :].astype(BF16), ones_w], axis=1))
    o_win = acc_win[:, 0:D_HEAD] / acc_win[:, D_HEAD:D_HEAD + 1]

    gates = 1.0 / (1.0 + jnp.exp(-sm_ref[...]))

    def gate_col(branch):
        cols = []
        for h in range(HPG):
            c0 = SM_GATE + h * 3 + branch
            c1 = SM_GATE + (HPG + h) * 3 + branch
            cols.append(jnp.where(g == 0, gates[:, c0:c0 + 1], gates[:, c1:c1 + 1]))
        return jnp.concatenate(cols, axis=0)

    o = gate_col(0) * o_cmp + gate_col(1) * o_sel + gate_col(2) * o_win
    o = _head_rms(o)
    for h in range(HPG):
        o_ref[:, h * D_HEAD:(h + 1) * D_HEAD] = (
            o[h * qb:(h + 1) * qb] * gm_ref[:, h * D_HEAD:(h + 1) * D_HEAD]).astype(o_ref.dtype)


def _key_aug(t_len):
    pos = np.arange(t_len)[:, None]
    lane = np.arange(LANE)[None, :]
    a = np.where(lane < SEL_BLOCK, (pos // SEL_BLOCK == lane).astype(np.float32),
                 np.where(lane == SEL_BLOCK, (pos // SEL_BLOCK).astype(np.float32),
                          np.where(lane == SEL_BLOCK + 1, (pos % SEL_BLOCK).astype(np.float32), 0.0)))
    return jnp.asarray(a, dtype=BF16)


def _cmp_aug(n_seg):
    a = np.zeros((n_seg, LANE), np.float32)
    a[:, 0] = np.arange(n_seg)
    return jnp.asarray(a, dtype=BF16)


def _window_bias(qb):
    n_pat = WINDOW // qb + 1
    r = np.arange(HPG * qb)
    head = r // qb
    dist = (np.arange(n_pat)[:, None, None] * qb + (r % qb)[None, :, None]
            - np.arange(WINDOW + qb)[None, None, :])
    valid = (dist >= 0) & (dist <= WINDOW)
    out = np.empty((G_NSA,) + dist.shape, np.float32)
    for g in range(G_NSA):
        slope = 2.0 ** (-8.0 * (g * HPG + head + 1) / H_NSA)
        out[g] = np.where(valid, -slope[None, :, None] * dist, NEG)
    return jnp.asarray(out)


def nsa_prompt(z, cmp_kv, g_mix, *, qb=256, tk=512):
    b, t_len, _ = z.shape
    assert t_len // SEL_BLOCK <= SEL_BLOCK and t_len % tk == 0 and tk % qb == 0 and WINDOW % qb == 0
    n_seg = cmp_kv.shape[2]
    assert n_seg <= 256
    gw = HPG * D_HEAD
    kv_blk = C_KV // D_HEAD
    n_pat = WINDOW // qb
    wbias = _window_bias(qb)

    def kv_spec(slot):
        return pl.BlockSpec((None, t_len, D_HEAD), lambda i, g, q: (i, 0, kv_blk + 2 * slot + g))

    return pl.pallas_call(
        functools.partial(_nsa_prompt_kernel, qb=qb, tk=tk),
        grid=(b, G_NSA, t_len // qb),
        in_specs=[pl.BlockSpec((None, qb, gw), lambda i, g, q: (i, q, g)),
                  pl.BlockSpec((None, qb, LANE), lambda i, g, q: (i, q, C_SM // LANE)),
                  kv_spec(2), kv_spec(3), kv_spec(4), kv_spec(5),
                  pl.BlockSpec((None, None, n_seg, D_HEAD), lambda i, g, q: (i, g, 0, 0)),
                  pl.BlockSpec((None, None, n_seg, D_HEAD), lambda i, g, q: (i, 2 + g, 0, 0)),
                  pl.BlockSpec((t_len, LANE), lambda i, g, q: (0, 0)),
                  pl.BlockSpec((n_seg, LANE), lambda i, g, q: (0, 0)),
                  pl.BlockSpec((LANE, n_seg), lambda i, g, q: (0, 0)),
                  pl.BlockSpec((None, None, HPG * qb, WINDOW + qb),
                               lambda i, g, q: (g, jnp.minimum(q, n_pat), 0, 0)),
                  pl.BlockSpec((1, gw), lambda i, g, q: (0, g))],
        out_specs=pl.BlockSpec((None, qb, gw), lambda i, g, q: (i, q, g)),
        out_shape=jax.ShapeDtypeStruct((b, t_len, H_NSA * D_HEAD), BF16),
        compiler_params=_cp(("arbitrary", "arbitrary", "arbitrary"), VMEM_BIG),
        name="nsa_prompt",
    )(z, z, z, z, z, z, cmp_kv, cmp_kv, _key_aug(t_len), _cmp_aug(n_seg),
      jnp.transpose(_slc_matrix(n_seg, n_seg - 1, LANE)), wbias, g_mix)


def _sb_prompt_kernel(q_ref, k_ref, v_ref, gm_ref, o_ref, *, qb, tk):
    qi = pl.program_id(2)
    q0 = qi * qb
    q = (q_ref[...] * (SCALE * LOG2E)).astype(BF16)
    rpos = q0 + lax.broadcasted_iota(jnp.int32, (qb, 1), 0)
    row = lax.broadcasted_iota(jnp.int32, (2 * tk, tk), 0) % tk
    tri2 = jnp.where(row > lax.broadcasted_iota(jnp.int32, (2 * tk, tk), 1), 1.0, 0.0).astype(BF16)
    n_diag = qb // tk

    def tile(k0, carry, masked):
        run, acc = carry
        z = _dot_nt(q, k_ref[pl.ds(k0, tk), :].astype(BF16))
        lb = _log2_sigmoid(z)
        lr = lb - z
        if masked:
            mask = (k0 + lax.broadcasted_iota(jnp.int32, (1, tk), 1)) < rpos
            lr = jnp.where(mask, lr, 0.0)
        hi, lo = _split2(lr)
        after = _dot(jnp.concatenate([hi, lo], axis=1), tri2) + run
        a = jnp.exp2(lb + after)
        if masked:
            a = jnp.where(mask, a, 0.0)
        acc = acc + _dot(a.astype(BF16), v_ref[pl.ds(k0, tk), :].astype(BF16))
        return after[:, 0:1] + lr[:, 0:1], acc

    carry = (jnp.zeros((qb, 1), F32), jnp.zeros((qb, D_HEAD), F32))
    for it in range(n_diag):
        carry = tile(pl.multiple_of(q0 + (n_diag - 1 - it) * tk, tk), carry, True)

    def full_tiles(it, carry):
        for u in range(n_diag):
            carry = tile(pl.multiple_of(q0 - (it * n_diag + u + 1) * tk, tk), carry, False)
        return carry

    _, acc = lax.fori_loop(0, qi, full_tiles, carry)
    o_ref[...] = (_head_rms(acc) * gm_ref[...]).astype(o_ref.dtype)


def sb_prompt(z, g_mix, *, qb=512, tk=256):
    b, t_len, _ = z.shape
    assert qb % tk == 0 and t_len % qb == 0
    blk = C_SB // D_HEAD
    gm_blk = (H_NSA * D_HEAD) // D_HEAD
    return pl.pallas_call(
        functools.partial(_sb_prompt_kernel, qb=qb, tk=tk),
        grid=(b, H_SB, t_len // qb),
        in_specs=[pl.BlockSpec((None, qb, D_HEAD), lambda i, h, q: (i, q, blk + h)),
                  pl.BlockSpec((None, t_len, D_HEAD), lambda i, h, q: (i, 0, blk + H_SB + h)),
                  pl.BlockSpec((None, t_len, D_HEAD), lambda i, h, q: (i, 0, blk + 2 * H_SB + h)),
                  pl.BlockSpec((1, D_HEAD), lambda i, h, q: (0, gm_blk + h))],
        out_specs=pl.BlockSpec((None, qb, D_HEAD), lambda i, h, q: (i, q, h)),
        out_shape=jax.ShapeDtypeStruct((b, t_len, H_SB * D_HEAD), BF16),
        compiler_params=_cp(("arbitrary", "arbitrary", "arbitrary")),
        name="sb_prompt",
    )(z, z, z, g_mix)


N_LEVELS = 6


def _gla_level_tables():
    c = GLA_CHUNK
    idx = np.arange(c)
    sums = np.zeros(((N_LEVELS + 1) * c, c), np.float32)
    pair = np.zeros((N_LEVELS + 1, c, c), np.float32)
    sums[:c] = (idx[None, :] <= idx[:, None])
    pair[0] = np.eye(c)
    for lv in range(N_LEVELS):
        mid = ((idx >> (lv + 1)) << (lv + 1)) + (1 << lv)
        upper = idx >= mid
        j = idx[None, :]
        in_up = upper[:, None] & (j >= mid[:, None]) & (j <= idx[:, None])
        in_lo = (~upper)[:, None] & (j > idx[:, None]) & (j <= mid[:, None] - 1)
        sums[(lv + 1) * c:(lv + 2) * c] = in_up | in_lo
        same = (idx[:, None] >> (lv + 1)) == (idx[None, :] >> (lv + 1))
        pair[lv + 1] = same & upper[:, None] & (~upper)[None, :]
    return jnp.asarray(sums, dtype=BF16), jnp.asarray(pair, dtype=F32)


def _gla_prompt_kernel(qg_ref, kg_ref, vg_ref, og_ref, sm_ref, wa2_ref, ba_ref, sums_ref, pair_ref,
                       gm_ref, o_ref, st_ref, state_ref, *, tb):
    nt = pl.program_id(1)
    c = GLA_CHUNK

    @pl.when(nt == 0)
    def _():
        state_ref[...] = jnp.zeros_like(state_ref)

    wa2 = wa2_ref[...].astype(BF16)
    sums = sums_ref[...]

    def chunk(ci, _):
        r0 = pl.multiple_of(ci * c, c)
        rows = pl.ds(r0, c)
        x = _dot(sm_ref[rows, :].astype(BF16), wa2) + ba_ref[...]
        loga = _log_sigmoid(x) * (1.0 / GLA_TAU)
        hi, lo = _split2(loga)
        dsum = _dot(sums, hi) + _dot(sums, lo)
        cb = dsum[0:c]
        q = qg_ref[rows, :] * (GLA_DK ** -0.5)
        k = kg_ref[rows, :]
        v = vg_ref[rows, :].astype(BF16)
        qf = [q.astype(BF16)]
        kf = [k.astype(BF16)]
        for lv in range(N_LEVELS):
            e = jnp.exp(dsum[(lv + 1) * c:(lv + 2) * c])
            qf.append((q * e).astype(BF16))
            kf.append((k * e).astype(BF16))
        c_last = cb[c - 1:c, :]
        q_in = (q * jnp.exp(cb)).astype(BF16)
        k_out = (k * jnp.exp(c_last - cb)).astype(BF16)
        decay = jnp.exp(c_last)
        for h in range(H_GLA):
            ks = slice(h * GLA_DK, (h + 1) * GLA_DK)
            vs = slice(h * GLA_DV, (h + 1) * GLA_DV)
            att = None
            for lv in range(N_LEVELS + 1):
                term = pair_ref[lv] * _dot_nt(qf[lv][:, ks], kf[lv][:, ks])
                att = term if att is None else att + term
            s_t = state_ref[h]
            o_h = _dot(att.astype(BF16), v[:, vs]) + _dot_nt(q_in[:, ks], s_t.astype(BF16))
            state_ref[h] = s_t * decay[:, ks] + _dot_tn(v[:, vs], k_out[:, ks])
            og = og_ref[rows, vs]
            o_ref[rows, vs] = (_head_rms(o_h) * _silu(og) * gm_ref[:, vs]).astype(o_ref.dtype)
        return 0

    lax.fori_loop(0, tb // c, chunk, 0, unroll=2)

    @pl.when(nt == pl.num_programs(1) - 1)
    def _():
        st_ref[...] = state_ref[...]


def _wa2_padded(gla_wa2):
    w = jnp.zeros((LANE, H_GLA * GLA_DK), F32)
    return w.at[SM_ALR:SM_ALR + GLA_RANK].set(gla_wa2)


def gla_prompt(z, gla_wa2, gla_ba, g_mix, *, tb=512):
    b, t_len, _ = z.shape
    kw = H_GLA * GLA_DK
    vw = H_GLA * GLA_DV
    sums, pair = _gla_level_tables()
    n_rows = sums.shape[0]
    return pl.pallas_call(
        functools.partial(_gla_prompt_kernel, tb=tb),
        grid=(b, t_len // tb),
        in_specs=[pl.BlockSpec((None, tb, kw), lambda i, n: (i, n, C_QG // kw)),
                  pl.BlockSpec((None, tb, kw), lambda i, n: (i, n, C_KG // kw)),
                  pl.BlockSpec((None, tb, vw), lambda i, n: (i, n, C_VG // vw)),
                  pl.BlockSpec((None, tb, vw), lambda i, n: (i, n, C_OG // vw)),
                  pl.BlockSpec((None, tb, LANE), lambda i, n: (i, n, C_SM // LANE)),
                  pl.BlockSpec((LANE, kw), lambda i, n: (0, 0)),
                  pl.BlockSpec((1, kw), lambda i, n: (0, 0)),
                  pl.BlockSpec((n_rows, GLA_CHUNK), lambda i, n: (0, 0)),
                  pl.BlockSpec((N_LEVELS + 1, GLA_CHUNK, GLA_CHUNK), lambda i, n: (0, 0, 0)),
                  pl.BlockSpec((1, vw), lambda i, n: (0, (H_NSA + H_SB) * D_HEAD // vw))],
        out_specs=[pl.BlockSpec((None, tb, vw), lambda i, n: (i, n, 0)),
                   pl.BlockSpec((None, H_GLA, GLA_DV, GLA_DK), lambda i, n: (i, 0, 0, 0))],
        out_shape=[jax.ShapeDtypeStruct((b, t_len, vw), BF16),
                   jax.ShapeDtypeStruct((b, H_GLA, GLA_DV, GLA_DK), F32)],
        scratch_shapes=[pltpu.VMEM((H_GLA, GLA_DV, GLA_DK), F32)],
        compiler_params=_cp(("arbitrary", "arbitrary")),
        name="gla_prompt",
    )(z, z, z, z, z, _wa2_padded(gla_wa2), gla_ba.reshape(1, kw), sums, pair, g_mix)


PAGES_PER_STEP = 8


def _gather_segments_kernel(pt_ref, *refs):
    page_refs = refs[:PAGES_PER_STEP]
    o_ref = refs[PAGES_PER_STEP]
    slots = 4 * G_NSA
    seg_per_page = page_refs[0].shape[0] // (slots * CMP_STRIDE)
    for pair in range(PAGES_PER_STEP // 2):
        r0 = pair * 2 * seg_per_page
        for cg in range(4):
            for p in range(CMP_STRIDE):
                rows = [page_refs[2 * pair + u][pl.ds(p * slots + cg, seg_per_page, stride=CMP_STRIDE * slots), :]
                        for u in range(2)]
                o_ref[cg, r0:r0 + 2 * seg_per_page, p * D_HEAD:(p + 1) * D_HEAD] = (
                    jnp.concatenate(rows, axis=0).astype(o_ref.dtype))


def _rows_view(cache):
    depth, n_pool, page, a, b, d = cache.shape
    return cache.reshape(depth, n_pool, page * a * b, d)


def gather_segments(cache_nsa, page_table, layer):
    page = cache_nsa.shape[2]
    b, n_pages = page_table.shape
    cache = _rows_view(cache_nsa)
    seg_per_page = page // CMP_STRIDE
    n_seg = n_pages * seg_per_page
    steps = n_pages // PAGES_PER_STEP

    def page_spec(u):
        return pl.BlockSpec((None, None, cache.shape[2], D_HEAD),
                            lambda i, s, pt: (layer, pt[i * n_pages + s * PAGES_PER_STEP + u], 0, 0))

    return pl.pallas_call(
        _gather_segments_kernel,
        grid_spec=pltpu.PrefetchScalarGridSpec(
            num_scalar_prefetch=1,
            grid=(b, steps),
            in_specs=[page_spec(u) for u in range(PAGES_PER_STEP)],
            out_specs=pl.BlockSpec((None, 4, PAGES_PER_STEP * seg_per_page, CMP_STRIDE * D_HEAD),
                                   lambda i, s, pt: (i, 0, s, 0)),
        ),
        out_shape=jax.ShapeDtypeStruct((b, 4, n_seg, CMP_STRIDE * D_HEAD), BF16),
        compiler_params=_cp(("arbitrary", "arbitrary")),
        name="gather_segments",
    )(page_table.reshape(-1), *([cache] * PAGES_PER_STEP))


def _nsa_sample_select_kernel(q_ref, cmp_ref, mslc_ref, ocmp_ref, idx_ref, *, pos, n_blk_lanes):
    n_seg = cmp_ref.shape[1]
    nc = n_seg - 1
    ns = pos // SEL_BLOCK + 1
    cur = pos // SEL_BLOCK
    q = (q_ref[...] * SCALE).astype(BF16)
    row = lax.broadcasted_iota(jnp.int32, (H_NSA, 1), 0)
    slope = _slope_column(0, 1)
    slope = jnp.concatenate([slope, _slope_column(1, 1)], axis=0)
    s = jnp.where(row < HPG, _dot_nt(q, cmp_ref[0].astype(BF16)), _dot_nt(q, cmp_ref[1].astype(BF16)))
    ci = lax.broadcasted_iota(jnp.int32, (1, n_seg), 1)
    c_dist = pos - (ci * CMP_STRIDE + (CMP_BLOCK - 1))
    p = _masked_softmax_rows(s - slope * c_dist.astype(F32), (c_dist >= 0) & (ci < nc))
    pb = p.astype(BF16)
    ocmp_ref[...] = jnp.where(row < HPG, _dot(pb, cmp_ref[2].astype(BF16)), _dot(pb, cmp_ref[3].astype(BF16)))

    imp = jnp.concatenate([jnp.sum(p[g * HPG:(g + 1) * HPG], axis=0, keepdims=True) for g in range(G_NSA)]
                          + [jnp.zeros((H_NSA - G_NSA, n_seg), F32)], axis=0)
    p_slc = None
    for part in _split3(imp):
        d = _dot(part, mslc_ref[...])
        p_slc = d if p_slc is None else p_slc + d
    blk = lax.broadcasted_iota(jnp.int32, p_slc.shape, 1)
    forced = (blk == 0) | (blk > cur - N_LOCAL)
    score = jnp.where(forced, FORCE_SCORE, p_slc)
    score = jnp.where(blk <= cur, score, -jnp.inf)
    sel = _select_blocks(score, ns)
    upper = (lax.broadcasted_iota(jnp.int32, (n_blk_lanes, n_blk_lanes), 0)
             < lax.broadcasted_iota(jnp.int32, (n_blk_lanes, n_blk_lanes), 1))
    before = _dot(jnp.where(sel, 1.0, 0.0).astype(BF16), jnp.where(upper, 1.0, 0.0).astype(BF16))
    blk_f = blk.astype(F32)
    out_lane = lax.broadcasted_iota(jnp.int32, (H_NSA, LANE), 1)
    out = jnp.zeros((H_NSA, LANE), F32)
    for n in range(N_SELECT):
        v = jnp.sum(jnp.where(sel & (before == float(n)), blk_f, 0.0), axis=-1, keepdims=True)
        out = jnp.where(out_lane == n, v, out)
    idx_ref[...] = out.astype(jnp.int32)


def nsa_sample_select(q8, cmp_kv, *, pos):
    b = q8.shape[0]
    n_seg = cmp_kv.shape[2]
    ns = pos // SEL_BLOCK + 1
    assert ns >= N_SELECT
    n_blk_lanes = -(-ns // LANE) * LANE
    return pl.pallas_call(
        functools.partial(_nsa_sample_select_kernel, pos=pos, n_blk_lanes=n_blk_lanes),
        grid=(b,),
        in_specs=[pl.BlockSpec((None, H_NSA, D_HEAD), lambda i: (i, 0, 0)),
                  pl.BlockSpec((None, 4, n_seg, D_HEAD), lambda i: (i, 0, 0, 0)),
                  pl.BlockSpec((n_seg, n_blk_lanes), lambda i: (0, 0))],
        out_specs=[pl.BlockSpec((None, H_NSA, D_HEAD), lambda i: (i, 0, 0)),
                   pl.BlockSpec((None, H_NSA, LANE), lambda i: (i, 0, 0))],
        out_shape=[jax.ShapeDtypeStruct((b, H_NSA, D_HEAD), F32),
                   jax.ShapeDtypeStruct((b, H_NSA, LANE), jnp.int32)],
        compiler_params=_cp(("arbitrary",)),
        name="nsa_sample_select",
    )(q8, cmp_kv, _slc_matrix(n_seg, n_seg - 1, n_blk_lanes))


def _softmax_with_new(s, valid, s_new, v_mat, v_new):
    s = jnp.where(valid, s, NEG)
    m = jnp.maximum(jnp.max(s, axis=-1, keepdims=True), s_new)
    e = jnp.where(valid, jnp.exp(s - m), 0.0)
    e_new = jnp.exp(s_new - m)
    d = jnp.sum(e, axis=-1, keepdims=True) + e_new
    return (_dot(e.astype(BF16), v_mat) + e_new.astype(BF16).astype(F32) * v_new) / d


def _nsa_sample_attend_kernel(idx_ref, pt_ref, *refs, pos):
    sel_refs = refs[:N_SELECT]
    (q_ref, nk_ref, nv_ref, nwk_ref, nwv_ref, sm_ref, win_ref, ocmp_ref, gm_ref, o_ref) = refs[N_SELECT:]
    b = pl.program_id(0)
    g = pl.program_id(1)
    n_past_blocks = pos // SEL_BLOCK
    q = (q_ref[...] * SCALE).astype(BF16)
    qf = q.astype(F32)
    slope = jnp.concatenate([_slope_column(0, 1), _slope_column(1, 1)], axis=0)

    def new_row(ref):
        return ref[pl.ds(b, 1), :].astype(BF16)

    def slot_rows(ref, slot, n_tok, n_slots):
        both = [ref[pl.ds(slot * G_NSA + gg, n_tok, stride=n_slots), :] for gg in range(G_NSA)]
        return jnp.where(g == 0, both[0], both[1]).astype(BF16)

    k_all = jnp.concatenate([slot_rows(r, 2, SEL_BLOCK, 4 * G_NSA) for r in sel_refs], axis=0)
    v_all = jnp.concatenate([slot_rows(r, 3, SEL_BLOCK, 4 * G_NSA) for r in sel_refs], axis=0)
    width = N_SELECT * SEL_BLOCK
    lane = lax.broadcasted_iota(jnp.int32, (1, width), 1)
    tok = jnp.zeros((1, width), jnp.int32)
    in_cache = jnp.zeros((1, width), jnp.bool_)
    for n in range(N_SELECT):
        blk_id = idx_ref[(b * G_NSA + g) * N_SELECT + n]
        here = (lane // SEL_BLOCK) == n
        tok = jnp.where(here, blk_id * SEL_BLOCK + lane % SEL_BLOCK, tok)
        in_cache = in_cache | (here & (blk_id < n_past_blocks))
    dist = pos - tok
    s = _dot_nt(q, k_all) - slope * dist.astype(F32)
    k_new = new_row(nk_ref)
    s_new = jnp.sum(qf * k_new.astype(F32), axis=-1, keepdims=True)
    o_sel = _softmax_with_new(s, in_cache & (dist >= 0), s_new, v_all, new_row(nv_ref).astype(F32))

    n_win = win_ref.shape[0] // (2 * G_NSA)
    wi = lax.broadcasted_iota(jnp.int32, (1, n_win), 1)
    w_dist = n_win - wi
    s = _dot_nt(q, slot_rows(win_ref, 0, n_win, 2 * G_NSA)) - slope * w_dist.astype(F32)
    s_new = jnp.sum(qf * new_row(nwk_ref).astype(F32), axis=-1, keepdims=True)
    o_win = _softmax_with_new(s, w_dist <= WINDOW, s_new, slot_rows(win_ref, 1, n_win, 2 * G_NSA),
                              new_row(nwv_ref).astype(F32))

    gates = 1.0 / (1.0 + jnp.exp(-sm_ref[pl.ds(b, 1), :]))
    hrow = lax.broadcasted_iota(jnp.int32, (H_NSA, LANE), 0)
    glane = lax.broadcasted_iota(jnp.int32, (H_NSA, LANE), 1)

    def gate_col(branch):
        return jnp.sum(jnp.where(glane == SM_GATE + 3 * hrow + branch, gates, 0.0), axis=-1, keepdims=True)

    o = gate_col(0) * ocmp_ref[...] + gate_col(1) * o_sel + gate_col(2) * o_win
    o = _head_rms(o) * gm_ref[...]
    o_ref[...] = jnp.where(g == 0, o[0:HPG], o[HPG:H_NSA])


def nsa_sample_attend(zs, q8, o_cmp, idx, cache_nsa, state_win, page_table, gm8, layer, *, pos):
    page = cache_nsa.shape[2]
    b, n_pages = page_table.shape
    halves = page // SEL_BLOCK
    cache = _rows_view(cache_nsa)
    blk_rows = cache.shape[2] // halves
    win = _rows_view(state_win)
    last_blk = pos // SEL_BLOCK - 1
    kv_blk = C_KV // D_HEAD

    def sel_spec(n):
        def index_map(i, g, idx_ref, pt_ref):
            blk_id = jnp.minimum(idx_ref[(i * G_NSA + g) * N_SELECT + n], last_blk)
            return (layer, pt_ref[i * n_pages + blk_id // halves], blk_id % halves, 0)
        return pl.BlockSpec((None, None, blk_rows, D_HEAD), index_map)

    def zs_spec(slot):
        return pl.BlockSpec((MS, D_HEAD), lambda i, g, a, c: (0, kv_blk + 2 * slot + g))

    in_specs = ([sel_spec(n) for n in range(N_SELECT)]
                + [pl.BlockSpec((None, H_NSA, D_HEAD), lambda i, g, a, c: (i, 0, 0)),
                   zs_spec(2), zs_spec(3), zs_spec(4), zs_spec(5),
                   pl.BlockSpec((MS, LANE), lambda i, g, a, c: (0, C_SM // LANE)),
                   pl.BlockSpec((None, None, win.shape[2], D_HEAD), lambda i, g, a, c: (layer, i, 0, 0)),
                   pl.BlockSpec((None, H_NSA, D_HEAD), lambda i, g, a, c: (i, 0, 0)),
                   pl.BlockSpec((H_NSA, D_HEAD), lambda i, g, a, c: (0, 0))])
    return pl.pallas_call(
        functools.partial(_nsa_sample_attend_kernel, pos=pos),
        grid_spec=pltpu.PrefetchScalarGridSpec(
            num_scalar_prefetch=2,
            grid=(b, G_NSA),
            in_specs=in_specs,
            out_specs=pl.BlockSpec((None, None, HPG, D_HEAD), lambda i, g, a, c: (i, g, 0, 0)),
        ),
        out_shape=jax.ShapeDtypeStruct((b, G_NSA, HPG, D_HEAD), F32),
        compiler_params=_cp(("arbitrary", "arbitrary")),
        name="nsa_sample_attend",
    )(idx, page_table.reshape(-1), *([cache] * N_SELECT), q8, zs, zs, zs, zs, zs, win, o_cmp, gm8)


SB_PAGES_PER_STEP = 16


def _sb_sample_kernel(pt_ref, *refs):
    page_refs = refs[:SB_PAGES_PER_STEP]
    q_ref, gm_ref, o_ref, run_ref, acc_ref = refs[SB_PAGES_PER_STEP:]
    b = pl.program_id(0)
    s_idx = pl.program_id(1)
    slots = 2 * H_SB
    page = page_refs[0].shape[0] // slots
    kw = H_SB * D_HEAD
    n_u = SB_PAGES_PER_STEP

    @pl.when(s_idx == 0)
    def _():
        run_ref[...] = jnp.zeros_like(run_ref)
        acc_ref[...] = jnp.zeros_like(acc_ref)

    def heads_on_lanes(ref, first_slot):
        return jnp.concatenate([ref[pl.ds(first_slot + h, page, stride=slots), :].astype(BF16)
                                for h in range(H_SB)], axis=1)

    qrow = q_ref[pl.ds(b, 1), :] * SCALE
    hrow = lax.broadcasted_iota(jnp.int32, (8, kw), 0)
    hlane = lax.broadcasted_iota(jnp.int32, (8, kw), 1) // D_HEAD
    qm = jnp.where(hrow == hlane, qrow, 0.0).astype(BF16)
    tri = (lax.broadcasted_iota(jnp.int32, (page, page), 0) > lax.broadcasted_iota(jnp.int32, (page, page), 1))
    tri = jnp.where(tri, 1.0, 0.0).astype(BF16)

    k_all = jnp.concatenate([heads_on_lanes(r, 0) for r in page_refs], axis=0)
    z = _dot_nt(qm, k_all)
    lb = _log_sigmoid(z)
    lr = lb - z
    lr_rows = jnp.concatenate([lr[:, u * page:(u + 1) * page] for u in range(n_u)], axis=0)
    hi, lo = _split2(lr_rows)
    local = _dot(hi, tri) + _dot(lo, tri)
    total = local[:, 0:1] + lr_rows[:, 0:1]
    run = run_ref[:, 0:1]
    offs = [None] * n_u
    for u in range(n_u - 1, -1, -1):
        offs[u] = run
        run = run + total[8 * u:8 * u + 8]
    run_ref[...] = jnp.broadcast_to(run, run_ref.shape)
    after = jnp.concatenate([local[8 * u:8 * u + 8] + offs[u] for u in range(n_u)], axis=1)
    a = jnp.exp(lb + after)
    v_all = jnp.concatenate([heads_on_lanes(r, H_SB) for r in page_refs], axis=0)
    acc = acc_ref[...] + _dot(a.astype(BF16), v_all)
    acc_ref[...] = acc

    @pl.when(s_idx == pl.num_programs(1) - 1)
    def _():
        o = jnp.concatenate([acc[h:h + 1, h * D_HEAD:(h + 1) * D_HEAD] for h in range(H_SB)], axis=0)
        o_ref[...] = _head_rms(o) * gm_ref[...]


def sb_sample(zs, cache_sb, page_table, gm4, layer):
    b, n_pages = page_table.shape
    cache = _rows_view(cache_sb)
    steps = n_pages // SB_PAGES_PER_STEP

    def page_spec(u):
        return pl.BlockSpec(
            (None, None, cache.shape[2], D_HEAD),
            lambda i, s, pt: (layer, pt[i * n_pages + (steps - 1 - s) * SB_PAGES_PER_STEP + u], 0, 0))

    return pl.pallas_call(
        _sb_sample_kernel,
        grid_spec=pltpu.PrefetchScalarGridSpec(
            num_scalar_prefetch=1,
            grid=(b, steps),
            in_specs=[page_spec(u) for u in range(SB_PAGES_PER_STEP)]
            + [pl.BlockSpec((MS, H_SB * D_HEAD), lambda i, s, pt: (0, C_SB // (H_SB * D_HEAD))),
               pl.BlockSpec((H_SB, D_HEAD), lambda i, s, pt: (0, 0))],
            out_specs=pl.BlockSpec((None, H_SB, D_HEAD), lambda i, s, pt: (i, 0, 0)),
            scratch_shapes=[pltpu.VMEM((8, LANE), F32), pltpu.VMEM((8, H_SB * D_HEAD), F32)],
        ),
        out_shape=jax.ShapeDtypeStruct((b, H_SB, D_HEAD), F32),
        compiler_params=_cp(("arbitrary", "arbitrary"), VMEM_BIG),
        name="sb_sample",
    )(page_table.reshape(-1), *([cache] * SB_PAGES_PER_STEP), zs, gm4)


def _gla_sample_kernel(qg_ref, kg_ref, vg_ref, og_ref, sm_ref, wa2_ref, ba_ref, st_ref, gm_ref,
                       o_ref, ns_ref, *, n_b):
    x = _dot(sm_ref[...].astype(BF16), wa2_ref[...].astype(BF16)) + ba_ref[...]
    decay = jnp.exp(_log_sigmoid(x) * (1.0 / GLA_TAU))
    eye = (lax.broadcasted_iota(jnp.int32, (GLA_DK, GLA_DK), 0)
           == lax.broadcasted_iota(jnp.int32, (GLA_DK, GLA_DK), 1))

    def column(row):
        return jnp.sum(jnp.where(eye, row, 0.0), axis=1, keepdims=True)

    o_ref[...] = jnp.zeros_like(o_ref)
    for b in range(n_b):
        for h in range(H_GLA):
            ks = slice(h * GLA_DK, (h + 1) * GLA_DK)
            vs = slice(h * GLA_DV, (h + 1) * GLA_DV)
            s_new = (column(decay[b:b + 1, ks]) * st_ref[b, h]
                     + column(kg_ref[b:b + 1, ks]) * vg_ref[b:b + 1, vs])
            ns_ref[b, h] = s_new
            q_col = column(qg_ref[b:b + 1, ks] * (GLA_DK ** -0.5))
            o = jnp.sum(q_col * s_new, axis=0, keepdims=True)
            o_ref[b:b + 1, vs] = _head_rms(o) * _silu(og_ref[b:b + 1, vs]) * gm_ref[:, vs]


def gla_sample(zs, gla_wa2, gla_ba, state, g_mix):
    n_b = state.shape[0]
    kw = H_GLA * GLA_DK
    vw = H_GLA * GLA_DV
    return pl.pallas_call(
        functools.partial(_gla_sample_kernel, n_b=n_b),
        grid=(1,),
        in_specs=[pl.BlockSpec((MS, kw), lambda i: (0, C_QG // kw)),
                  pl.BlockSpec((MS, kw), lambda i: (0, C_KG // kw)),
                  pl.BlockSpec((MS, vw), lambda i: (0, C_VG // vw)),
                  pl.BlockSpec((MS, vw), lambda i: (0, C_OG // vw)),
                  pl.BlockSpec((MS, LANE), lambda i: (0, C_SM // LANE)),
                  pl.BlockSpec((LANE, kw), lambda i: (0, 0)),
                  pl.BlockSpec((1, kw), lambda i: (0, 0)),
                  pl.BlockSpec(state.shape, lambda i: (0, 0, 0, 0)),
                  pl.BlockSpec((1, vw), lambda i: (0, (H_NSA + H_SB) * D_HEAD // vw))],
        out_specs=[pl.BlockSpec((MS, vw), lambda i: (0, 0)),
                   pl.BlockSpec(state.shape, lambda i: (0, 0, 0, 0))],
        out_shape=[jax.ShapeDtypeStruct((MS, vw), F32), jax.ShapeDtypeStruct(state.shape, F32)],
        compiler_params=_cp(("arbitrary",)),
        name="gla_sample",
    )(zs, zs, zs, zs, zs, _wa2_padded(gla_wa2), gla_ba.reshape(1, kw), state, g_mix)


def _reorder_w_in(w_in):
    o = np.cumsum([0, 1024, 24, 1536, 1536, 256, 256, 512, 16, 512])
    seg = [w_in[..., o[i]:o[i + 1]] for i in range(9)]
    q_n, gate, kv, sb, qg, kg, vg, alr, og = seg
    pad = jnp.zeros(w_in.shape[:-1] + (N_Z - C_SM - 40,), w_in.dtype)
    return jnp.concatenate([q_n, kv, sb, qg, kg, vg, og, gate, alr, pad], axis=-1).astype(BF16)


def kernel(x_prompt, x_sample, cache_nsa, cache_sb, state_win, state_gla, page_table, c_prompt, c_sample, norm1, norm2, w_ada, b_ada, w_in, gla_wa2, gla_ba, cmp_pe, cmp_w1, cmp_w2, g_mix, w_out, ffn_w1, ffn_w3, ffn_w2, final_norm):
    depth = w_in.shape[0]
    bp, t_len, d = x_prompt.shape
    bs = x_sample.shape[0]
    n_pages = page_table.shape[1]
    page = cache_nsa.shape[2]
    past = n_pages * page
    wbuf = state_win.shape[2]
    mp = bp * t_len
    nsa_w = 4 * G_NSA * D_HEAD
    sb_w = 2 * H_SB * D_HEAD
    win_w = 2 * G_NSA * D_HEAD
    assert bs <= MS and x_sample.shape[1] == 1 and wbuf == WINDOW and t_len >= wbuf

    c_all = jnp.zeros((MS, d), F32).at[:bs].set(c_sample).at[bs:bs + bp].set(c_prompt)
    mod = matmul_bias(jnp.broadcast_to(c_all, (depth, MS, d)), w_ada, b_ada[:, None, :],
                      tm=MS, tn=1024, silu_a=True)
    w_in_r = _reorder_w_in(w_in)

    xp = x_prompt.reshape(mp, d)
    xs = jnp.zeros((MS, d), F32).at[:bs].set(x_sample[:, 0])
    outs = {k: [] for k in ("nsa_s", "sb_s", "win_p", "win_s", "gla_p", "gla_s")}
    rows_p = None

    for l in range(depth):
        def mod_p(k):
            return mod[l, bs:bs + bp, None, k * d:(k + 1) * d]

        def mod_s(k):
            return mod[l, None, :, k * d:(k + 1) * d]

        gm = g_mix[l][None, :]
        h = norm_mod(xp, norm1[l][None, :], mod_p(1), mod_p(0), tm=512)
        z, rows_nsa, rows_sb = project_in(h, w_in_r, l, rows_p, tm=1024, emit_rows=True)
        rows_p = (rows_nsa, rows_sb)
        z3 = z.reshape(bp, t_len, N_Z)
        seg = z3[..., C_KV:C_KV + 4 * D_HEAD].reshape(bp, t_len // CMP_STRIDE, CMP_STRIDE, 4, D_HEAD)
        seg = jnp.transpose(seg, (0, 3, 1, 2, 4)).reshape(bp, 4, t_len // CMP_STRIDE, CMP_STRIDE * D_HEAD)
        cmp_kv = compress_segments(seg, cmp_w1[l], cmp_pe[l], cmp_w2[l])
        o_nsa = nsa_prompt(z3, cmp_kv, gm)
        o_sb = sb_prompt(z3, gm)
        o_gla, st_t = gla_prompt(z3, gla_wa2[l], gla_ba[l], gm)
        xp = matmul_resid([o_nsa.reshape(mp, -1), o_sb.reshape(mp, -1), o_gla.reshape(mp, -1)],
                          w_out, l, xp, mod_p(2), tm=2048, tn=512)
        h = norm_mod(xp, norm2[l][None, :], mod_p(4), mod_p(3), tm=512)
        hid = matmul_swiglu(h, ffn_w1, ffn_w3, l, tm=2048, tn=512)
        xp = matmul_resid([hid], ffn_w2, l, xp, mod_p(5), tm=1024, tn=512, tk=2816)
        outs["win_p"].append(z3[:, t_len - wbuf:, C_KV + nsa_w:C_KV + nsa_w + win_w]
                             .reshape(bp, wbuf, 2, G_NSA, D_HEAD))
        outs["gla_p"].append(jnp.swapaxes(st_t, 2, 3))

        hs = norm_mod(xs, norm1[l][None, :], mod_s(1), mod_s(0), tm=MS)
        zs = project_in(hs, w_in_r, l, tm=MS, emit_rows=False)
        seg_s = gather_segments(cache_nsa, page_table, l)
        cmp_s = compress_segments(seg_s, cmp_w1[l], cmp_pe[l], cmp_w2[l])
        q8 = zs[:bs, C_Q:C_Q + H_NSA * D_HEAD].reshape(bs, H_NSA, D_HEAD)
        o_cmp, idx = nsa_sample_select(q8, cmp_s, pos=past)
        idx_flat = idx[:, :G_NSA, :N_SELECT].reshape(-1)
        o_nsa_s = nsa_sample_attend(zs, q8, o_cmp, idx_flat, cache_nsa, state_win, page_table,
                                    gm[0, :H_NSA * D_HEAD].reshape(H_NSA, D_HEAD), l, pos=past)
        o_sb_s = sb_sample(zs, cache_sb, page_table,
                           gm[0, H_NSA * D_HEAD:(H_NSA + H_SB) * D_HEAD].reshape(H_SB, D_HEAD), l)
        o_gla_s, st_new = gla_sample(zs, gla_wa2[l], gla_ba[l], state_gla[l], gm)

        def pad_rows(a):
            return jnp.zeros((MS, a.shape[1]), BF16).at[:bs].set(a.astype(BF16))

        xs = matmul_resid([pad_rows(o_nsa_s.reshape(bs, -1)), pad_rows(o_sb_s.reshape(bs, -1)),
                           o_gla_s.astype(BF16)], w_out, l, xs, mod_s(2), tm=MS, tn=512)
        hs = norm_mod(xs, norm2[l][None, :], mod_s(4), mod_s(3), tm=MS)
        hid_s = matmul_swiglu(hs, ffn_w1, ffn_w3, l, tm=MS, tn=512)
        xs = matmul_resid([hid_s], ffn_w2, l, xs, mod_s(5), tm=MS, tn=512, tk=2816)
        outs["nsa_s"].append(zs[:bs, C_KV:C_KV + nsa_w].reshape(bs, 1, 4, G_NSA, D_HEAD))
        outs["sb_s"].append(zs[:bs, C_SB + H_SB * D_HEAD:C_SB + H_SB * D_HEAD + sb_w].reshape(bs, 1, 2, H_SB, D_HEAD))
        win_new = zs[:bs, C_KV + nsa_w:C_KV + nsa_w + win_w].reshape(bs, 1, 2, G_NSA, D_HEAD)
        outs["win_s"].append(jnp.concatenate([state_win[l][:, 1:], win_new], axis=1))
        outs["gla_s"].append(st_new.astype(state_gla.dtype))

    y_prompt = rmsnorm_rows(xp, final_norm[None, :], tm=512).reshape(bp, t_len, d)
    y_sample = rmsnorm_rows(xs, final_norm[None, :], tm=MS)[:bs].reshape(bs, 1, d)
    nsa_p = rows_p[0].reshape(depth, bp, t_len, 4, G_NSA, D_HEAD)
    sb_p = rows_p[1].reshape(depth, bp, t_len, 2, H_SB, D_HEAD)
    return (y_prompt, y_sample, nsa_p, jnp.stack(outs["nsa_s"]), sb_p,
            jnp.stack(outs["sb_s"]), jnp.stack(outs["win_p"]), jnp.stack(outs["win_s"]),
            jnp.stack(outs["gla_p"]), jnp.stack(outs["gla_s"]))
```

```python
import functools

import numpy as np
import jax
import jax.numpy as jnp
from jax import lax
from jax.experimental import pallas as pl
from jax.experimental.pallas import tpu as pltpu

F32 = jnp.float32
BF16 = jnp.bfloat16

EPS = 1e-6
D_HEAD = 128
H_NSA = 8
G_NSA = 2
HPG = 4
H_SB = 4
H_GLA = 4
GLA_DK = 64
GLA_DV = 128
GLA_RANK = 16
GLA_TAU = 16.0
GLA_CHUNK = 64
CMP_BLOCK = 32
CMP_STRIDE = 16
SEL_BLOCK = 64
N_SELECT = 16
N_LOCAL = 2
WINDOW = 512
FORCE_SCORE = 1e6
SCALE = D_HEAD ** -0.5
LANE = 128
MS = 16

C_Q = 0
C_KV = 1024
C_SB = 2560
C_QG = 4096
C_KG = 4352
C_VG = 4608
C_OG = 5120
C_SM = 5632
N_Z = 6144
SM_GATE = 0
SM_ALR = 24

NEG = -1e30
PEN = -(2.0 ** 100)
PEN_TEST = -(2.0 ** 90)

VMEM_BIG = 56 * 1024 * 1024


def _cp(sem, vmem=None):
    return pltpu.CompilerParams(dimension_semantics=sem, vmem_limit_bytes=vmem)


def _log_sigmoid(x):
    return jnp.minimum(x, 0.0) - jnp.log(1.0 + jnp.exp(-jnp.abs(x)))


LOG2E = 1.4426950408889634


def _log2_sigmoid(x2):
    return jnp.minimum(x2, 0.0) - jnp.log2(1.0 + jnp.exp2(-jnp.abs(x2)))


def _silu(x):
    return x * (1.0 / (1.0 + jnp.exp(-x)))


def _split2(x):
    hi = x.astype(BF16)
    lo = (x - hi.astype(F32)).astype(BF16)
    return hi, lo


def _split3(x):
    hi = x.astype(BF16)
    r = x - hi.astype(F32)
    mid = r.astype(BF16)
    lo = (r - mid.astype(F32)).astype(BF16)
    return hi, mid, lo


def _dot(a, b):
    return jnp.dot(a, b, preferred_element_type=F32)


def _dot_nt(a, b):
    return lax.dot_general(a, b, (((1,), (1,)), ((), ())), preferred_element_type=F32)


def _dot_tn(a, b):
    return lax.dot_general(a, b, (((0,), (0,)), ((), ())), preferred_element_type=F32)


def _mm_plain_kernel(a_ref, w_ref, b_ref, o_ref, *, silu_a):
    a = a_ref[...]
    if silu_a:
        a = _silu(a.astype(F32))
    o_ref[...] = _dot(a.astype(BF16), w_ref[...].astype(BF16)) + b_ref[...]


def matmul_bias(a, w, b, *, tm, tn, silu_a=False):
    nl, m, k = a.shape
    n = w.shape[2]
    return pl.pallas_call(
        functools.partial(_mm_plain_kernel, silu_a=silu_a),
        grid=(nl, m // tm, n // tn),
        in_specs=[pl.BlockSpec((None, tm, k), lambda l, i, j: (l, i, 0)),
                  pl.BlockSpec((None, k, tn), lambda l, i, j: (l, 0, j)),
                  pl.BlockSpec((None, 1, tn), lambda l, i, j: (l, 0, j))],
        out_specs=pl.BlockSpec((None, tm, tn), lambda l, i, j: (l, i, j)),
        out_shape=jax.ShapeDtypeStruct((nl, m, n), F32),
        compiler_params=_cp(("arbitrary", "arbitrary", "arbitrary"), VMEM_BIG),
        name="matmul_bias",
    )(a, w, b)


PROJ_TN = 1024
ROW_SLOTS = 8


def _project_in_kernel(*refs, emit_rows, tm):
    a_ref, w_ref = refs[:2]
    acc = _dot(a_ref[...], w_ref[...])
    if not emit_rows:
        refs[2][...] = acc
        return
    z_ref, nsa_ref, sb_ref = refs[-3:]
    z_ref[...] = acc
    j = pl.program_id(1)

    def write_rows(rows_ref):
        for s in range(ROW_SLOTS):
            rows_ref[pl.ds(s, tm, stride=ROW_SLOTS), :] = acc[:, s * D_HEAD:(s + 1) * D_HEAD]

    @pl.when(j == C_KV // PROJ_TN)
    def _():
        write_rows(nsa_ref)

    @pl.when(j == (C_SB + H_SB * D_HEAD) // PROJ_TN)
    def _():
        write_rows(sb_ref)


def project_in(a, w_all, layer, rows_prev=None, *, tm, emit_rows):
    m, k = a.shape
    depth, _, n = w_all.shape
    in_specs = [pl.BlockSpec((tm, k), lambda i, j: (i, 0)),
                pl.BlockSpec((None, k, PROJ_TN), lambda i, j: (layer, 0, j))]
    z_spec = pl.BlockSpec((tm, PROJ_TN), lambda i, j: (i, j))
    z_shape = jax.ShapeDtypeStruct((m, n), F32)
    if not emit_rows:
        return pl.pallas_call(
            functools.partial(_project_in_kernel, emit_rows=False, tm=tm),
            grid=(m // tm, n // PROJ_TN), in_specs=in_specs, out_specs=z_spec, out_shape=z_shape,
            compiler_params=_cp(("arbitrary", "arbitrary"), VMEM_BIG), name="project_in_small",
        )(a, w_all)
    rows_shape = jax.ShapeDtypeStruct((depth, m * ROW_SLOTS, D_HEAD), F32)
    rows_spec = pl.BlockSpec((None, tm * ROW_SLOTS, D_HEAD), lambda i, j: (layer, i, 0))
    operands = [a, w_all]
    aliases = {}
    if rows_prev is not None:
        in_specs = in_specs + [pl.BlockSpec(memory_space=pl.ANY)] * 2
        operands += list(rows_prev)
        aliases = {2: 1, 3: 2}
    return pl.pallas_call(
        functools.partial(_project_in_kernel, emit_rows=True, tm=tm),
        grid=(m // tm, n // PROJ_TN), in_specs=in_specs,
        out_specs=[z_spec, rows_spec, rows_spec], out_shape=[z_shape, rows_shape, rows_shape],
        input_output_aliases=aliases,
        compiler_params=_cp(("arbitrary", "arbitrary"), VMEM_BIG), name="project_in",
    )(*operands)


def _mm_swiglu_kernel(a_ref, w1_ref, w3_ref, o_ref):
    a = a_ref[...]
    u = _dot(a, w1_ref[...].astype(BF16))
    v = _dot(a, w3_ref[...].astype(BF16))
    o_ref[...] = (_silu(u) * v).astype(o_ref.dtype)


def matmul_swiglu(a, w1, w3, layer, *, tm, tn):
    m, k = a.shape
    n = w1.shape[2]
    return pl.pallas_call(
        _mm_swiglu_kernel,
        grid=(m // tm, n // tn),
        in_specs=[pl.BlockSpec((tm, k), lambda i, j: (i, 0)),
                  pl.BlockSpec((None, k, tn), lambda i, j: (layer, 0, j)),
                  pl.BlockSpec((None, k, tn), lambda i, j: (layer, 0, j))],
        out_specs=pl.BlockSpec((tm, tn), lambda i, j: (i, j)),
        out_shape=jax.ShapeDtypeStruct((m, n), BF16),
        compiler_params=_cp(("arbitrary", "arbitrary"), VMEM_BIG),
        name="matmul_swiglu",
    )(a, w1, w3)


def _mm_resid_kernel(*refs, a_widths, nk):
    n_a = len(a_widths)
    a_refs = refs[:n_a]
    w_ref, x_ref, g_ref, o_ref = refs[n_a:n_a + 4]
    part = None
    off = 0
    for a_ref, kw in zip(a_refs, a_widths):
        d = _dot(a_ref[...], w_ref[off:off + kw, :].astype(BF16))
        part = d if part is None else part + d
        off += kw
    if nk == 1:
        o_ref[...] = x_ref[...] + g_ref[...] * part
        return
    acc_ref = refs[n_a + 4]
    k = pl.program_id(2)

    @pl.when(k == 0)
    def _():
        acc_ref[...] = part

    @pl.when(k > 0)
    def _():
        acc_ref[...] += part

    @pl.when(k == nk - 1)
    def _():
        o_ref[...] = x_ref[...] + g_ref[...] * acc_ref[...]


def matmul_resid(a_list, w, layer, x, gate, *, tm, tn, tk=None):
    m, n = x.shape
    k_total = w.shape[1]
    a_widths = tuple(a.shape[1] for a in a_list)
    if tk is None:
        tk = k_total
    nk = k_total // tk
    assert nk == 1 or len(a_list) == 1
    g_cnt, g_rows, _ = gate.shape
    rows_per_gate = m // g_cnt
    if nk == 1:
        a_specs = [pl.BlockSpec((tm, kw), lambda i, j, k: (i, 0)) for kw in a_widths]
        kernel_widths = a_widths
    else:
        a_specs = [pl.BlockSpec((tm, tk), lambda i, j, k: (i, k))]
        kernel_widths = (tk,)
    g_blk = 1 if g_rows == 1 else tm
    return pl.pallas_call(
        functools.partial(_mm_resid_kernel, a_widths=kernel_widths, nk=nk),
        grid=(m // tm, n // tn, nk),
        in_specs=a_specs + [
            pl.BlockSpec((None, tk, tn), lambda i, j, k: (layer, k, j)),
            pl.BlockSpec((tm, tn), lambda i, j, k: (i, j)),
            pl.BlockSpec((None, g_blk, tn), lambda i, j, k: ((i * tm) // rows_per_gate, 0, j)),
        ],
        out_specs=pl.BlockSpec((tm, tn), lambda i, j, k: (i, j)),
        out_shape=jax.ShapeDtypeStruct((m, n), F32),
        scratch_shapes=[pltpu.VMEM((tm, tn), F32)] if nk > 1 else [],
        compiler_params=_cp(("arbitrary", "arbitrary", "arbitrary"), VMEM_BIG),
        name="matmul_resid",
    )(*a_list, w, x, gate)


def _norm_mod_kernel(x_ref, g_ref, sc_ref, sh_ref, o_ref):
    x = x_ref[...]
    y = x * lax.rsqrt(jnp.mean(x * x, axis=-1, keepdims=True) + EPS) * g_ref[...]
    o_ref[...] = (y * (1.0 + sc_ref[...]) + sh_ref[...]).astype(o_ref.dtype)


def norm_mod(x, g, sc, sh, *, tm):
    m, d = x.shape
    g_cnt, g_rows, _ = sc.shape
    rows_per_gate = m // g_cnt
    g_blk = 1 if g_rows == 1 else tm
    mod_spec = pl.BlockSpec((None, g_blk, d), lambda i: ((i * tm) // rows_per_gate, 0, 0))
    return pl.pallas_call(
        _norm_mod_kernel,
        grid=(m // tm,),
        in_specs=[pl.BlockSpec((tm, d), lambda i: (i, 0)),
                  pl.BlockSpec((1, d), lambda i: (0, 0)), mod_spec, mod_spec],
        out_specs=pl.BlockSpec((tm, d), lambda i: (i, 0)),
        out_shape=jax.ShapeDtypeStruct((m, d), BF16),
        compiler_params=_cp(("arbitrary",)),
        name="norm_mod",
    )(x, g, sc, sh)


def _rmsnorm_kernel(x_ref, g_ref, o_ref):
    x = x_ref[...]
    o_ref[...] = x * lax.rsqrt(jnp.mean(x * x, axis=-1, keepdims=True) + EPS) * g_ref[...]


def rmsnorm_rows(x, g, *, tm):
    m, d = x.shape
    return pl.pallas_call(
        _rmsnorm_kernel,
        grid=(m // tm,),
        in_specs=[pl.BlockSpec((tm, d), lambda i: (i, 0)), pl.BlockSpec((1, d), lambda i: (0, 0))],
        out_specs=pl.BlockSpec((tm, d), lambda i: (i, 0)),
        out_shape=jax.ShapeDtypeStruct((m, d), F32),
        compiler_params=_cp(("arbitrary",)),
        name="rmsnorm_rows",
    )(x, g)


def _compress_kernel(s_ref, w1_ref, pe_ref, w2_ref, o_ref):
    n_seg = s_ref.shape[0]
    half = s_ref.shape[1]
    s = s_ref[...].astype(BF16)
    w_lo = w1_ref[0].astype(BF16)
    w_hi = w1_ref[1].astype(BF16)
    lo = _dot(s, w_lo)
    hi = _dot(s, w_hi)
    pe = pe_ref[...].astype(BF16)
    bias = _dot(pe[:, :half], w_lo) + _dot(pe[:, half:], w_hi)
    pre = lo + pltpu.roll(hi, n_seg - 1, 0) + bias[0:1, :]
    out = _dot(_silu(pre).astype(BF16), w2_ref[...].astype(BF16))
    row = lax.broadcasted_iota(jnp.int32, out.shape, 0)
    o_ref[...] = jnp.where(row < n_seg - 1, out, 0.0)


def compress_segments(seg, w1, pe, w2):
    b, _, n_seg, half = seg.shape
    w1r = w1.reshape(2, 2, half, D_HEAD)
    pe8 = jnp.broadcast_to(pe.reshape(2, 1, 2 * half), (2, 8, 2 * half))
    return pl.pallas_call(
        _compress_kernel,
        grid=(b, 4),
        in_specs=[pl.BlockSpec((None, None, n_seg, half), lambda i, j: (i, j, 0, 0)),
                  pl.BlockSpec((None, 2, half, D_HEAD), lambda i, j: (j // 2, 0, 0, 0)),
                  pl.BlockSpec((None, 8, 2 * half), lambda i, j: (j // 2, 0, 0)),
                  pl.BlockSpec((None, D_HEAD, D_HEAD), lambda i, j: (j // 2, 0, 0))],
        out_specs=pl.BlockSpec((None, None, n_seg, D_HEAD), lambda i, j: (i, j, 0, 0)),
        out_shape=jax.ShapeDtypeStruct((b, 4, n_seg, D_HEAD), F32),
        compiler_params=_cp(("arbitrary", "arbitrary"), VMEM_BIG),
        name="compress_segments",
    )(seg, w1r, pe8, w2)


def _slope_column(g, rows_per_head):
    n = HPG * rows_per_head
    h = lax.broadcasted_iota(jnp.int32, (n, 1), 0) // rows_per_head + g * HPG
    out = jnp.zeros((n, 1), F32)
    for hh in range(H_NSA):
        out = jnp.where(h == hh, 2.0 ** (-8.0 * (hh + 1) / H_NSA), out)
    return out


def _masked_softmax_rows(s, valid):
    s = jnp.where(valid, s, NEG)
    m = jnp.max(s, axis=-1, keepdims=True)
    e = jnp.where(valid, jnp.exp(s - m), 0.0)
    d = jnp.sum(e, axis=-1, keepdims=True)
    return e / jnp.where(d > 0.0, d, 1.0)


def _select_blocks(score, n_blocks):
    lane = lax.broadcasted_iota(jnp.int32, score.shape, 1)
    rank = jnp.zeros(score.shape, jnp.int32)
    for i in range(n_blocks):
        c = score[:, i:i + 1]
        ahead = (c > score) | ((c == score) & (lane > i))
        rank = rank + ahead.astype(jnp.int32)
    return (rank < N_SELECT) & (score > -jnp.inf)


def _slc_matrix(n_cmp_rows, nc, n_lanes):
    i = np.arange(n_cmp_rows)[:, None]
    j = np.arange(n_lanes)[None, :]
    m = (i >= 4 * j - 1) & (i <= 4 * j + 3) & (i < nc)
    return jnp.asarray(m.astype(np.float32), dtype=BF16)


def _head_rms(o):
    return o * lax.rsqrt(jnp.mean(o * o, axis=-1, keepdims=True) + EPS)


def _nsa_prompt_kernel(q_ref, sm_ref, selk_ref, selv_ref, wink_ref, winv_ref, cmpk_ref, cmpv_ref,
                       kaug_ref, caug_ref, mslc_ref, wbias_ref, gm_ref, o_ref, *, qb, tk):
    g = pl.program_id(1)
    qi = pl.program_id(2)
    q0 = qi * qb
    rows = HPG * qb
    n_seg = cmpk_ref.shape[0]
    nc = n_seg - 1
    t_len = selk_ref.shape[0]
    ns = t_len // SEL_BLOCK

    q = q_ref[...] * SCALE
    qs = jnp.concatenate([q[:, h * D_HEAD:(h + 1) * D_HEAD] for h in range(HPG)], axis=0)
    qs_b = qs.astype(BF16)
    slope = _slope_column(g, qb)
    rpos = q0 + lax.broadcasted_iota(jnp.int32, (rows, 1), 0) % qb
    lane = lax.broadcasted_iota(jnp.int32, (rows, LANE), 1)
    ones_col = jnp.where(lax.broadcasted_iota(jnp.int32, (tk, LANE), 1) == 0, 1.0, 0.0).astype(BF16)

    qa_c = jnp.concatenate([qs_b, jnp.where(lane == 0, slope * CMP_STRIDE, 0.0).astype(BF16)], axis=1)
    ka_c = jnp.concatenate([cmpk_ref[...].astype(BF16), caug_ref[...]], axis=1)
    ci = lax.broadcasted_iota(jnp.int32, (1, n_seg), 1)
    c_valid = (ci * CMP_STRIDE + (CMP_BLOCK - 1) <= rpos) & (ci < nc)
    p_cmp = _masked_softmax_rows(_dot_nt(qa_c, ka_c), c_valid)
    o_cmp = _dot(p_cmp.astype(BF16), cmpv_ref[...].astype(BF16))

    imp = p_cmp[0:qb]
    for h in range(1, HPG):
        imp = imp + p_cmp[h * qb:(h + 1) * qb]
    p_slc = None
    for part in _split3(imp):
        d = _dot_nt(mslc_ref[...], part)
        p_slc = d if p_slc is None else p_slc + d
    p_slc = p_slc[0:SEL_BLOCK]
    blk = lax.broadcasted_iota(jnp.int32, (SEL_BLOCK, qb), 0)
    cur = (q0 + lax.broadcasted_iota(jnp.int32, (1, qb), 1)) // SEL_BLOCK
    forced = (blk == 0) | (blk > cur - N_LOCAL)
    score = jnp.where(forced, FORCE_SCORE, p_slc)
    score = jnp.where((blk <= cur) & (blk < ns), score, -jnp.inf)
    sub = 8
    pieces = [score[v * sub:(v + 1) * sub] for v in range(SEL_BLOCK // sub)]
    ranks = [jnp.zeros((sub, qb), jnp.int32) for _ in pieces]
    sub_row = lax.broadcasted_iota(jnp.int32, (sub, qb), 0)
    for i in range(ns):
        c = pieces[i // sub][i % sub:i % sub + 1, :]
        for v, piece in enumerate(pieces):
            if v < i // sub:
                ahead = (c > piece).astype(jnp.int32)
            elif v > i // sub:
                ahead = (c >= piece).astype(jnp.int32)
            else:
                ahead = jnp.where(sub_row > i % sub, (c >= piece).astype(jnp.int32), (c > piece).astype(jnp.int32))
            ranks[v] = ranks[v] + ahead
    rank = jnp.concatenate(ranks, axis=0)
    sel = (rank < N_SELECT) & (score > -jnp.inf)
    pen_t = jnp.where(sel, 0.0, PEN)
    pen = jnp.transpose(jnp.concatenate([pen_t, jnp.zeros((LANE - SEL_BLOCK, qb), F32)], axis=0))
    pen4 = jnp.concatenate([pen] * HPG, axis=0)
    q_aug = jnp.where(lane < SEL_BLOCK, pen4,
                      jnp.where(lane == SEL_BLOCK, slope * SEL_BLOCK,
                                jnp.where(lane == SEL_BLOCK + 1, slope, 0.0)))
    qa = jnp.concatenate([qs_b, q_aug.astype(BF16)], axis=1)

    def sel_scores(k0):
        ka = jnp.concatenate([selk_ref[pl.ds(k0, tk), :].astype(BF16), kaug_ref[pl.ds(k0, tk), :]], axis=1)
        return _dot_nt(qa, ka)

    def sel_update(st, k0, m_run, acc):
        va = jnp.concatenate([selv_ref[pl.ds(k0, tk), :].astype(BF16), ones_col], axis=1)
        m_new = jnp.maximum(m_run, jnp.max(st, axis=-1, keepdims=True))
        p = jnp.exp(st - m_new).astype(BF16)
        return m_new, jnp.exp(m_run - m_new) * acc + _dot(p, va)

    def sel_tile(kt, carry):
        m_run, acc, st = carry
        k0 = pl.multiple_of(kt * tk, tk)
        st_next = sel_scores(k0 + tk)
        m_new, acc = sel_update(st, k0, m_run, acc)
        return m_new, acc, st_next

    n_full = q0 // tk
    init = (jnp.full((rows, 1), NEG, F32), jnp.zeros((rows, 2 * LANE), F32), sel_scores(0))
    m_run, acc_sel, st = lax.fori_loop(0, n_full, sel_tile, init)
    k0 = pl.multiple_of(n_full * tk, tk)
    st = jnp.where(k0 + lax.broadcasted_iota(jnp.int32, (1, tk), 1) <= rpos, st, NEG)
    _, acc_sel = sel_update(st, k0, m_run, acc_sel)
    o_sel = acc_sel[:, 0:D_HEAD] / acc_sel[:, D_HEAD:D_HEAD + 1]

    wk = WINDOW + qb
    ks = pl.multiple_of(jnp.maximum(q0 - WINDOW, 0), qb)
    s = _dot_nt(qs_b, wink_ref[pl.ds(ks, wk), :].astype(BF16)) + wbias_ref[...]
    m = jnp.max(s, axis=-1, keepdims=True)
    p = jnp.exp(s - m).astype(BF16)
    ones_w = jnp.where(lax.broadcasted_iota(jnp.int32, (wk, LANE), 1) == 0, 1.0, 0.0).astype(BF16)
    acc_win = _dot(p, jnp.concatenate([winv_ref[pl.ds(ks, wk), :].astype(BF16), ones_w], axis=1))
    o_win = acc_win[:, 0:D_HEAD] / acc_win[:, D_HEAD:D_HEAD + 1]

    gates = 1.0 / (1.0 + jnp.exp(-sm_ref[...]))

    def gate_col(branch):
        cols = []
        for h in range(HPG):
            c0 = SM_GATE + h * 3 + branch
            c1 = SM_GATE + (HPG + h) * 3 + branch
            cols.append(jnp.where(g == 0, gates[:, c0:c0 + 1], gates[:, c1:c1 + 1]))
        return jnp.concatenate(cols, axis=0)

    o = gate_col(0) * o_cmp + gate_col(1) * o_sel + gate_col(2) * o_win
    o = _head_rms(o)
    for h in range(HPG):
        o_ref[:, h * D_HEAD:(h + 1) * D_HEAD] = (
            o[h * qb:(h + 1) * qb] * gm_ref[:, h * D_HEAD:(h + 1) * D_HEAD]).astype(o_ref.dtype)


def _key_aug(t_len):
    pos = np.arange(t_len)[:, None]
    lane = np.arange(LANE)[None, :]
    a = np.where(lane < SEL_BLOCK, (pos // SEL_BLOCK == lane).astype(np.float32),
                 np.where(lane == SEL_BLOCK, (pos // SEL_BLOCK).astype(np.float32),
                          np.where(lane == SEL_BLOCK + 1, (pos % SEL_BLOCK).astype(np.float32), 0.0)))
    return jnp.asarray(a, dtype=BF16)


def _cmp_aug(n_seg):
    a = np.zeros((n_seg, LANE), np.float32)
    a[:, 0] = np.arange(n_seg)
    return jnp.asarray(a, dtype=BF16)


def _window_bias(qb):
    n_pat = WINDOW // qb + 1
    r = np.arange(HPG * qb)
    head = r // qb
    dist = (np.arange(n_pat)[:, None, None] * qb + (r % qb)[None, :, None]
            - np.arange(WINDOW + qb)[None, None, :])
    valid = (dist >= 0) & (dist <= WINDOW)
    out = np.empty((G_NSA,) + dist.shape, np.float32)
    for g in range(G_NSA):
        slope = 2.0 ** (-8.0 * (g * HPG + head + 1) / H_NSA)
        out[g] = np.where(valid, -slope[None, :, None] * dist, NEG)
    return jnp.asarray(out)


def nsa_prompt(z, cmp_kv, g_mix, *, qb=256, tk=512):
    b, t_len, _ = z.shape
    assert t_len // SEL_BLOCK <= SEL_BLOCK and t_len % tk == 0 and tk % qb == 0 and WINDOW % qb == 0
    n_seg = cmp_kv.shape[2]
    assert n_seg <= 256
    gw = HPG * D_HEAD
    kv_blk = C_KV // D_HEAD
    n_pat = WINDOW // qb
    wbias = _window_bias(qb)

    def kv_spec(slot):
        return pl.BlockSpec((None, t_len, D_HEAD), lambda i, g, q: (i, 0, kv_blk + 2 * slot + g))

    return pl.pallas_call(
        functools.partial(_nsa_prompt_kernel, qb=qb, tk=tk),
        grid=(b, G_NSA, t_len // qb),
        in_specs=[pl.BlockSpec((None, qb, gw), lambda i, g, q: (i, q, g)),
                  pl.BlockSpec((None, qb, LANE), lambda i, g, q: (i, q, C_SM // LANE)),
                  kv_spec(2), kv_spec(3), kv_spec(4), kv_spec(5),
                  pl.BlockSpec((None, None, n_seg, D_HEAD), lambda i, g, q: (i, g, 0, 0)),
                  pl.BlockSpec((None, None, n_seg, D_HEAD), lambda i, g, q: (i, 2 + g, 0, 0)),
                  pl.BlockSpec((t_len, LANE), lambda i, g, q: (0, 0)),
                  pl.BlockSpec((n_seg, LANE), lambda i, g, q: (0, 0)),
                  pl.BlockSpec((LANE, n_seg), lambda i, g, q: (0, 0)),
                  pl.BlockSpec((None, None, HPG * qb, WINDOW + qb),
                               lambda i, g, q: (g, jnp.minimum(q, n_pat), 0, 0)),
                  pl.BlockSpec((1, gw), lambda i, g, q: (0, g))],
        out_specs=pl.BlockSpec((None, qb, gw), lambda i, g, q: (i, q, g)),
        out_shape=jax.ShapeDtypeStruct((b, t_len, H_NSA * D_HEAD), BF16),
        compiler_params=_cp(("arbitrary", "arbitrary", "arbitrary"), VMEM_BIG),
        name="nsa_prompt",
    )(z, z, z, z, z, z, cmp_kv, cmp_kv, _key_aug(t_len), _cmp_aug(n_seg),
      jnp.transpose(_slc_matrix(n_seg, n_seg - 1, LANE)), wbias, g_mix)


def _sb_prompt_kernel(q_ref, k_ref, v_ref, gm_ref, o_ref, *, qb, tk):
    qi = pl.program_id(2)
    q0 = qi * qb
    q = (q_ref[...] * (SCALE * LOG2E)).astype(BF16)
    rpos = q0 + lax.broadcasted_iota(jnp.int32, (qb, 1), 0)
    row = lax.broadcasted_iota(jnp.int32, (2 * tk, tk), 0) % tk
    tri2 = jnp.where(row > lax.broadcasted_iota(jnp.int32, (2 * tk, tk), 1), 1.0, 0.0).astype(BF16)
    n_diag = qb // tk

    def tile(k0, carry, masked):
        run, acc = carry
        z = _dot_nt(q, k_ref[pl.ds(k0, tk), :].astype(BF16))
        lb = _log2_sigmoid(z)
        lr = lb - z
        if masked:
            mask = (k0 + lax.broadcasted_iota(jnp.int32, (1, tk), 1)) < rpos
            lr = jnp.where(mask, lr, 0.0)
        hi, lo = _split2(lr)
        after = _dot(jnp.concatenate([hi, lo], axis=1), tri2) + run
        a = jnp.exp2(lb + after)
        if masked:
            a = jnp.where(mask, a, 0.0)
        acc = acc + _dot(a.astype(BF16), v_ref[pl.ds(k0, tk), :].astype(BF16))
        return after[:, 0:1] + lr[:, 0:1], acc

    carry = (jnp.zeros((qb, 1), F32), jnp.zeros((qb, D_HEAD), F32))
    for it in range(n_diag):
        carry = tile(pl.multiple_of(q0 + (n_diag - 1 - it) * tk, tk), carry, True)

    def full_tiles(it, carry):
        for u in range(n_diag):
            carry = tile(pl.multiple_of(q0 - (it * n_diag + u + 1) * tk, tk), carry, False)
        return carry

    _, acc = lax.fori_loop(0, qi, full_tiles, carry)
    o_ref[...] = (_head_rms(acc) * gm_ref[...]).astype(o_ref.dtype)


def sb_prompt(z, g_mix, *, qb=1024, tk=256):
    b, t_len, _ = z.shape
    assert qb % tk == 0 and t_len % qb == 0
    blk = C_SB // D_HEAD
    gm_blk = (H_NSA * D_HEAD) // D_HEAD
    return pl.pallas_call(
        functools.partial(_sb_prompt_kernel, qb=qb, tk=tk),
        grid=(b, H_SB, t_len // qb),
        in_specs=[pl.BlockSpec((None, qb, D_HEAD), lambda i, h, q: (i, q, blk + h)),
                  pl.BlockSpec((None, t_len, D_HEAD), lambda i, h, q: (i, 0, blk + H_SB + h)),
                  pl.BlockSpec((None, t_len, D_HEAD), lambda i, h, q: (i, 0, blk + 2 * H_SB + h)),
                  pl.BlockSpec((1, D_HEAD), lambda i, h, q: (0, gm_blk + h))],
        out_specs=pl.BlockSpec((None, qb, D_HEAD), lambda i, h, q: (i, q, h)),
        out_shape=jax.ShapeDtypeStruct((b, t_len, H_SB * D_HEAD), BF16),
        compiler_params=_cp(("arbitrary", "arbitrary", "arbitrary")),
        name="sb_prompt",
    )(z, z, z, g_mix)


N_LEVELS = 6


def _gla_level_tables():
    c = GLA_CHUNK
    idx = np.arange(c)
    sums = np.zeros(((N_LEVELS + 1) * c, c), np.float32)
    pair = np.zeros((N_LEVELS + 1, c, c), np.float32)
    sums[:c] = (idx[None, :] <= idx[:, None])
    pair[0] = np.eye(c)
    for lv in range(N_LEVELS):
        mid = ((idx >> (lv + 1)) << (lv + 1)) + (1 << lv)
        upper = idx >= mid
        j = idx[None, :]
        in_up = upper[:, None] & (j >= mid[:, None]) & (j <= idx[:, None])
        in_lo = (~upper)[:, None] & (j > idx[:, None]) & (j <= mid[:, None] - 1)
        sums[(lv + 1) * c:(lv + 2) * c] = in_up | in_lo
        same = (idx[:, None] >> (lv + 1)) == (idx[None, :] >> (lv + 1))
        pair[lv + 1] = same & upper[:, None] & (~upper)[None, :]
    return jnp.asarray(sums, dtype=BF16), jnp.asarray(pair, dtype=F32)


def _gla_prompt_kernel(qg_ref, kg_ref, vg_ref, og_ref, sm_ref, wa2_ref, ba_ref, sums_ref, pair_ref,
                       gm_ref, o_ref, st_ref, state_ref, *, tb):
    nt = pl.program_id(1)
    c = GLA_CHUNK

    @pl.when(nt == 0)
    def _():
        state_ref[...] = jnp.zeros_like(state_ref)

    wa2 = wa2_ref[...].astype(BF16)
    sums = sums_ref[...]

    def chunk(ci, _):
        r0 = pl.multiple_of(ci * c, c)
        rows = pl.ds(r0, c)
        x = _dot(sm_ref[rows, :].astype(BF16), wa2) + ba_ref[...]
        loga = _log_sigmoid(x) * (1.0 / GLA_TAU)
        hi, lo = _split2(loga)
        dsum = _dot(sums, hi) + _dot(sums, lo)
        cb = dsum[0:c]
        q = qg_ref[rows, :] * (GLA_DK ** -0.5)
        k = kg_ref[rows, :]
        v = vg_ref[rows, :].astype(BF16)
        qf = [q.astype(BF16)]
        kf = [k.astype(BF16)]
        for lv in range(N_LEVELS):
            e = jnp.exp(dsum[(lv + 1) * c:(lv + 2) * c])
            qf.append((q * e).astype(BF16))
            kf.append((k * e).astype(BF16))
        c_last = cb[c - 1:c, :]
        q_in = (q * jnp.exp(cb)).astype(BF16)
        k_out = (k * jnp.exp(c_last - cb)).astype(BF16)
        decay = jnp.exp(c_last)
        for h in range(H_GLA):
            ks = slice(h * GLA_DK, (h + 1) * GLA_DK)
            vs = slice(h * GLA_DV, (h + 1) * GLA_DV)
            att = None
            for lv in range(N_LEVELS + 1):
                term = pair_ref[lv] * _dot_nt(qf[lv][:, ks], kf[lv][:, ks])
                att = term if att is None else att + term
            s_t = state_ref[h]
            o_h = _dot(att.astype(BF16), v[:, vs]) + _dot_nt(q_in[:, ks], s_t.astype(BF16))
            state_ref[h] = s_t * decay[:, ks] + _dot_tn(v[:, vs], k_out[:, ks])
            og = og_ref[rows, vs]
            o_ref[rows, vs] = (_head_rms(o_h) * _silu(og) * gm_ref[:, vs]).astype(o_ref.dtype)
        return 0

    lax.fori_loop(0, tb // c, chunk, 0, unroll=4)

    @pl.when(nt == pl.num_programs(1) - 1)
    def _():
        st_ref[...] = state_ref[...]


def _wa2_padded(gla_wa2):
    w = jnp.zeros((LANE, H_GLA * GLA_DK), F32)
    return w.at[SM_ALR:SM_ALR + GLA_RANK].set(gla_wa2)


def gla_prompt(z, gla_wa2, gla_ba, g_mix, *, tb=512):
    b, t_len, _ = z.shape
    kw = H_GLA * GLA_DK
    vw = H_GLA * GLA_DV
    sums, pair = _gla_level_tables()
    n_rows = sums.shape[0]
    return pl.pallas_call(
        functools.partial(_gla_prompt_kernel, tb=tb),
        grid=(b, t_len // tb),
        in_specs=[pl.BlockSpec((None, tb, kw), lambda i, n: (i, n, C_QG // kw)),
                  pl.BlockSpec((None, tb, kw), lambda i, n: (i, n, C_KG // kw)),
                  pl.BlockSpec((None, tb, vw), lambda i, n: (i, n, C_VG // vw)),
                  pl.BlockSpec((None, tb, vw), lambda i, n: (i, n, C_OG // vw)),
                  pl.BlockSpec((None, tb, LANE), lambda i, n: (i, n, C_SM // LANE)),
                  pl.BlockSpec((LANE, kw), lambda i, n: (0, 0)),
                  pl.BlockSpec((1, kw), lambda i, n: (0, 0)),
                  pl.BlockSpec((n_rows, GLA_CHUNK), lambda i, n: (0, 0)),
                  pl.BlockSpec((N_LEVELS + 1, GLA_CHUNK, GLA_CHUNK), lambda i, n: (0, 0, 0)),
                  pl.BlockSpec((1, vw), lambda i, n: (0, (H_NSA + H_SB) * D_HEAD // vw))],
        out_specs=[pl.BlockSpec((None, tb, vw), lambda i, n: (i, n, 0)),
                   pl.BlockSpec((None, H_GLA, GLA_DV, GLA_DK), lambda i, n: (i, 0, 0, 0))],
        out_shape=[jax.ShapeDtypeStruct((b, t_len, vw), BF16),
                   jax.ShapeDtypeStruct((b, H_GLA, GLA_DV, GLA_DK), F32)],
        scratch_shapes=[pltpu.VMEM((H_GLA, GLA_DV, GLA_DK), F32)],
        compiler_params=_cp(("arbitrary", "arbitrary")),
        name="gla_prompt",
    )(z, z, z, z, z, _wa2_padded(gla_wa2), gla_ba.reshape(1, kw), sums, pair, g_mix)


def _gather_pages(page_refs, o_ref):
    slots = 4 * G_NSA
    seg_per_page = page_refs[0].shape[0] // (slots * CMP_STRIDE)
    for pair in range(len(page_refs) // 2):
        r0 = pair * 2 * seg_per_page
        for cg in range(4):
            for p in range(CMP_STRIDE):
                rows = [page_refs[2 * pair + u][pl.ds(p * slots + cg, seg_per_page, stride=CMP_STRIDE * slots), :]
                        for u in range(2)]
                o_ref[cg, r0:r0 + 2 * seg_per_page, p * D_HEAD:(p + 1) * D_HEAD] = (
                    jnp.concatenate(rows, axis=0).astype(o_ref.dtype))


def _rows_view(cache):
    depth, n_pool, page, a, b, d = cache.shape
    return cache.reshape(depth, n_pool, page * a * b, d)


def _ffn_down_gather_kernel(pt_ref, *refs, n_pg):
    page_refs = refs[:n_pg]
    a_ref, w_ref, x_ref, g_ref, o_ref, seg_ref = refs[n_pg:]
    o_ref[...] = x_ref[...] + g_ref[...] * _dot(a_ref[...], w_ref[...].astype(BF16))
    _gather_pages(page_refs, seg_ref)


def ffn_down_gather(a, w, layer, x, gate, cache_nsa, page_table, *, tm, tn):
    m, n = x.shape
    k = w.shape[1]
    page = cache_nsa.shape[2]
    b, n_pages = page_table.shape
    cache = _rows_view(cache_nsa)
    n_i, n_j = m // tm, n // tn
    assert n_i == b and n_pages % n_j == 0 and gate.shape[1] == 1
    n_pg = n_pages // n_j
    seg_per_page = page // CMP_STRIDE
    rows_per_gate = m // gate.shape[0]

    def page_spec(u):
        return pl.BlockSpec((None, None, cache.shape[2], D_HEAD),
                            lambda i, j, pt: (layer, pt[i * n_pages + j * n_pg + u], 0, 0))

    return pl.pallas_call(
        functools.partial(_ffn_down_gather_kernel, n_pg=n_pg),
        grid_spec=pltpu.PrefetchScalarGridSpec(
            num_scalar_prefetch=1,
            grid=(n_i, n_j),
            in_specs=[page_spec(u) for u in range(n_pg)] + [
                pl.BlockSpec((tm, k), lambda i, j, pt: (i, 0), pipeline_mode=pl.Buffered(1)),
                pl.BlockSpec((None, k, tn), lambda i, j, pt: (layer, 0, j)),
                pl.BlockSpec((tm, tn), lambda i, j, pt: (i, j)),
                pl.BlockSpec((None, 1, tn), lambda i, j, pt: ((i * tm) // rows_per_gate, 0, j))],
            out_specs=[pl.BlockSpec((tm, tn), lambda i, j, pt: (i, j)),
                       pl.BlockSpec((None, 4, n_pg * seg_per_page, CMP_STRIDE * D_HEAD),
                                    lambda i, j, pt: (i, 0, j, 0))],
        ),
        out_shape=[jax.ShapeDtypeStruct((m, n), F32),
                   jax.ShapeDtypeStruct((b, 4, n_pages * seg_per_page, CMP_STRIDE * D_HEAD), BF16)],
        compiler_params=_cp(("arbitrary", "arbitrary"), VMEM_BIG),
        name="ffn_down_gather",
    )(page_table.reshape(-1), *([cache] * n_pg), a, w, x, gate)


def _nsa_sample_select_kernel(q_ref, cmp_ref, mslc_ref, ocmp_ref, idx_ref, *, pos, n_blk_lanes):
    n_seg = cmp_ref.shape[1]
    nc = n_seg - 1
    ns = pos // SEL_BLOCK + 1
    cur = pos // SEL_BLOCK
    q = (q_ref[...] * SCALE).astype(BF16)
    row = lax.broadcasted_iota(jnp.int32, (H_NSA, 1), 0)
    slope = _slope_column(0, 1)
    slope = jnp.concatenate([slope, _slope_column(1, 1)], axis=0)
    s = jnp.where(row < HPG, _dot_nt(q, cmp_ref[0].astype(BF16)), _dot_nt(q, cmp_ref[1].astype(BF16)))
    ci = lax.broadcasted_iota(jnp.int32, (1, n_seg), 1)
    c_dist = pos - (ci * CMP_STRIDE + (CMP_BLOCK - 1))
    p = _masked_softmax_rows(s - slope * c_dist.astype(F32), (c_dist >= 0) & (ci < nc))
    pb = p.astype(BF16)
    ocmp_ref[...] = jnp.where(row < HPG, _dot(pb, cmp_ref[2].astype(BF16)), _dot(pb, cmp_ref[3].astype(BF16)))

    imp = jnp.concatenate([jnp.sum(p[g * HPG:(g + 1) * HPG], axis=0, keepdims=True) for g in range(G_NSA)]
                          + [jnp.zeros((H_NSA - G_NSA, n_seg), F32)], axis=0)
    p_slc = None
    for part in _split3(imp):
        d = _dot(part, mslc_ref[...])
        p_slc = d if p_slc is None else p_slc + d
    blk = lax.broadcasted_iota(jnp.int32, p_slc.shape, 1)
    forced = (blk == 0) | (blk > cur - N_LOCAL)
    score = jnp.where(forced, FORCE_SCORE, p_slc)
    score = jnp.where(blk <= cur, score, -jnp.inf)
    sel = _select_blocks(score, ns)
    upper = (lax.broadcasted_iota(jnp.int32, (n_blk_lanes, n_blk_lanes), 0)
             < lax.broadcasted_iota(jnp.int32, (n_blk_lanes, n_blk_lanes), 1))
    before = _dot(jnp.where(sel, 1.0, 0.0).astype(BF16), jnp.where(upper, 1.0, 0.0).astype(BF16))
    blk_f = blk.astype(F32)
    out_lane = lax.broadcasted_iota(jnp.int32, (H_NSA, LANE), 1)
    out = jnp.zeros((H_NSA, LANE), F32)
    for n in range(N_SELECT):
        v = jnp.sum(jnp.where(sel & (before == float(n)), blk_f, 0.0), axis=-1, keepdims=True)
        out = jnp.where(out_lane == n, v, out)
    idx_ref[...] = out.astype(jnp.int32)


def nsa_sample_select(q8, cmp_kv, *, pos):
    b = q8.shape[0]
    n_seg = cmp_kv.shape[2]
    ns = pos // SEL_BLOCK + 1
    assert ns >= N_SELECT
    n_blk_lanes = -(-ns // LANE) * LANE
    return pl.pallas_call(
        functools.partial(_nsa_sample_select_kernel, pos=pos, n_blk_lanes=n_blk_lanes),
        grid=(b,),
        in_specs=[pl.BlockSpec((None, H_NSA, D_HEAD), lambda i: (i, 0, 0)),
                  pl.BlockSpec((None, 4, n_seg, D_HEAD), lambda i: (i, 0, 0, 0)),
                  pl.BlockSpec((n_seg, n_blk_lanes), lambda i: (0, 0))],
        out_specs=[pl.BlockSpec((None, H_NSA, D_HEAD), lambda i: (i, 0, 0)),
                   pl.BlockSpec((None, H_NSA, LANE), lambda i: (i, 0, 0))],
        out_shape=[jax.ShapeDtypeStruct((b, H_NSA, D_HEAD), F32),
                   jax.ShapeDtypeStruct((b, H_NSA, LANE), jnp.int32)],
        compiler_params=_cp(("arbitrary",)),
        name="nsa_sample_select",
    )(q8, cmp_kv, _slc_matrix(n_seg, n_seg - 1, n_blk_lanes))


def _softmax_with_new(s, valid, s_new, v_mat, v_new):
    s = jnp.where(valid, s, NEG)
    m = jnp.maximum(jnp.max(s, axis=-1, keepdims=True), s_new)
    e = jnp.where(valid, jnp.exp(s - m), 0.0)
    e_new = jnp.exp(s_new - m)
    d = jnp.sum(e, axis=-1, keepdims=True) + e_new
    return (_dot(e.astype(BF16), v_mat) + e_new.astype(BF16).astype(F32) * v_new) / d


def _nsa_sample_attend_kernel(idx_ref, pt_ref, *refs, pos):
    sel_refs = refs[:N_SELECT]
    (q_ref, nk_ref, nv_ref, nwk_ref, nwv_ref, sm_ref, win_ref, ocmp_ref, gm_ref, o_ref) = refs[N_SELECT:]
    b = pl.program_id(0)
    g = pl.program_id(1)
    n_past_blocks = pos // SEL_BLOCK
    q = (q_ref[...] * SCALE).astype(BF16)
    qf = q.astype(F32)
    slope = jnp.concatenate([_slope_column(0, 1), _slope_column(1, 1)], axis=0)

    def new_row(ref):
        return ref[pl.ds(b, 1), :].astype(BF16)

    def slot_rows(ref, slot, n_tok, n_slots):
        both = [ref[pl.ds(slot * G_NSA + gg, n_tok, stride=n_slots), :] for gg in range(G_NSA)]
        return jnp.where(g == 0, both[0], both[1]).astype(BF16)

    k_all = jnp.concatenate([slot_rows(r, 2, SEL_BLOCK, 4 * G_NSA) for r in sel_refs], axis=0)
    v_all = jnp.concatenate([slot_rows(r, 3, SEL_BLOCK, 4 * G_NSA) for r in sel_refs], axis=0)
    width = N_SELECT * SEL_BLOCK
    lane = lax.broadcasted_iota(jnp.int32, (1, width), 1)
    tok = jnp.zeros((1, width), jnp.int32)
    in_cache = jnp.zeros((1, width), jnp.bool_)
    for n in range(N_SELECT):
        blk_id = idx_ref[(b * G_NSA + g) * N_SELECT + n]
        here = (lane // SEL_BLOCK) == n
        tok = jnp.where(here, blk_id * SEL_BLOCK + lane % SEL_BLOCK, tok)
        in_cache = in_cache | (here & (blk_id < n_past_blocks))
    dist = pos - tok
    s = _dot_nt(q, k_all) - slope * dist.astype(F32)
    k_new = new_row(nk_ref)
    s_new = jnp.sum(qf * k_new.astype(F32), axis=-1, keepdims=True)
    o_sel = _softmax_with_new(s, in_cache & (dist >= 0), s_new, v_all, new_row(nv_ref).astype(F32))

    n_win = win_ref.shape[0] // (2 * G_NSA)
    wi = lax.broadcasted_iota(jnp.int32, (1, n_win), 1)
    w_dist = n_win - wi
    s = _dot_nt(q, slot_rows(win_ref, 0, n_win, 2 * G_NSA)) - slope * w_dist.astype(F32)
    s_new = jnp.sum(qf * new_row(nwk_ref).astype(F32), axis=-1, keepdims=True)
    o_win = _softmax_with_new(s, w_dist <= WINDOW, s_new, slot_rows(win_ref, 1, n_win, 2 * G_NSA),
                              new_row(nwv_ref).astype(F32))

    gates = 1.0 / (1.0 + jnp.exp(-sm_ref[pl.ds(b, 1), :]))
    hrow = lax.broadcasted_iota(jnp.int32, (H_NSA, LANE), 0)
    glane = lax.broadcasted_iota(jnp.int32, (H_NSA, LANE), 1)

    def gate_col(branch):
        return jnp.sum(jnp.where(glane == SM_GATE + 3 * hrow + branch, gates, 0.0), axis=-1, keepdims=True)

    o = gate_col(0) * ocmp_ref[...] + gate_col(1) * o_sel + gate_col(2) * o_win
    o = _head_rms(o) * gm_ref[...]
    o_ref[...] = jnp.where(g == 0, o[0:HPG], o[HPG:H_NSA])


def nsa_sample_attend(zs, q8, o_cmp, idx, cache_nsa, state_win, page_table, gm8, layer, *, pos):
    page = cache_nsa.shape[2]
    b, n_pages = page_table.shape
    halves = page // SEL_BLOCK
    cache = _rows_view(cache_nsa)
    blk_rows = cache.shape[2] // halves
    win = _rows_view(state_win)
    last_blk = pos // SEL_BLOCK - 1
    kv_blk = C_KV // D_HEAD

    def sel_spec(n):
        def index_map(i, g, idx_ref, pt_ref):
            blk_id = jnp.minimum(idx_ref[(i * G_NSA + g) * N_SELECT + n], last_blk)
            return (layer, pt_ref[i * n_pages + blk_id // halves], blk_id % halves, 0)
        return pl.BlockSpec((None, None, blk_rows, D_HEAD), index_map)

    def zs_spec(slot):
        return pl.BlockSpec((MS, D_HEAD), lambda i, g, a, c: (0, kv_blk + 2 * slot + g))

    in_specs = ([sel_spec(n) for n in range(N_SELECT)]
                + [pl.BlockSpec((None, H_NSA, D_HEAD), lambda i, g, a, c: (i, 0, 0)),
                   zs_spec(2), zs_spec(3), zs_spec(4), zs_spec(5),
                   pl.BlockSpec((MS, LANE), lambda i, g, a, c: (0, C_SM // LANE)),
                   pl.BlockSpec((None, None, win.shape[2], D_HEAD), lambda i, g, a, c: (layer, i, 0, 0)),
                   pl.BlockSpec((None, H_NSA, D_HEAD), lambda i, g, a, c: (i, 0, 0)),
                   pl.BlockSpec((H_NSA, D_HEAD), lambda i, g, a, c: (0, 0))])
    return pl.pallas_call(
        functools.partial(_nsa_sample_attend_kernel, pos=pos),
        grid_spec=pltpu.PrefetchScalarGridSpec(
            num_scalar_prefetch=2,
            grid=(b, G_NSA),
            in_specs=in_specs,
            out_specs=pl.BlockSpec((None, None, HPG, D_HEAD), lambda i, g, a, c: (i, g, 0, 0)),
        ),
        out_shape=jax.ShapeDtypeStruct((b, G_NSA, HPG, D_HEAD), F32),
        compiler_params=_cp(("arbitrary", "arbitrary")),
        name="nsa_sample_attend",
    )(idx, page_table.reshape(-1), *([cache] * N_SELECT), q8, zs, zs, zs, zs, zs, win, o_cmp, gm8)


SB_PAGES_PER_STEP = 16


def _sb_sample_kernel(pt_ref, *refs):
    page_refs = refs[:SB_PAGES_PER_STEP]
    q_ref, gm_ref, o_ref, run_ref, acc_ref = refs[SB_PAGES_PER_STEP:]
    b = pl.program_id(0)
    s_idx = pl.program_id(1)
    slots = 2 * H_SB
    page = page_refs[0].shape[0] // slots
    kw = H_SB * D_HEAD
    n_u = SB_PAGES_PER_STEP

    @pl.when(s_idx == 0)
    def _():
        run_ref[...] = jnp.zeros_like(run_ref)
        acc_ref[...] = jnp.zeros_like(acc_ref)

    def heads_on_lanes(ref, first_slot):
        return jnp.concatenate([ref[pl.ds(first_slot + h, page, stride=slots), :].astype(BF16)
                                for h in range(H_SB)], axis=1)

    qrow = q_ref[pl.ds(b, 1), :] * SCALE
    hrow = lax.broadcasted_iota(jnp.int32, (8, kw), 0)
    hlane = lax.broadcasted_iota(jnp.int32, (8, kw), 1) // D_HEAD
    qm = jnp.where(hrow == hlane, qrow, 0.0).astype(BF16)
    tri = (lax.broadcasted_iota(jnp.int32, (page, page), 0) > lax.broadcasted_iota(jnp.int32, (page, page), 1))
    tri = jnp.where(tri, 1.0, 0.0).astype(BF16)

    k_all = jnp.concatenate([heads_on_lanes(r, 0) for r in page_refs], axis=0)
    z = _dot_nt(qm, k_all)
    lb = _log_sigmoid(z)
    lr = lb - z
    lr_rows = jnp.concatenate([lr[:, u * page:(u + 1) * page] for u in range(n_u)], axis=0)
    hi, lo = _split2(lr_rows)
    local = _dot(hi, tri) + _dot(lo, tri)
    total = local[:, 0:1] + lr_rows[:, 0:1]
    run = run_ref[:, 0:1]
    offs = [None] * n_u
    for u in range(n_u - 1, -1, -1):
        offs[u] = run
        run = run + total[8 * u:8 * u + 8]
    run_ref[...] = jnp.broadcast_to(run, run_ref.shape)
    after = jnp.concatenate([local[8 * u:8 * u + 8] + offs[u] for u in range(n_u)], axis=1)
    a = jnp.exp(lb + after)
    v_all = jnp.concatenate([heads_on_lanes(r, H_SB) for r in page_refs], axis=0)
    acc = acc_ref[...] + _dot(a.astype(BF16), v_all)
    acc_ref[...] = acc

    @pl.when(s_idx == pl.num_programs(1) - 1)
    def _():
        o = jnp.concatenate([acc[h:h + 1, h * D_HEAD:(h + 1) * D_HEAD] for h in range(H_SB)], axis=0)
        o_ref[...] = _head_rms(o) * gm_ref[...]


def sb_sample(zs, cache_sb, page_table, gm4, layer):
    b, n_pages = page_table.shape
    cache = _rows_view(cache_sb)
    steps = n_pages // SB_PAGES_PER_STEP

    def page_spec(u):
        return pl.BlockSpec(
            (None, None, cache.shape[2], D_HEAD),
            lambda i, s, pt: (layer, pt[i * n_pages + (steps - 1 - s) * SB_PAGES_PER_STEP + u], 0, 0))

    return pl.pallas_call(
        _sb_sample_kernel,
        grid_spec=pltpu.PrefetchScalarGridSpec(
            num_scalar_prefetch=1,
            grid=(b, steps),
            in_specs=[page_spec(u) for u in range(SB_PAGES_PER_STEP)]
            + [pl.BlockSpec((MS, H_SB * D_HEAD), lambda i, s, pt: (0, C_SB // (H_SB * D_HEAD))),
               pl.BlockSpec((H_SB, D_HEAD), lambda i, s, pt: (0, 0))],
            out_specs=pl.BlockSpec((None, H_SB, D_HEAD), lambda i, s, pt: (i, 0, 0)),
            scratch_shapes=[pltpu.VMEM((8, LANE), F32), pltpu.VMEM((8, H_SB * D_HEAD), F32)],
        ),
        out_shape=jax.ShapeDtypeStruct((b, H_SB, D_HEAD), F32),
        compiler_params=_cp(("arbitrary", "arbitrary"), VMEM_BIG),
        name="sb_sample",
    )(page_table.reshape(-1), *([cache] * SB_PAGES_PER_STEP), zs, gm4)


def _gla_sample_kernel(qg_ref, kg_ref, vg_ref, og_ref, sm_ref, wa2_ref, ba_ref, st_ref, gm_ref,
                       o_ref, ns_ref, *, n_b):
    x = _dot(sm_ref[...].astype(BF16), wa2_ref[...].astype(BF16)) + ba_ref[...]
    decay = jnp.exp(_log_sigmoid(x) * (1.0 / GLA_TAU))
    eye = (lax.broadcasted_iota(jnp.int32, (GLA_DK, GLA_DK), 0)
           == lax.broadcasted_iota(jnp.int32, (GLA_DK, GLA_DK), 1))

    def column(row):
        return jnp.sum(jnp.where(eye, row, 0.0), axis=1, keepdims=True)

    o_ref[...] = jnp.zeros_like(o_ref)
    for b in range(n_b):
        for h in range(H_GLA):
            ks = slice(h * GLA_DK, (h + 1) * GLA_DK)
            vs = slice(h * GLA_DV, (h + 1) * GLA_DV)
            s_new = (column(decay[b:b + 1, ks]) * st_ref[b, h]
                     + column(kg_ref[b:b + 1, ks]) * vg_ref[b:b + 1, vs])
            ns_ref[b, h] = s_new
            q_col = column(qg_ref[b:b + 1, ks] * (GLA_DK ** -0.5))
            o = jnp.sum(q_col * s_new, axis=0, keepdims=True)
            o_ref[b:b + 1, vs] = _head_rms(o) * _silu(og_ref[b:b + 1, vs]) * gm_ref[:, vs]


def gla_sample(zs, gla_wa2, gla_ba, state, g_mix):
    n_b = state.shape[0]
    kw = H_GLA * GLA_DK
    vw = H_GLA * GLA_DV
    return pl.pallas_call(
        functools.partial(_gla_sample_kernel, n_b=n_b),
        grid=(1,),
        in_specs=[pl.BlockSpec((MS, kw), lambda i: (0, C_QG // kw)),
                  pl.BlockSpec((MS, kw), lambda i: (0, C_KG // kw)),
                  pl.BlockSpec((MS, vw), lambda i: (0, C_VG // vw)),
                  pl.BlockSpec((MS, vw), lambda i: (0, C_OG // vw)),
                  pl.BlockSpec((MS, LANE), lambda i: (0, C_SM // LANE)),
                  pl.BlockSpec((LANE, kw), lambda i: (0, 0)),
                  pl.BlockSpec((1, kw), lambda i: (0, 0)),
                  pl.BlockSpec(state.shape, lambda i: (0, 0, 0, 0)),
                  pl.BlockSpec((1, vw), lambda i: (0, (H_NSA + H_SB) * D_HEAD // vw))],
        out_specs=[pl.BlockSpec((MS, vw), lambda i: (0, 0)),
                   pl.BlockSpec(state.shape, lambda i: (0, 0, 0, 0))],
        out_shape=[jax.ShapeDtypeStruct((MS, vw), F32), jax.ShapeDtypeStruct(state.shape, F32)],
        compiler_params=_cp(("arbitrary",)),
        name="gla_sample",
    )(zs, zs, zs, zs, zs, _wa2_padded(gla_wa2), gla_ba.reshape(1, kw), state, g_mix)


def _reorder_w_in(w_in):
    o = np.cumsum([0, 1024, 24, 1536, 1536, 256, 256, 512, 16, 512])
    seg = [w_in[..., o[i]:o[i + 1]] for i in range(9)]
    q_n, gate, kv, sb, qg, kg, vg, alr, og = seg
    pad = jnp.zeros(w_in.shape[:-1] + (N_Z - C_SM - 40,), w_in.dtype)
    return jnp.concatenate([q_n, kv, sb, qg, kg, vg, og, gate, alr, pad], axis=-1).astype(BF16)


def kernel(x_prompt, x_sample, cache_nsa, cache_sb, state_win, state_gla, page_table, c_prompt, c_sample, norm1, norm2, w_ada, b_ada, w_in, gla_wa2, gla_ba, cmp_pe, cmp_w1, cmp_w2, g_mix, w_out, ffn_w1, ffn_w3, ffn_w2, final_norm):
    depth = w_in.shape[0]
    bp, t_len, d = x_prompt.shape
    bs = x_sample.shape[0]
    n_pages = page_table.shape[1]
    page = cache_nsa.shape[2]
    past = n_pages * page
    wbuf = state_win.shape[2]
    mp = bp * t_len
    nsa_w = 4 * G_NSA * D_HEAD
    sb_w = 2 * H_SB * D_HEAD
    win_w = 2 * G_NSA * D_HEAD
    assert bs <= MS and x_sample.shape[1] == 1 and wbuf == WINDOW and t_len >= wbuf

    c_all = jnp.zeros((MS, d), F32).at[:bs].set(c_sample).at[bs:bs + bp].set(c_prompt)
    mod = matmul_bias(jnp.broadcast_to(c_all, (depth, MS, d)), w_ada, b_ada[:, None, :],
                      tm=MS, tn=1024, silu_a=True)
    w_in_r = _reorder_w_in(w_in)

    xp = x_prompt.reshape(mp, d)
    xs = jnp.zeros((MS, d), F32).at[:bs].set(x_sample[:, 0])
    outs = {k: [] for k in ("nsa_s", "sb_s", "win_p", "win_s", "gla_p", "gla_s")}
    rows_p = None

    for l in range(depth):
        def mod_p(k):
            return mod[l, bs:bs + bp, None, k * d:(k + 1) * d]

        def mod_s(k):
            return mod[l, None, :, k * d:(k + 1) * d]

        gm = g_mix[l][None, :]
        h = norm_mod(xp, norm1[l][None, :], mod_p(1), mod_p(0), tm=512)
        z, rows_nsa, rows_sb = project_in(h, w_in_r, l, rows_p, tm=1024, emit_rows=True)
        rows_p = (rows_nsa, rows_sb)
        z3 = z.reshape(bp, t_len, N_Z)
        seg = z3[..., C_KV:C_KV + 4 * D_HEAD].reshape(bp, t_len // CMP_STRIDE, CMP_STRIDE, 4, D_HEAD)
        seg = jnp.transpose(seg, (0, 3, 1, 2, 4)).reshape(bp, 4, t_len // CMP_STRIDE, CMP_STRIDE * D_HEAD)
        cmp_kv = compress_segments(seg, cmp_w1[l], cmp_pe[l], cmp_w2[l])
        o_nsa = nsa_prompt(z3, cmp_kv, gm)
        o_sb = sb_prompt(z3, gm)
        o_gla, st_t = gla_prompt(z3, gla_wa2[l], gla_ba[l], gm)
        xp = matmul_resid([o_nsa.reshape(mp, -1), o_sb.reshape(mp, -1), o_gla.reshape(mp, -1)],
                          w_out, l, xp, mod_p(2), tm=2048, tn=512)
        h = norm_mod(xp, norm2[l][None, :], mod_p(4), mod_p(3), tm=512)
        hid = matmul_swiglu(h, ffn_w1, ffn_w3, l, tm=2048, tn=512)
        xp, seg_s = ffn_down_gather(hid, ffn_w2, l, xp, mod_p(5), cache_nsa, page_table,
                                    tm=mp // bs, tn=256)
        outs["win_p"].append(z3[:, t_len - wbuf:, C_KV + nsa_w:C_KV + nsa_w + win_w]
                             .reshape(bp, wbuf, 2, G_NSA, D_HEAD))
        outs["gla_p"].append(jnp.swapaxes(st_t, 2, 3))

        hs = norm_mod(xs, norm1[l][None, :], mod_s(1), mod_s(0), tm=MS)
        zs = project_in(hs, w_in_r, l, tm=MS, emit_rows=False)
        cmp_s = compress_segments(seg_s, cmp_w1[l], cmp_pe[l], cmp_w2[l])
        q8 = zs[:bs, C_Q:C_Q + H_NSA * D_HEAD].reshape(bs, H_NSA, D_HEAD)
        o_cmp, idx = nsa_sample_select(q8, cmp_s, pos=past)
        idx_flat = idx[:, :G_NSA, :N_SELECT].reshape(-1)
        o_nsa_s = nsa_sample_attend(zs, q8, o_cmp, idx_flat, cache_nsa, state_win, page_table,
                                    gm[0, :H_NSA * D_HEAD].reshape(H_NSA, D_HEAD), l, pos=past)
        o_sb_s = sb_sample(zs, cache_sb, page_table,
                           gm[0, H_NSA * D_HEAD:(H_NSA + H_SB) * D_HEAD].reshape(H_SB, D_HEAD), l)
        o_gla_s, st_new = gla_sample(zs, gla_wa2[l], gla_ba[l], state_gla[l], gm)

        def pad_rows(a):
            return jnp.zeros((MS, a.shape[1]), BF16).at[:bs].set(a.astype(BF16))

        xs = matmul_resid([pad_rows(o_nsa_s.reshape(bs, -1)), pad_rows(o_sb_s.reshape(bs, -1)),
                           o_gla_s.astype(BF16)], w_out, l, xs, mod_s(2), tm=MS, tn=512)
        hs = norm_mod(xs, norm2[l][None, :], mod_s(4), mod_s(3), tm=MS)
        hid_s = matmul_swiglu(hs, ffn_w1, ffn_w3, l, tm=MS, tn=512)
        xs = matmul_resid([hid_s], ffn_w2, l, xs, mod_s(5), tm=MS, tn=512, tk=2816)
        outs["nsa_s"].append(zs[:bs, C_KV:C_KV + nsa_w].reshape(bs, 1, 4, G_NSA, D_HEAD))
        outs["sb_s"].append(zs[:bs, C_SB + H_SB * D_HEAD:C_SB + H_SB * D_HEAD + sb_w].reshape(bs, 1, 2, H_SB, D_HEAD))
        win_new = zs[:bs, C_KV + nsa_w:C_KV + nsa_w + win_w].reshape(bs, 1, 2, G_NSA, D_HEAD)
        outs["win_s"].append(jnp.concatenate([state_win[l][:, 1:], win_new], axis=1))
        outs["gla_s"].append(st_new.astype(state_gla.dtype))

    y_prompt = rmsnorm_rows(xp, final_norm[None, :], tm=512).reshape(bp, t_len, d)
    y_sample = rmsnorm_rows(xs, final_norm[None, :], tm=MS)[:bs].reshape(bs, 1, d)
    nsa_p = rows_p[0].reshape(depth, bp, t_len, 4, G_NSA, D_HEAD)
    sb_p = rows_p[1].reshape(depth, bp, t_len, 2, H_SB, D_HEAD)
    return (y_prompt, y_sample, nsa_p, jnp.stack(outs["nsa_s"]), sb_p,
            jnp.stack(outs["sb_s"]), jnp.stack(outs["win_p"]), jnp.stack(outs["win_s"]),
            jnp.stack(outs["gla_p"]), jnp.stack(outs["gla_s"]))
```

```python
import functools

import numpy as np
import jax
import jax.numpy as jnp
from jax import lax
from jax.experimental import pallas as pl
from jax.experimental.pallas import tpu as pltpu

F32 = jnp.float32
BF16 = jnp.bfloat16

EPS = 1e-6
D_HEAD = 128
H_NSA = 8
G_NSA = 2
HPG = 4
H_SB = 4
H_GLA = 4
GLA_DK = 64
GLA_DV = 128
GLA_RANK = 16
GLA_TAU = 16.0
GLA_CHUNK = 128
CMP_BLOCK = 32
CMP_STRIDE = 16
SEL_BLOCK = 64
N_SELECT = 16
N_LOCAL = 2
WINDOW = 512
FORCE_SCORE = 1e6
SCALE = D_HEAD ** -0.5
LANE = 128
MS = 16

C_Q = 0
C_KV = 1024
C_SB = 2560
C_QG = 4096
C_KG = 4352
C_VG = 4608
C_OG = 5120
C_SM = 5632
N_Z = 6144
SM_GATE = 0
SM_ALR = 24

NEG = -1e30
PEN = -(2.0 ** 100)
PEN_TEST = -(2.0 ** 90)

VMEM_BIG = 56 * 1024 * 1024


def _cp(sem, vmem=None):
    return pltpu.CompilerParams(dimension_semantics=sem, vmem_limit_bytes=vmem)


def _log_sigmoid(x):
    return jnp.minimum(x, 0.0) - jnp.log(1.0 + jnp.exp(-jnp.abs(x)))


LOG2E = 1.4426950408889634


def _log2_sigmoid(x2):
    return jnp.minimum(x2, 0.0) - jnp.log2(1.0 + jnp.exp2(-jnp.abs(x2)))


def _silu(x):
    return x * (1.0 / (1.0 + jnp.exp(-x)))


def _split2(x):
    hi = x.astype(BF16)
    lo = (x - hi.astype(F32)).astype(BF16)
    return hi, lo


def _split3(x):
    hi = x.astype(BF16)
    r = x - hi.astype(F32)
    mid = r.astype(BF16)
    lo = (r - mid.astype(F32)).astype(BF16)
    return hi, mid, lo


def _dot(a, b):
    return jnp.dot(a, b, preferred_element_type=F32)


def _dot_nt(a, b):
    return lax.dot_general(a, b, (((1,), (1,)), ((), ())), preferred_element_type=F32)


def _dot_tn(a, b):
    return lax.dot_general(a, b, (((0,), (0,)), ((), ())), preferred_element_type=F32)


def _mm_plain_kernel(a_ref, w_ref, b_ref, o_ref, *, silu_a):
    a = a_ref[...]
    if silu_a:
        a = _silu(a.astype(F32))
    o_ref[...] = _dot(a.astype(BF16), w_ref[...].astype(BF16)) + b_ref[...]


def matmul_bias(a, w, b, *, tm, tn, silu_a=False):
    nl, m, k = a.shape
    n = w.shape[2]
    return pl.pallas_call(
        functools.partial(_mm_plain_kernel, silu_a=silu_a),
        grid=(nl, m // tm, n // tn),
        in_specs=[pl.BlockSpec((None, tm, k), lambda l, i, j: (l, i, 0)),
                  pl.BlockSpec((None, k, tn), lambda l, i, j: (l, 0, j)),
                  pl.BlockSpec((None, 1, tn), lambda l, i, j: (l, 0, j))],
        out_specs=pl.BlockSpec((None, tm, tn), lambda l, i, j: (l, i, j)),
        out_shape=jax.ShapeDtypeStruct((nl, m, n), F32),
        compiler_params=_cp(("arbitrary", "arbitrary", "arbitrary"), VMEM_BIG),
        name="matmul_bias",
    )(a, w, b)


PROJ_TN = 1024
ROW_SLOTS = 8


def _project_in_kernel(*refs, emit_rows, tm):
    a_ref, w_ref = refs[:2]
    acc = _dot(a_ref[...], w_ref[...])
    if not emit_rows:
        refs[2][...] = acc
        return
    z_ref, nsa_ref, sb_ref = refs[-3:]
    z_ref[...] = acc
    j = pl.program_id(1)

    def write_rows(rows_ref):
        for s in range(ROW_SLOTS):
            rows_ref[pl.ds(s, tm, stride=ROW_SLOTS), :] = acc[:, s * D_HEAD:(s + 1) * D_HEAD]

    @pl.when(j == C_KV // PROJ_TN)
    def _():
        write_rows(nsa_ref)

    @pl.when(j == (C_SB + H_SB * D_HEAD) // PROJ_TN)
    def _():
        write_rows(sb_ref)


def project_in(a, w_all, layer, rows_prev=None, *, tm, emit_rows):
    m, k = a.shape
    depth, _, n = w_all.shape
    in_specs = [pl.BlockSpec((tm, k), lambda i, j: (i, 0)),
                pl.BlockSpec((None, k, PROJ_TN), lambda i, j: (layer, 0, j))]
    z_spec = pl.BlockSpec((tm, PROJ_TN), lambda i, j: (i, j))
    z_shape = jax.ShapeDtypeStruct((m, n), F32)
    if not emit_rows:
        return pl.pallas_call(
            functools.partial(_project_in_kernel, emit_rows=False, tm=tm),
            grid=(m // tm, n // PROJ_TN), in_specs=in_specs, out_specs=z_spec, out_shape=z_shape,
            compiler_params=_cp(("arbitrary", "arbitrary"), VMEM_BIG), name="project_in_small",
        )(a, w_all)
    rows_shape = jax.ShapeDtypeStruct((depth, m * ROW_SLOTS, D_HEAD), F32)
    rows_spec = pl.BlockSpec((None, tm * ROW_SLOTS, D_HEAD), lambda i, j: (layer, i, 0))
    return pl.pallas_call(
        functools.partial(_project_in_kernel, emit_rows=True, tm=tm),
        grid=(m // tm, n // PROJ_TN), in_specs=in_specs + [pl.BlockSpec(memory_space=pl.ANY)] * 2,
        out_specs=[z_spec, rows_spec, rows_spec], out_shape=[z_shape, rows_shape, rows_shape],
        input_output_aliases={2: 1, 3: 2},
        compiler_params=_cp(("arbitrary", "arbitrary"), VMEM_BIG), name="project_in",
    )(a, w_all, *rows_prev)


def _mm_swiglu_kernel(a_ref, w1_ref, w3_ref, o_ref):
    a = a_ref[...]
    u = _dot(a, w1_ref[...].astype(BF16))
    v = _dot(a, w3_ref[...].astype(BF16))
    o_ref[...] = (_silu(u) * v).astype(o_ref.dtype)


def matmul_swiglu(a, w1, w3, layer, *, tm, tn):
    m, k = a.shape
    n = w1.shape[2]
    return pl.pallas_call(
        _mm_swiglu_kernel,
        grid=(m // tm, n // tn),
        in_specs=[pl.BlockSpec((tm, k), lambda i, j: (i, 0)),
                  pl.BlockSpec((None, k, tn), lambda i, j: (layer, 0, j)),
                  pl.BlockSpec((None, k, tn), lambda i, j: (layer, 0, j))],
        out_specs=pl.BlockSpec((tm, tn), lambda i, j: (i, j)),
        out_shape=jax.ShapeDtypeStruct((m, n), BF16),
        compiler_params=_cp(("arbitrary", "arbitrary"), VMEM_BIG),
        name="matmul_swiglu",
    )(a, w1, w3)


def _mm_resid_kernel(*refs, a_widths, nk):
    n_a = len(a_widths)
    a_refs = refs[:n_a]
    w_ref, x_ref, g_ref, o_ref = refs[n_a:n_a + 4]
    part = None
    off = 0
    for a_ref, kw in zip(a_refs, a_widths):
        d = _dot(a_ref[...], w_ref[off:off + kw, :].astype(BF16))
        part = d if part is None else part + d
        off += kw
    if nk == 1:
        o_ref[...] = x_ref[...] + g_ref[...] * part
        return
    acc_ref = refs[n_a + 4]
    k = pl.program_id(2)

    @pl.when(k == 0)
    def _():
        acc_ref[...] = part

    @pl.when(k > 0)
    def _():
        acc_ref[...] += part

    @pl.when(k == nk - 1)
    def _():
        o_ref[...] = x_ref[...] + g_ref[...] * acc_ref[...]


def matmul_resid(a_list, w, layer, x, gate, *, tm, tn, tk=None):
    m, n = x.shape
    k_total = w.shape[1]
    a_widths = tuple(a.shape[1] for a in a_list)
    if tk is None:
        tk = k_total
    nk = k_total // tk
    assert nk == 1 or len(a_list) == 1
    g_cnt, g_rows, _ = gate.shape
    rows_per_gate = m // g_cnt
    if nk == 1:
        a_specs = [pl.BlockSpec((tm, kw), lambda i, j, k: (i, 0)) for kw in a_widths]
        kernel_widths = a_widths
    else:
        a_specs = [pl.BlockSpec((tm, tk), lambda i, j, k: (i, k))]
        kernel_widths = (tk,)
    g_blk = 1 if g_rows == 1 else tm
    return pl.pallas_call(
        functools.partial(_mm_resid_kernel, a_widths=kernel_widths, nk=nk),
        grid=(m // tm, n // tn, nk),
        in_specs=a_specs + [
            pl.BlockSpec((None, tk, tn), lambda i, j, k: (layer, k, j)),
            pl.BlockSpec((tm, tn), lambda i, j, k: (i, j)),
            pl.BlockSpec((None, g_blk, tn), lambda i, j, k: ((i * tm) // rows_per_gate, 0, j)),
        ],
        out_specs=pl.BlockSpec((tm, tn), lambda i, j, k: (i, j)),
        out_shape=jax.ShapeDtypeStruct((m, n), F32),
        scratch_shapes=[pltpu.VMEM((tm, tn), F32)] if nk > 1 else [],
        compiler_params=_cp(("arbitrary", "arbitrary", "arbitrary"), VMEM_BIG),
        name="matmul_resid",
    )(*a_list, w, x, gate)


def _norm_mod_kernel(x_ref, g_ref, sc_ref, sh_ref, o_ref):
    x = x_ref[...]
    y = x * lax.rsqrt(jnp.mean(x * x, axis=-1, keepdims=True) + EPS) * g_ref[...]
    o_ref[...] = (y * (1.0 + sc_ref[...]) + sh_ref[...]).astype(o_ref.dtype)


def norm_mod(x, g, sc, sh, *, tm):
    m, d = x.shape
    g_cnt, g_rows, _ = sc.shape
    rows_per_gate = m // g_cnt
    g_blk = 1 if g_rows == 1 else tm
    mod_spec = pl.BlockSpec((None, g_blk, d), lambda i: ((i * tm) // rows_per_gate, 0, 0))
    return pl.pallas_call(
        _norm_mod_kernel,
        grid=(m // tm,),
        in_specs=[pl.BlockSpec((tm, d), lambda i: (i, 0)),
                  pl.BlockSpec((1, d), lambda i: (0, 0)), mod_spec, mod_spec],
        out_specs=pl.BlockSpec((tm, d), lambda i: (i, 0)),
        out_shape=jax.ShapeDtypeStruct((m, d), BF16),
        compiler_params=_cp(("arbitrary",)),
        name="norm_mod",
    )(x, g, sc, sh)


def _rmsnorm_kernel(x_ref, g_ref, o_ref):
    x = x_ref[...]
    o_ref[...] = x * lax.rsqrt(jnp.mean(x * x, axis=-1, keepdims=True) + EPS) * g_ref[...]


def rmsnorm_rows(x, g, *, tm):
    m, d = x.shape
    return pl.pallas_call(
        _rmsnorm_kernel,
        grid=(m // tm,),
        in_specs=[pl.BlockSpec((tm, d), lambda i: (i, 0)), pl.BlockSpec((1, d), lambda i: (0, 0))],
        out_specs=pl.BlockSpec((tm, d), lambda i: (i, 0)),
        out_shape=jax.ShapeDtypeStruct((m, d), F32),
        compiler_params=_cp(("arbitrary",)),
        name="rmsnorm_rows",
    )(x, g)


def _compress_kernel(s_ref, w1_ref, pe_ref, w2_ref, o_ref):
    n_seg = s_ref.shape[0]
    half = s_ref.shape[1]
    s = s_ref[...].astype(BF16)
    w_lo = w1_ref[0].astype(BF16)
    w_hi = w1_ref[1].astype(BF16)
    both = _dot(s, jnp.concatenate([w_lo, w_hi], axis=1))
    lo = both[:, :D_HEAD]
    hi = both[:, D_HEAD:]
    pe = pe_ref[...].astype(BF16)
    bias = _dot(pe[:, :half], w_lo) + _dot(pe[:, half:], w_hi)
    pre = lo + pltpu.roll(hi, n_seg - 1, 0) + bias[0:1, :]
    out = _dot(_silu(pre).astype(BF16), w2_ref[...].astype(BF16))
    row = lax.broadcasted_iota(jnp.int32, out.shape, 0)
    o_ref[...] = jnp.where(row < n_seg - 1, out, 0.0)


def compress_segments(seg, w1, pe, w2):
    b, _, n_seg, half = seg.shape
    w1r = w1.reshape(2, 2, half, D_HEAD)
    pe8 = jnp.broadcast_to(pe.reshape(2, 1, 2 * half), (2, 8, 2 * half))
    return pl.pallas_call(
        _compress_kernel,
        grid=(b, 4),
        in_specs=[pl.BlockSpec((None, None, n_seg, half), lambda i, j: (i, j, 0, 0)),
                  pl.BlockSpec((None, 2, half, D_HEAD), lambda i, j: (j // 2, 0, 0, 0)),
                  pl.BlockSpec((None, 8, 2 * half), lambda i, j: (j // 2, 0, 0)),
                  pl.BlockSpec((None, D_HEAD, D_HEAD), lambda i, j: (j // 2, 0, 0))],
        out_specs=pl.BlockSpec((None, None, n_seg, D_HEAD), lambda i, j: (i, j, 0, 0)),
        out_shape=jax.ShapeDtypeStruct((b, 4, n_seg, D_HEAD), F32),
        compiler_params=_cp(("arbitrary", "arbitrary"), VMEM_BIG),
        name="compress_segments",
    )(seg, w1r, pe8, w2)


def _slope_column(g, rows_per_head):
    n = HPG * rows_per_head
    h = lax.broadcasted_iota(jnp.int32, (n, 1), 0) // rows_per_head + g * HPG
    out = jnp.zeros((n, 1), F32)
    for hh in range(H_NSA):
        out = jnp.where(h == hh, 2.0 ** (-8.0 * (hh + 1) / H_NSA), out)
    return out


def _masked_softmax_rows(s, valid):
    s = jnp.where(valid, s, NEG)
    m = jnp.max(s, axis=-1, keepdims=True)
    e = jnp.where(valid, jnp.exp(s - m), 0.0)
    d = jnp.sum(e, axis=-1, keepdims=True)
    return e / jnp.where(d > 0.0, d, 1.0)


def _select_blocks(score, n_blocks):
    lane = lax.broadcasted_iota(jnp.int32, score.shape, 1)
    rank = jnp.zeros(score.shape, jnp.int32)
    for i in range(n_blocks):
        c = score[:, i:i + 1]
        ahead = (c > score) | ((c == score) & (lane > i))
        rank = rank + ahead.astype(jnp.int32)
    return (rank < N_SELECT) & (score > -jnp.inf)


def _slc_matrix(n_cmp_rows, nc, n_lanes):
    i = np.arange(n_cmp_rows)[:, None]
    j = np.arange(n_lanes)[None, :]
    m = (i >= 4 * j - 1) & (i <= 4 * j + 3) & (i < nc)
    return jnp.asarray(m.astype(np.float32), dtype=BF16)


def _head_rms(o):
    return o * lax.rsqrt(jnp.mean(o * o, axis=-1, keepdims=True) + EPS)


def _nsa_prompt_kernel(q_ref, sm_ref, selk_ref, selv_ref, wink_ref, winv_ref, cmpk_ref, cmpv_ref,
                       kaug_ref, caug_ref, mslc_ref, wbias_ref, gm_ref, o_ref, *, qb, tk):
    g = pl.program_id(1)
    qi = pl.program_id(2)
    q0 = qi * qb
    rows = HPG * qb
    n_seg = cmpk_ref.shape[0]
    nc = n_seg - 1
    t_len = selk_ref.shape[0]
    ns = t_len // SEL_BLOCK

    q = q_ref[...] * SCALE
    qs = jnp.concatenate([q[:, h * D_HEAD:(h + 1) * D_HEAD] for h in range(HPG)], axis=0)
    qs_b = qs.astype(BF16)
    slope = _slope_column(g, qb)
    rpos = q0 + lax.broadcasted_iota(jnp.int32, (rows, 1), 0) % qb
    lane = lax.broadcasted_iota(jnp.int32, (rows, LANE), 1)
    ones_col = jnp.where(lax.broadcasted_iota(jnp.int32, (tk, LANE), 1) == 0, 1.0, 0.0).astype(BF16)

    qa_c = jnp.concatenate([qs_b, jnp.where(lane == 0, slope * CMP_STRIDE, 0.0).astype(BF16)], axis=1)
    ka_c = jnp.concatenate([cmpk_ref[...].astype(BF16), caug_ref[...]], axis=1)
    ci = lax.broadcasted_iota(jnp.int32, (1, n_seg), 1)
    c_valid = (ci * CMP_STRIDE + (CMP_BLOCK - 1) <= rpos) & (ci < nc)
    p_cmp = _masked_softmax_rows(_dot_nt(qa_c, ka_c), c_valid)
    o_cmp = _dot(p_cmp.astype(BF16), cmpv_ref[...].astype(BF16))

    imp = p_cmp[0:qb]
    for h in range(1, HPG):
        imp = imp + p_cmp[h * qb:(h + 1) * qb]
    p_slc = None
    for part in _split3(imp):
        d = _dot_nt(mslc_ref[...], part)
        p_slc = d if p_slc is None else p_slc + d
    p_slc = p_slc[0:SEL_BLOCK]
    blk = lax.broadcasted_iota(jnp.int32, (SEL_BLOCK, qb), 0)
    cur = (q0 + lax.broadcasted_iota(jnp.int32, (1, qb), 1)) // SEL_BLOCK
    forced = (blk == 0) | (blk > cur - N_LOCAL)
    score = jnp.where(forced, FORCE_SCORE, p_slc)
    score = jnp.where((blk <= cur) & (blk < ns), score, -jnp.inf)
    sub = 8
    pieces = [score[v * sub:(v + 1) * sub] for v in range(SEL_BLOCK // sub)]
    ranks = [jnp.zeros((sub, qb), jnp.int32) for _ in pieces]
    sub_row = lax.broadcasted_iota(jnp.int32, (sub, qb), 0)
    for i in range(ns):
        c = pieces[i // sub][i % sub:i % sub + 1, :]
        for v, piece in enumerate(pieces):
            if v < i // sub:
                ahead = (c > piece).astype(jnp.int32)
            elif v > i // sub:
                ahead = (c >= piece).astype(jnp.int32)
            else:
                ahead = jnp.where(sub_row > i % sub, (c >= piece).astype(jnp.int32), (c > piece).astype(jnp.int32))
            ranks[v] = ranks[v] + ahead
    rank = jnp.concatenate(ranks, axis=0)
    sel = (rank < N_SELECT) & (score > -jnp.inf)
    pen_t = jnp.where(sel, 0.0, PEN)
    pen = jnp.transpose(jnp.concatenate([pen_t, jnp.zeros((LANE - SEL_BLOCK, qb), F32)], axis=0))
    pen4 = jnp.concatenate([pen] * HPG, axis=0)
    q_aug = jnp.where(lane < SEL_BLOCK, pen4,
                      jnp.where(lane == SEL_BLOCK, slope * SEL_BLOCK,
                                jnp.where(lane == SEL_BLOCK + 1, slope, 0.0)))
    qa = jnp.concatenate([qs_b, q_aug.astype(BF16)], axis=1)

    def sel_scores(k0):
        ka = jnp.concatenate([selk_ref[pl.ds(k0, tk), :].astype(BF16), kaug_ref[pl.ds(k0, tk), :]], axis=1)
        return _dot_nt(qa, ka)

    def sel_update(st, k0, m_run, acc):
        va = jnp.concatenate([selv_ref[pl.ds(k0, tk), :].astype(BF16), ones_col], axis=1)
        m_new = jnp.maximum(m_run, jnp.max(st, axis=-1, keepdims=True))
        p = jnp.exp(st - m_new).astype(BF16)
        return m_new, jnp.exp(m_run - m_new) * acc + _dot(p, va)

    def sel_tile(kt, carry):
        m_run, acc, st = carry
        k0 = pl.multiple_of(kt * tk, tk)
        st_next = sel_scores(k0 + tk)
        m_new, acc = sel_update(st, k0, m_run, acc)
        return m_new, acc, st_next

    n_full = q0 // tk
    init = (jnp.full((rows, 1), NEG, F32), jnp.zeros((rows, 2 * LANE), F32), sel_scores(0))
    m_run, acc_sel, st = lax.fori_loop(0, n_full, sel_tile, init)
    k0 = pl.multiple_of(n_full * tk, tk)
    st = jnp.where(k0 + lax.broadcasted_iota(jnp.int32, (1, tk), 1) <= rpos, st, NEG)
    _, acc_sel = sel_update(st, k0, m_run, acc_sel)
    o_sel = acc_sel[:, 0:D_HEAD] / acc_sel[:, D_HEAD:D_HEAD + 1]

    wk = WINDOW + qb
    ks = pl.multiple_of(jnp.maximum(q0 - WINDOW, 0), qb)
    s = _dot_nt(qs_b, wink_ref[pl.ds(ks, wk), :].astype(BF16)) + wbias_ref[...]
    m = jnp.max(s, axis=-1, keepdims=True)
    p = jnp.exp(s - m).astype(BF16)
    ones_w = jnp.where(lax.broadcasted_iota(jnp.int32, (wk, LANE), 1) == 0, 1.0, 0.0).astype(BF16)
    acc_win = _dot(p, jnp.concatenate([winv_ref[pl.ds(ks, wk), :].astype(BF16), ones_w], axis=1))
    o_win = acc_win[:, 0:D_HEAD] / acc_win[:, D_HEAD:D_HEAD + 1]

    gates = 1.0 / (1.0 + jnp.exp(-sm_ref[...]))

    def gate_col(branch):
        cols = []
        for h in range(HPG):
            c0 = SM_GATE + h * 3 + branch
            c1 = SM_GATE + (HPG + h) * 3 + branch
            cols.append(jnp.where(g == 0, gates[:, c0:c0 + 1], gates[:, c1:c1 + 1]))
        return jnp.concatenate(cols, axis=0)

    o = gate_col(0) * o_cmp + gate_col(1) * o_sel + gate_col(2) * o_win
    o = _head_rms(o)
    for h in range(HPG):
        o_ref[:, h * D_HEAD:(h + 1) * D_HEAD] = (
            o[h * qb:(h + 1) * qb] * gm_ref[:, h * D_HEAD:(h + 1) * D_HEAD]).astype(o_ref.dtype)


def _key_aug(t_len):
    pos = np.arange(t_len)[:, None]
    lane = np.arange(LANE)[None, :]
    a = np.where(lane < SEL_BLOCK, (pos // SEL_BLOCK == lane).astype(np.float32),
                 np.where(lane == SEL_BLOCK, (pos // SEL_BLOCK).astype(np.float32),
                          np.where(lane == SEL_BLOCK + 1, (pos % SEL_BLOCK).astype(np.float32), 0.0)))
    return jnp.asarray(a, dtype=BF16)


def _cmp_aug(n_seg):
    a = np.zeros((n_seg, LANE), np.float32)
    a[:, 0] = np.arange(n_seg)
    return jnp.asarray(a, dtype=BF16)


def _window_bias(qb):
    n_pat = WINDOW // qb + 1
    r = np.arange(HPG * qb)
    head = r // qb
    dist = (np.arange(n_pat)[:, None, None] * qb + (r % qb)[None, :, None]
            - np.arange(WINDOW + qb)[None, None, :])
    valid = (dist >= 0) & (dist <= WINDOW)
    out = np.empty((G_NSA,) + dist.shape, np.float32)
    for g in range(G_NSA):
        slope = 2.0 ** (-8.0 * (g * HPG + head + 1) / H_NSA)
        out[g] = np.where(valid, -slope[None, :, None] * dist, NEG)
    return jnp.asarray(out)


def nsa_prompt(z, cmp_kv, g_mix, *, qb=256, tk=512):
    b, t_len, _ = z.shape
    assert t_len // SEL_BLOCK <= SEL_BLOCK and t_len % tk == 0 and tk % qb == 0 and WINDOW % qb == 0
    n_seg = cmp_kv.shape[2]
    assert n_seg <= 256
    gw = HPG * D_HEAD
    kv_blk = C_KV // D_HEAD
    n_pat = WINDOW // qb
    wbias = _window_bias(qb)

    def kv_spec(slot):
        return pl.BlockSpec((None, t_len, D_HEAD), lambda i, g, q: (i, 0, kv_blk + 2 * slot + g))

    return pl.pallas_call(
        functools.partial(_nsa_prompt_kernel, qb=qb, tk=tk),
        grid=(b, G_NSA, t_len // qb),
        in_specs=[pl.BlockSpec((None, qb, gw), lambda i, g, q: (i, q, g)),
                  pl.BlockSpec((None, qb, LANE), lambda i, g, q: (i, q, C_SM // LANE)),
                  kv_spec(2), kv_spec(3), kv_spec(4), kv_spec(5),
                  pl.BlockSpec((None, None, n_seg, D_HEAD), lambda i, g, q: (i, g, 0, 0)),
                  pl.BlockSpec((None, None, n_seg, D_HEAD), lambda i, g, q: (i, 2 + g, 0, 0)),
                  pl.BlockSpec((t_len, LANE), lambda i, g, q: (0, 0)),
                  pl.BlockSpec((n_seg, LANE), lambda i, g, q: (0, 0)),
                  pl.BlockSpec((LANE, n_seg), lambda i, g, q: (0, 0)),
                  pl.BlockSpec((None, None, HPG * qb, WINDOW + qb),
                               lambda i, g, q: (g, jnp.minimum(q, n_pat), 0, 0)),
                  pl.BlockSpec((1, gw), lambda i, g, q: (0, g))],
        out_specs=pl.BlockSpec((None, qb, gw), lambda i, g, q: (i, q, g)),
        out_shape=jax.ShapeDtypeStruct((b, t_len, H_NSA * D_HEAD), BF16),
        compiler_params=_cp(("arbitrary", "arbitrary", "arbitrary"), VMEM_BIG),
        name="nsa_prompt",
    )(z, z, z, z, z, z, cmp_kv, cmp_kv, _key_aug(t_len), _cmp_aug(n_seg),
      jnp.transpose(_slc_matrix(n_seg, n_seg - 1, LANE)), wbias, g_mix)


def _sb_prompt_kernel(q_ref, k_ref, v_ref, gm_ref, o_ref, *, qb, tk):
    qi = pl.program_id(2)
    q0 = qi * qb
    q = (q_ref[...] * (SCALE * LOG2E)).astype(BF16)
    rpos = q0 + lax.broadcasted_iota(jnp.int32, (qb, 1), 0)
    row = lax.broadcasted_iota(jnp.int32, (2 * tk, tk), 0) % tk
    tri2 = jnp.where(row > lax.broadcasted_iota(jnp.int32, (2 * tk, tk), 1), 1.0, 0.0).astype(BF16)
    n_diag = qb // tk

    def tile(k0, carry, masked):
        run, acc = carry
        z = _dot_nt(q, k_ref[pl.ds(k0, tk), :].astype(BF16))
        lb = _log2_sigmoid(z)
        lr = lb - z
        if masked:
            mask = (k0 + lax.broadcasted_iota(jnp.int32, (1, tk), 1)) < rpos
            lr = jnp.where(mask, lr, 0.0)
        hi, lo = _split2(lr)
        after = _dot(jnp.concatenate([hi, lo], axis=1), tri2) + run
        a = jnp.exp2(lb + after)
        if masked:
            a = jnp.where(mask, a, 0.0)
        acc = acc + _dot(a.astype(BF16), v_ref[pl.ds(k0, tk), :].astype(BF16))
        return after[:, 0:1] + lr[:, 0:1], acc

    carry = (jnp.zeros((qb, 1), F32), jnp.zeros((qb, D_HEAD), F32))
    for it in range(n_diag):
        carry = tile(pl.multiple_of(q0 + (n_diag - 1 - it) * tk, tk), carry, True)

    def full_tiles(it, carry):
        for u in range(n_diag):
            carry = tile(pl.multiple_of(q0 - (it * n_diag + u + 1) * tk, tk), carry, False)
        return carry

    _, acc = lax.fori_loop(0, qi, full_tiles, carry)
    o_ref[...] = (_head_rms(acc) * gm_ref[...]).astype(o_ref.dtype)


def sb_prompt(z, g_mix, *, qb=1024, tk=256):
    b, t_len, _ = z.shape
    assert qb % tk == 0 and t_len % qb == 0
    blk = C_SB // D_HEAD
    gm_blk = (H_NSA * D_HEAD) // D_HEAD
    return pl.pallas_call(
        functools.partial(_sb_prompt_kernel, qb=qb, tk=tk),
        grid=(b, H_SB, t_len // qb),
        in_specs=[pl.BlockSpec((None, qb, D_HEAD), lambda i, h, q: (i, q, blk + h)),
                  pl.BlockSpec((None, t_len, D_HEAD), lambda i, h, q: (i, 0, blk + H_SB + h)),
                  pl.BlockSpec((None, t_len, D_HEAD), lambda i, h, q: (i, 0, blk + 2 * H_SB + h)),
                  pl.BlockSpec((1, D_HEAD), lambda i, h, q: (0, gm_blk + h))],
        out_specs=pl.BlockSpec((None, qb, D_HEAD), lambda i, h, q: (i, q, h)),
        out_shape=jax.ShapeDtypeStruct((b, t_len, H_SB * D_HEAD), BF16),
        compiler_params=_cp(("arbitrary", "arbitrary", "arbitrary")),
        name="sb_prompt",
    )(z, z, z, g_mix)


N_LEVELS = 7


def _gla_level_tables():
    c = GLA_CHUNK
    idx = np.arange(c)
    sums = np.zeros(((N_LEVELS + 1) * c, c), np.float32)
    pair = np.zeros((N_LEVELS + 1, c, c), np.float32)
    sums[:c] = (idx[None, :] <= idx[:, None])
    pair[0] = np.eye(c)
    for lv in range(N_LEVELS):
        mid = ((idx >> (lv + 1)) << (lv + 1)) + (1 << lv)
        upper = idx >= mid
        j = idx[None, :]
        in_up = upper[:, None] & (j >= mid[:, None]) & (j <= idx[:, None])
        in_lo = (~upper)[:, None] & (j > idx[:, None]) & (j <= mid[:, None] - 1)
        sums[(lv + 1) * c:(lv + 2) * c] = in_up | in_lo
        same = (idx[:, None] >> (lv + 1)) == (idx[None, :] >> (lv + 1))
        pair[lv + 1] = same & upper[:, None] & (~upper)[None, :]
    return jnp.asarray(sums, dtype=BF16), jnp.asarray(pair, dtype=F32)


def _gla_prompt_kernel(qg_ref, kg_ref, vg_ref, og_ref, sm_ref, wa2_ref, ba_ref, sums_ref, pair_ref,
                       gm_ref, o_ref, st_ref, state_ref, *, tb):
    nt = pl.program_id(1)
    c = GLA_CHUNK

    @pl.when(nt == 0)
    def _():
        state_ref[...] = jnp.zeros_like(state_ref)

    wa2 = wa2_ref[...].astype(BF16)
    sums = sums_ref[...]

    def chunk(ci, _):
        r0 = pl.multiple_of(ci * c, c)
        rows = pl.ds(r0, c)
        x = _dot(sm_ref[rows, :].astype(BF16), wa2) + ba_ref[...]
        loga = _log_sigmoid(x) * (1.0 / GLA_TAU)
        hi, lo = _split2(loga)
        dsum = _dot(sums, hi) + _dot(sums, lo)
        cb = dsum[0:c]
        q = qg_ref[rows, :] * (GLA_DK ** -0.5)
        k = kg_ref[rows, :]
        v = vg_ref[rows, :].astype(BF16)
        qf = [q.astype(BF16)]
        kf = [k.astype(BF16)]
        for lv in range(N_LEVELS):
            e = jnp.exp(dsum[(lv + 1) * c:(lv + 2) * c])
            qf.append((q * e).astype(BF16))
            kf.append((k * e).astype(BF16))
        c_last = cb[c - 1:c, :]
        q_in = (q * jnp.exp(cb)).astype(BF16)
        k_out = (k * jnp.exp(c_last - cb)).astype(BF16)
        decay = jnp.exp(c_last)
        for h in range(H_GLA):
            ks = slice(h * GLA_DK, (h + 1) * GLA_DK)
            vs = slice(h * GLA_DV, (h + 1) * GLA_DV)
            att = None
            for lv in range(N_LEVELS + 1):
                term = pair_ref[lv] * _dot_nt(qf[lv][:, ks], kf[lv][:, ks])
                att = term if att is None else att + term
            s_t = state_ref[h]
            o_h = _dot(att.astype(BF16), v[:, vs]) + _dot_nt(q_in[:, ks], s_t.astype(BF16))
            state_ref[h] = s_t * decay[:, ks] + _dot_tn(v[:, vs], k_out[:, ks])
            og = og_ref[rows, vs]
            o_ref[rows, vs] = (_head_rms(o_h) * _silu(og) * gm_ref[:, vs]).astype(o_ref.dtype)
        return 0

    lax.fori_loop(0, tb // c, chunk, 0, unroll=4)

    @pl.when(nt == pl.num_programs(1) - 1)
    def _():
        st_ref[...] = state_ref[...]


def _wa2_padded(gla_wa2):
    w = jnp.zeros((LANE, H_GLA * GLA_DK), F32)
    return w.at[SM_ALR:SM_ALR + GLA_RANK].set(gla_wa2)


def gla_prompt(z, gla_wa2, gla_ba, g_mix, *, tb=512):
    b, t_len, _ = z.shape
    kw = H_GLA * GLA_DK
    vw = H_GLA * GLA_DV
    sums, pair = _gla_level_tables()
    n_rows = sums.shape[0]
    return pl.pallas_call(
        functools.partial(_gla_prompt_kernel, tb=tb),
        grid=(b, t_len // tb),
        in_specs=[pl.BlockSpec((None, tb, kw), lambda i, n: (i, n, C_QG // kw)),
                  pl.BlockSpec((None, tb, kw), lambda i, n: (i, n, C_KG // kw)),
                  pl.BlockSpec((None, tb, vw), lambda i, n: (i, n, C_VG // vw)),
                  pl.BlockSpec((None, tb, vw), lambda i, n: (i, n, C_OG // vw)),
                  pl.BlockSpec((None, tb, LANE), lambda i, n: (i, n, C_SM // LANE)),
                  pl.BlockSpec((LANE, kw), lambda i, n: (0, 0)),
                  pl.BlockSpec((1, kw), lambda i, n: (0, 0)),
                  pl.BlockSpec((n_rows, GLA_CHUNK), lambda i, n: (0, 0)),
                  pl.BlockSpec((N_LEVELS + 1, GLA_CHUNK, GLA_CHUNK), lambda i, n: (0, 0, 0)),
                  pl.BlockSpec((1, vw), lambda i, n: (0, (H_NSA + H_SB) * D_HEAD // vw))],
        out_specs=[pl.BlockSpec((None, tb, vw), lambda i, n: (i, n, 0)),
                   pl.BlockSpec((None, H_GLA, GLA_DV, GLA_DK), lambda i, n: (i, 0, 0, 0))],
        out_shape=[jax.ShapeDtypeStruct((b, t_len, vw), BF16),
                   jax.ShapeDtypeStruct((b, H_GLA, GLA_DV, GLA_DK), F32)],
        scratch_shapes=[pltpu.VMEM((H_GLA, GLA_DV, GLA_DK), F32)],
        compiler_params=_cp(("arbitrary", "arbitrary")),
        name="gla_prompt",
    )(z, z, z, z, z, _wa2_padded(gla_wa2), gla_ba.reshape(1, kw), sums, pair, g_mix)


def _gather_pages(page_refs, o_ref):
    slots = 4 * G_NSA
    seg_per_page = page_refs[0].shape[0] // (slots * CMP_STRIDE)
    for pair in range(len(page_refs) // 2):
        r0 = pair * 2 * seg_per_page
        for cg in range(4):
            for p in range(CMP_STRIDE):
                rows = [page_refs[2 * pair + u][pl.ds(p * slots + cg, seg_per_page, stride=CMP_STRIDE * slots), :]
                        for u in range(2)]
                o_ref[cg, r0:r0 + 2 * seg_per_page, p * D_HEAD:(p + 1) * D_HEAD] = (
                    jnp.concatenate(rows, axis=0).astype(o_ref.dtype))


def _rows_view(cache):
    depth, n_pool, page, a, b, d = cache.shape
    return cache.reshape(depth, n_pool, page * a * b, d)


def _ffn_down_gather_kernel(pt_ref, *refs, n_pg):
    page_refs = refs[:n_pg]
    a_ref, w_ref, x_ref, g_ref, o_ref, seg_ref = refs[n_pg:]
    o_ref[...] = x_ref[...] + g_ref[...] * _dot(a_ref[...], w_ref[...].astype(BF16))
    _gather_pages(page_refs, seg_ref)


def ffn_down_gather(a, w, layer, x, gate, cache_nsa, page_table, *, tm, tn):
    m, n = x.shape
    k = w.shape[1]
    page = cache_nsa.shape[2]
    b, n_pages = page_table.shape
    cache = _rows_view(cache_nsa)
    n_i, n_j = m // tm, n // tn
    assert n_i == b and n_pages % n_j == 0 and gate.shape[1] == 1
    n_pg = n_pages // n_j
    seg_per_page = page // CMP_STRIDE
    rows_per_gate = m // gate.shape[0]

    def page_spec(u):
        return pl.BlockSpec((None, None, cache.shape[2], D_HEAD),
                            lambda i, j, pt: (layer, pt[i * n_pages + j * n_pg + u], 0, 0))

    return pl.pallas_call(
        functools.partial(_ffn_down_gather_kernel, n_pg=n_pg),
        grid_spec=pltpu.PrefetchScalarGridSpec(
            num_scalar_prefetch=1,
            grid=(n_i, n_j),
            in_specs=[page_spec(u) for u in range(n_pg)] + [
                pl.BlockSpec((tm, k), lambda i, j, pt: (i, 0), pipeline_mode=pl.Buffered(1)),
                pl.BlockSpec((None, k, tn), lambda i, j, pt: (layer, 0, j)),
                pl.BlockSpec((tm, tn), lambda i, j, pt: (i, j)),
                pl.BlockSpec((None, 1, tn), lambda i, j, pt: ((i * tm) // rows_per_gate, 0, j))],
            out_specs=[pl.BlockSpec((tm, tn), lambda i, j, pt: (i, j)),
                       pl.BlockSpec((None, 4, n_pg * seg_per_page, CMP_STRIDE * D_HEAD),
                                    lambda i, j, pt: (i, 0, j, 0))],
        ),
        out_shape=[jax.ShapeDtypeStruct((m, n), F32),
                   jax.ShapeDtypeStruct((b, 4, n_pages * seg_per_page, CMP_STRIDE * D_HEAD), BF16)],
        compiler_params=_cp(("arbitrary", "arbitrary"), VMEM_BIG),
        name="ffn_down_gather",
    )(page_table.reshape(-1), *([cache] * n_pg), a, w, x, gate)


def _nsa_sample_select_kernel(q_ref, cmp_ref, mslc_ref, ocmp_ref, idx_ref, *, pos, n_blk_lanes):
    n_seg = cmp_ref.shape[1]
    nc = n_seg - 1
    ns = pos // SEL_BLOCK + 1
    cur = pos // SEL_BLOCK
    q = (q_ref[...] * SCALE).astype(BF16)
    row = lax.broadcasted_iota(jnp.int32, (H_NSA, 1), 0)
    slope = _slope_column(0, 1)
    slope = jnp.concatenate([slope, _slope_column(1, 1)], axis=0)
    s = jnp.where(row < HPG, _dot_nt(q, cmp_ref[0].astype(BF16)), _dot_nt(q, cmp_ref[1].astype(BF16)))
    ci = lax.broadcasted_iota(jnp.int32, (1, n_seg), 1)
    c_dist = pos - (ci * CMP_STRIDE + (CMP_BLOCK - 1))
    p = _masked_softmax_rows(s - slope * c_dist.astype(F32), (c_dist >= 0) & (ci < nc))
    pb = p.astype(BF16)
    ocmp_ref[...] = jnp.where(row < HPG, _dot(pb, cmp_ref[2].astype(BF16)), _dot(pb, cmp_ref[3].astype(BF16)))

    imp = jnp.concatenate([jnp.sum(p[g * HPG:(g + 1) * HPG], axis=0, keepdims=True) for g in range(G_NSA)]
                          + [jnp.zeros((H_NSA - G_NSA, n_seg), F32)], axis=0)
    p_slc = None
    for part in _split3(imp):
        d = _dot(part, mslc_ref[...])
        p_slc = d if p_slc is None else p_slc + d
    blk = lax.broadcasted_iota(jnp.int32, p_slc.shape, 1)
    forced = (blk == 0) | (blk > cur - N_LOCAL)
    score = jnp.where(forced, FORCE_SCORE, p_slc)
    score = jnp.where(blk <= cur, score, -jnp.inf)
    sel = _select_blocks(score, ns)
    upper = (lax.broadcasted_iota(jnp.int32, (n_blk_lanes, n_blk_lanes), 0)
             < lax.broadcasted_iota(jnp.int32, (n_blk_lanes, n_blk_lanes), 1))
    before = _dot(jnp.where(sel, 1.0, 0.0).astype(BF16), jnp.where(upper, 1.0, 0.0).astype(BF16))
    blk_f = blk.astype(F32)
    out_lane = lax.broadcasted_iota(jnp.int32, (H_NSA, LANE), 1)
    out = jnp.zeros((H_NSA, LANE), F32)
    for n in range(N_SELECT):
        v = jnp.sum(jnp.where(sel & (before == float(n)), blk_f, 0.0), axis=-1, keepdims=True)
        out = jnp.where(out_lane == n, v, out)
    idx_ref[...] = out.astype(jnp.int32)


def nsa_sample_select(q8, cmp_kv, *, pos):
    b = q8.shape[0]
    n_seg = cmp_kv.shape[2]
    ns = pos // SEL_BLOCK + 1
    assert ns >= N_SELECT
    n_blk_lanes = -(-ns // LANE) * LANE
    return pl.pallas_call(
        functools.partial(_nsa_sample_select_kernel, pos=pos, n_blk_lanes=n_blk_lanes),
        grid=(b,),
        in_specs=[pl.BlockSpec((None, H_NSA, D_HEAD), lambda i: (i, 0, 0)),
                  pl.BlockSpec((None, 4, n_seg, D_HEAD), lambda i: (i, 0, 0, 0)),
                  pl.BlockSpec((n_seg, n_blk_lanes), lambda i: (0, 0))],
        out_specs=[pl.BlockSpec((None, H_NSA, D_HEAD), lambda i: (i, 0, 0)),
                   pl.BlockSpec((None, H_NSA, LANE), lambda i: (i, 0, 0))],
        out_shape=[jax.ShapeDtypeStruct((b, H_NSA, D_HEAD), F32),
                   jax.ShapeDtypeStruct((b, H_NSA, LANE), jnp.int32)],
        compiler_params=_cp(("arbitrary",)),
        name="nsa_sample_select",
    )(q8, cmp_kv, _slc_matrix(n_seg, n_seg - 1, n_blk_lanes))


def _softmax_with_new(s, valid, s_new, v_mat, v_new):
    s = jnp.where(valid, s, NEG)
    m = jnp.maximum(jnp.max(s, axis=-1, keepdims=True), s_new)
    e = jnp.where(valid, jnp.exp(s - m), 0.0)
    e_new = jnp.exp(s_new - m)
    d = jnp.sum(e, axis=-1, keepdims=True) + e_new
    return (_dot(e.astype(BF16), v_mat) + e_new.astype(BF16).astype(F32) * v_new) / d


def _nsa_sample_attend_kernel(idx_ref, pt_ref, *refs, pos):
    sel_refs = refs[:N_SELECT]
    (q_ref, nk_ref, nv_ref, nwk_ref, nwv_ref, sm_ref, win_ref, ocmp_ref, gm_ref, o_ref) = refs[N_SELECT:]
    b = pl.program_id(0)
    g = pl.program_id(1)
    n_past_blocks = pos // SEL_BLOCK
    q = (q_ref[...] * SCALE).astype(BF16)
    qf = q.astype(F32)
    slope = jnp.concatenate([_slope_column(0, 1), _slope_column(1, 1)], axis=0)

    def new_row(ref):
        return ref[pl.ds(b, 1), :].astype(BF16)

    def slot_rows(ref, slot, n_tok, n_slots):
        both = [ref[pl.ds(slot * G_NSA + gg, n_tok, stride=n_slots), :] for gg in range(G_NSA)]
        return jnp.where(g == 0, both[0], both[1]).astype(BF16)

    k_all = jnp.concatenate([slot_rows(r, 2, SEL_BLOCK, 4 * G_NSA) for r in sel_refs], axis=0)
    v_all = jnp.concatenate([slot_rows(r, 3, SEL_BLOCK, 4 * G_NSA) for r in sel_refs], axis=0)
    width = N_SELECT * SEL_BLOCK
    lane = lax.broadcasted_iota(jnp.int32, (1, width), 1)
    tok = jnp.zeros((1, width), jnp.int32)
    in_cache = jnp.zeros((1, width), jnp.bool_)
    for n in range(N_SELECT):
        blk_id = idx_ref[(b * G_NSA + g) * N_SELECT + n]
        here = (lane // SEL_BLOCK) == n
        tok = jnp.where(here, blk_id * SEL_BLOCK + lane % SEL_BLOCK, tok)
        in_cache = in_cache | (here & (blk_id < n_past_blocks))
    dist = pos - tok
    s = _dot_nt(q, k_all) - slope * dist.astype(F32)
    k_new = new_row(nk_ref)
    s_new = jnp.sum(qf * k_new.astype(F32), axis=-1, keepdims=True)
    o_sel = _softmax_with_new(s, in_cache & (dist >= 0), s_new, v_all, new_row(nv_ref).astype(F32))

    n_win = win_ref.shape[0] // (2 * G_NSA)
    wi = lax.broadcasted_iota(jnp.int32, (1, n_win), 1)
    w_dist = n_win - wi
    s = _dot_nt(q, slot_rows(win_ref, 0, n_win, 2 * G_NSA)) - slope * w_dist.astype(F32)
    s_new = jnp.sum(qf * new_row(nwk_ref).astype(F32), axis=-1, keepdims=True)
    o_win = _softmax_with_new(s, w_dist <= WINDOW, s_new, slot_rows(win_ref, 1, n_win, 2 * G_NSA),
                              new_row(nwv_ref).astype(F32))

    gates = 1.0 / (1.0 + jnp.exp(-sm_ref[pl.ds(b, 1), :]))
    hrow = lax.broadcasted_iota(jnp.int32, (H_NSA, LANE), 0)
    glane = lax.broadcasted_iota(jnp.int32, (H_NSA, LANE), 1)

    def gate_col(branch):
        return jnp.sum(jnp.where(glane == SM_GATE + 3 * hrow + branch, gates, 0.0), axis=-1, keepdims=True)

    o = gate_col(0) * ocmp_ref[...] + gate_col(1) * o_sel + gate_col(2) * o_win
    o = _head_rms(o) * gm_ref[...]
    o_ref[...] = jnp.where(g == 0, o[0:HPG], o[HPG:H_NSA])


def nsa_sample_attend(zs, q8, o_cmp, idx, cache_nsa, state_win, page_table, gm8, layer, *, pos):
    page = cache_nsa.shape[2]
    b, n_pages = page_table.shape
    halves = page // SEL_BLOCK
    cache = _rows_view(cache_nsa)
    blk_rows = cache.shape[2] // halves
    win = _rows_view(state_win)
    last_blk = pos // SEL_BLOCK - 1
    kv_blk = C_KV // D_HEAD

    def sel_spec(n):
        def index_map(i, g, idx_ref, pt_ref):
            blk_id = jnp.minimum(idx_ref[(i * G_NSA + g) * N_SELECT + n], last_blk)
            return (layer, pt_ref[i * n_pages + blk_id // halves], blk_id % halves, 0)
        return pl.BlockSpec((None, None, blk_rows, D_HEAD), index_map)

    def zs_spec(slot):
        return pl.BlockSpec((MS, D_HEAD), lambda i, g, a, c: (0, kv_blk + 2 * slot + g))

    in_specs = ([sel_spec(n) for n in range(N_SELECT)]
                + [pl.BlockSpec((None, H_NSA, D_HEAD), lambda i, g, a, c: (i, 0, 0)),
                   zs_spec(2), zs_spec(3), zs_spec(4), zs_spec(5),
                   pl.BlockSpec((MS, LANE), lambda i, g, a, c: (0, C_SM // LANE)),
                   pl.BlockSpec((None, None, win.shape[2], D_HEAD), lambda i, g, a, c: (layer, i, 0, 0)),
                   pl.BlockSpec((None, H_NSA, D_HEAD), lambda i, g, a, c: (i, 0, 0)),
                   pl.BlockSpec((H_NSA, D_HEAD), lambda i, g, a, c: (0, 0))])
    return pl.pallas_call(
        functools.partial(_nsa_sample_attend_kernel, pos=pos),
        grid_spec=pltpu.PrefetchScalarGridSpec(
            num_scalar_prefetch=2,
            grid=(b, G_NSA),
            in_specs=in_specs,
            out_specs=pl.BlockSpec((None, None, HPG, D_HEAD), lambda i, g, a, c: (i, g, 0, 0)),
        ),
        out_shape=jax.ShapeDtypeStruct((b, G_NSA, HPG, D_HEAD), F32),
        compiler_params=_cp(("arbitrary", "arbitrary")),
        name="nsa_sample_attend",
    )(idx, page_table.reshape(-1), *([cache] * N_SELECT), q8, zs, zs, zs, zs, zs, win, o_cmp, gm8)


SB_PAGES_PER_STEP = 16


def _sb_sample_kernel(pt_ref, *refs):
    page_refs = refs[:SB_PAGES_PER_STEP]
    q_ref, gm_ref, o_ref, run_ref, acc_ref = refs[SB_PAGES_PER_STEP:]
    b = pl.program_id(0)
    s_idx = pl.program_id(1)
    slots = 2 * H_SB
    page = page_refs[0].shape[0] // slots
    kw = H_SB * D_HEAD
    n_u = SB_PAGES_PER_STEP

    @pl.when(s_idx == 0)
    def _():
        run_ref[...] = jnp.zeros_like(run_ref)
        acc_ref[...] = jnp.zeros_like(acc_ref)

    def heads_on_lanes(ref, first_slot):
        return jnp.concatenate([ref[pl.ds(first_slot + h, page, stride=slots), :].astype(BF16)
                                for h in range(H_SB)], axis=1)

    qrow = q_ref[pl.ds(b, 1), :] * SCALE
    hrow = lax.broadcasted_iota(jnp.int32, (8, kw), 0)
    hlane = lax.broadcasted_iota(jnp.int32, (8, kw), 1) // D_HEAD
    qm = jnp.where(hrow == hlane, qrow, 0.0).astype(BF16)
    tri = (lax.broadcasted_iota(jnp.int32, (page, page), 0) > lax.broadcasted_iota(jnp.int32, (page, page), 1))
    tri = jnp.where(tri, 1.0, 0.0).astype(BF16)

    k_all = jnp.concatenate([heads_on_lanes(r, 0) for r in page_refs], axis=0)
    z = _dot_nt(qm, k_all)
    lb = _log_sigmoid(z)
    lr = lb - z
    lr_rows = jnp.concatenate([lr[:, u * page:(u + 1) * page] for u in range(n_u)], axis=0)
    hi, lo = _split2(lr_rows)
    local = _dot(hi, tri) + _dot(lo, tri)
    total = local[:, 0:1] + lr_rows[:, 0:1]
    run = run_ref[:, 0:1]
    offs = [None] * n_u
    for u in range(n_u - 1, -1, -1):
        offs[u] = run
        run = run + total[8 * u:8 * u + 8]
    run_ref[...] = jnp.broadcast_to(run, run_ref.shape)
    after = jnp.concatenate([local[8 * u:8 * u + 8] + offs[u] for u in range(n_u)], axis=1)
    a = jnp.exp(lb + after)
    v_all = jnp.concatenate([heads_on_lanes(r, H_SB) for r in page_refs], axis=0)
    acc = acc_ref[...] + _dot(a.astype(BF16), v_all)
    acc_ref[...] = acc

    @pl.when(s_idx == pl.num_programs(1) - 1)
    def _():
        o = jnp.concatenate([acc[h:h + 1, h * D_HEAD:(h + 1) * D_HEAD] for h in range(H_SB)], axis=0)
        o_ref[...] = _head_rms(o) * gm_ref[...]


def sb_sample(zs, cache_sb, page_table, gm4, layer):
    b, n_pages = page_table.shape
    cache = _rows_view(cache_sb)
    steps = n_pages // SB_PAGES_PER_STEP

    def page_spec(u):
        return pl.BlockSpec(
            (None, None, cache.shape[2], D_HEAD),
            lambda i, s, pt: (layer, pt[i * n_pages + (steps - 1 - s) * SB_PAGES_PER_STEP + u], 0, 0))

    return pl.pallas_call(
        _sb_sample_kernel,
        grid_spec=pltpu.PrefetchScalarGridSpec(
            num_scalar_prefetch=1,
            grid=(b, steps),
            in_specs=[page_spec(u) for u in range(SB_PAGES_PER_STEP)]
            + [pl.BlockSpec((MS, H_SB * D_HEAD), lambda i, s, pt: (0, C_SB // (H_SB * D_HEAD))),
               pl.BlockSpec((H_SB, D_HEAD), lambda i, s, pt: (0, 0))],
            out_specs=pl.BlockSpec((None, H_SB, D_HEAD), lambda i, s, pt: (i, 0, 0)),
            scratch_shapes=[pltpu.VMEM((8, LANE), F32), pltpu.VMEM((8, H_SB * D_HEAD), F32)],
        ),
        out_shape=jax.ShapeDtypeStruct((b, H_SB, D_HEAD), F32),
        compiler_params=_cp(("arbitrary", "arbitrary"), VMEM_BIG),
        name="sb_sample",
    )(page_table.reshape(-1), *([cache] * SB_PAGES_PER_STEP), zs, gm4)


def _gla_sample_kernel(qg_ref, kg_ref, vg_ref, og_ref, sm_ref, wa2_ref, ba_ref, st_ref, gm_ref,
                       o_ref, ns_ref, *, n_b):
    x = _dot(sm_ref[...].astype(BF16), wa2_ref[...].astype(BF16)) + ba_ref[...]
    decay = jnp.exp(_log_sigmoid(x) * (1.0 / GLA_TAU))
    eye = (lax.broadcasted_iota(jnp.int32, (GLA_DK, GLA_DK), 0)
           == lax.broadcasted_iota(jnp.int32, (GLA_DK, GLA_DK), 1))

    def column(row):
        return jnp.sum(jnp.where(eye, row, 0.0), axis=1, keepdims=True)

    o_ref[...] = jnp.zeros_like(o_ref)
    for b in range(n_b):
        for h in range(H_GLA):
            ks = slice(h * GLA_DK, (h + 1) * GLA_DK)
            vs = slice(h * GLA_DV, (h + 1) * GLA_DV)
            s_new = (column(decay[b:b + 1, ks]) * st_ref[b, h]
                     + column(kg_ref[b:b + 1, ks]) * vg_ref[b:b + 1, vs])
            ns_ref[b, h] = s_new
            q_col = column(qg_ref[b:b + 1, ks] * (GLA_DK ** -0.5))
            o = jnp.sum(q_col * s_new, axis=0, keepdims=True)
            o_ref[b:b + 1, vs] = _head_rms(o) * _silu(og_ref[b:b + 1, vs]) * gm_ref[:, vs]


def gla_sample(zs, gla_wa2, gla_ba, state, g_mix):
    n_b = state.shape[0]
    kw = H_GLA * GLA_DK
    vw = H_GLA * GLA_DV
    return pl.pallas_call(
        functools.partial(_gla_sample_kernel, n_b=n_b),
        grid=(1,),
        in_specs=[pl.BlockSpec((MS, kw), lambda i: (0, C_QG // kw)),
                  pl.BlockSpec((MS, kw), lambda i: (0, C_KG // kw)),
                  pl.BlockSpec((MS, vw), lambda i: (0, C_VG // vw)),
                  pl.BlockSpec((MS, vw), lambda i: (0, C_OG // vw)),
                  pl.BlockSpec((MS, LANE), lambda i: (0, C_SM // LANE)),
                  pl.BlockSpec((LANE, kw), lambda i: (0, 0)),
                  pl.BlockSpec((1, kw), lambda i: (0, 0)),
                  pl.BlockSpec(state.shape, lambda i: (0, 0, 0, 0)),
                  pl.BlockSpec((1, vw), lambda i: (0, (H_NSA + H_SB) * D_HEAD // vw))],
        out_specs=[pl.BlockSpec((MS, vw), lambda i: (0, 0)),
                   pl.BlockSpec(state.shape, lambda i: (0, 0, 0, 0))],
        out_shape=[jax.ShapeDtypeStruct((MS, vw), F32), jax.ShapeDtypeStruct(state.shape, F32)],
        compiler_params=_cp(("arbitrary",)),
        name="gla_sample",
    )(zs, zs, zs, zs, zs, _wa2_padded(gla_wa2), gla_ba.reshape(1, kw), state, g_mix)


def _reorder_w_in(w_in):
    o = np.cumsum([0, 1024, 24, 1536, 1536, 256, 256, 512, 16, 512])
    seg = [w_in[..., o[i]:o[i + 1]] for i in range(9)]
    q_n, gate, kv, sb, qg, kg, vg, alr, og = seg
    pad = jnp.zeros(w_in.shape[:-1] + (N_Z - C_SM - 40,), w_in.dtype)
    return jnp.concatenate([q_n, kv, sb, qg, kg, vg, og, gate, alr, pad], axis=-1).astype(BF16)


def kernel(x_prompt, x_sample, cache_nsa, cache_sb, state_win, state_gla, page_table, c_prompt, c_sample, norm1, norm2, w_ada, b_ada, w_in, gla_wa2, gla_ba, cmp_pe, cmp_w1, cmp_w2, g_mix, w_out, ffn_w1, ffn_w3, ffn_w2, final_norm):
    depth = w_in.shape[0]
    bp, t_len, d = x_prompt.shape
    bs = x_sample.shape[0]
    n_pages = page_table.shape[1]
    page = cache_nsa.shape[2]
    past = n_pages * page
    wbuf = state_win.shape[2]
    mp = bp * t_len
    nsa_w = 4 * G_NSA * D_HEAD
    sb_w = 2 * H_SB * D_HEAD
    win_w = 2 * G_NSA * D_HEAD
    assert bs <= MS and x_sample.shape[1] == 1 and wbuf == WINDOW and t_len >= wbuf

    c_all = jnp.zeros((MS, d), F32).at[:bs].set(c_sample).at[bs:bs + bp].set(c_prompt)
    mod = matmul_bias(jnp.broadcast_to(c_all, (depth, MS, d)), w_ada, b_ada[:, None, :],
                      tm=MS, tn=1024, silu_a=True)
    w_in_r = _reorder_w_in(w_in)

    xp = x_prompt.reshape(mp, d)
    xs = jnp.zeros((MS, d), F32).at[:bs].set(x_sample[:, 0])
    outs = {k: [] for k in ("nsa_s", "sb_s", "win_p", "win_s", "gla_p", "gla_s")}
    rows_p = (jnp.zeros((depth, mp * ROW_SLOTS, D_HEAD), F32),) * 2

    for l in range(depth):
        def mod_p(k):
            return mod[l, bs:bs + bp, None, k * d:(k + 1) * d]

        def mod_s(k):
            return mod[l, None, :, k * d:(k + 1) * d]

        gm = g_mix[l][None, :]
        h = norm_mod(xp, norm1[l][None, :], mod_p(1), mod_p(0), tm=512)
        z, rows_nsa, rows_sb = project_in(h, w_in_r, l, rows_p, tm=1024, emit_rows=True)
        rows_p = (rows_nsa, rows_sb)
        z3 = z.reshape(bp, t_len, N_Z)
        seg = z3[..., C_KV:C_KV + 4 * D_HEAD].reshape(bp, t_len // CMP_STRIDE, CMP_STRIDE, 4, D_HEAD)
        seg = jnp.transpose(seg, (0, 3, 1, 2, 4)).reshape(bp, 4, t_len // CMP_STRIDE, CMP_STRIDE * D_HEAD)
        cmp_kv = compress_segments(seg, cmp_w1[l], cmp_pe[l], cmp_w2[l])
        o_nsa = nsa_prompt(z3, cmp_kv, gm)
        o_sb = sb_prompt(z3, gm)
        o_gla, st_t = gla_prompt(z3, gla_wa2[l], gla_ba[l], gm)
        xp = matmul_resid([o_nsa.reshape(mp, -1), o_sb.reshape(mp, -1), o_gla.reshape(mp, -1)],
                          w_out, l, xp, mod_p(2), tm=2048, tn=512)
        h = norm_mod(xp, norm2[l][None, :], mod_p(4), mod_p(3), tm=512)
        hid = matmul_swiglu(h, ffn_w1, ffn_w3, l, tm=2048, tn=512)
        xp, seg_s = ffn_down_gather(hid, ffn_w2, l, xp, mod_p(5), cache_nsa, page_table,
                                    tm=mp // bs, tn=256)
        outs["win_p"].append(z3[:, t_len - wbuf:, C_KV + nsa_w:C_KV + nsa_w + win_w]
                             .reshape(bp, wbuf, 2, G_NSA, D_HEAD))
        outs["gla_p"].append(jnp.swapaxes(st_t, 2, 3))

        hs = norm_mod(xs, norm1[l][None, :], mod_s(1), mod_s(0), tm=MS)
        zs = project_in(hs, w_in_r, l, tm=MS, emit_rows=False)
        cmp_s = compress_segments(seg_s, cmp_w1[l], cmp_pe[l], cmp_w2[l])
        q8 = zs[:bs, C_Q:C_Q + H_NSA * D_HEAD].reshape(bs, H_NSA, D_HEAD)
        o_cmp, idx = nsa_sample_select(q8, cmp_s, pos=past)
        idx_flat = idx[:, :G_NSA, :N_SELECT].reshape(-1)
        o_nsa_s = nsa_sample_attend(zs, q8, o_cmp, idx_flat, cache_nsa, state_win, page_table,
                                    gm[0, :H_NSA * D_HEAD].reshape(H_NSA, D_HEAD), l, pos=past)
        o_sb_s = sb_sample(zs, cache_sb, page_table,
                           gm[0, H_NSA * D_HEAD:(H_NSA + H_SB) * D_HEAD].reshape(H_SB, D_HEAD), l)
        o_gla_s, st_new = gla_sample(zs, gla_wa2[l], gla_ba[l], state_gla[l], gm)

        def pad_rows(a):
            return jnp.zeros((MS, a.shape[1]), BF16).at[:bs].set(a.astype(BF16))

        xs = matmul_resid([pad_rows(o_nsa_s.reshape(bs, -1)), pad_rows(o_sb_s.reshape(bs, -1)),
                           o_gla_s.astype(BF16)], w_out, l, xs, mod_s(2), tm=MS, tn=512)
        hs = norm_mod(xs, norm2[l][None, :], mod_s(4), mod_s(3), tm=MS)
        hid_s = matmul_swiglu(hs, ffn_w1, ffn_w3, l, tm=MS, tn=512)
        xs = matmul_resid([hid_s], ffn_w2, l, xs, mod_s(5), tm=MS, tn=512, tk=2816)
        outs["nsa_s"].append(zs[:bs, C_KV:C_KV + nsa_w].reshape(bs, 1, 4, G_NSA, D_HEAD))
        outs["sb_s"].append(zs[:bs, C_SB + H_SB * D_HEAD:C_SB + H_SB * D_HEAD + sb_w].reshape(bs, 1, 2, H_SB, D_HEAD))
        win_new = zs[:bs, C_KV + nsa_w:C_KV + nsa_w + win_w].reshape(bs, 1, 2, G_NSA, D_HEAD)
        outs["win_s"].append(jnp.concatenate([state_win[l][:, 1:], win_new], axis=1))
        outs["gla_s"].append(st_new.astype(state_gla.dtype))

    y_prompt = rmsnorm_rows(xp, final_norm[None, :], tm=512).reshape(bp, t_len, d)
    y_sample = rmsnorm_rows(xs, final_norm[None, :], tm=MS)[:bs].reshape(bs, 1, d)
    nsa_p = rows_p[0].reshape(depth, bp, t_len, 4, G_NSA, D_HEAD)
    sb_p = rows_p[1].reshape(depth, bp, t_len, 2, H_SB, D_HEAD)
    return (y_prompt, y_sample, nsa_p, jnp.stack(outs["nsa_s"]), sb_p,
            jnp.stack(outs["sb_s"]), jnp.stack(outs["win_p"]), jnp.stack(outs["win_s"]),
            jnp.stack(outs["gla_p"]), jnp.stack(outs["gla_s"]))
```

```python
import functools

import numpy as np
import jax
import jax.numpy as jnp
from jax import lax
from jax.experimental import pallas as pl
from jax.experimental.pallas import tpu as pltpu

F32 = jnp.float32
BF16 = jnp.bfloat16

EPS = 1e-6
D_HEAD = 128
H_NSA = 8
G_NSA = 2
HPG = 4
H_SB = 4
H_GLA = 4
GLA_DK = 64
GLA_DV = 128
GLA_RANK = 16
GLA_TAU = 16.0
GLA_CHUNK = 128
CMP_BLOCK = 32
CMP_STRIDE = 16
SEL_BLOCK = 64
N_SELECT = 16
N_LOCAL = 2
WINDOW = 512
FORCE_SCORE = 1e6
SCALE = D_HEAD ** -0.5
LANE = 128
MS = 16

C_Q = 0
C_KV = 1024
C_SB = 2560
C_QG = 4096
C_KG = 4352
C_VG = 4608
C_OG = 5120
C_SM = 5632
N_Z = 6144
SM_GATE = 0
SM_ALR = 24

NEG = -1e30
PEN = -(2.0 ** 100)
PEN_TEST = -(2.0 ** 90)

VMEM_BIG = 56 * 1024 * 1024


def _cp(sem, vmem=None):
    return pltpu.CompilerParams(dimension_semantics=sem, vmem_limit_bytes=vmem)


def _log_sigmoid(x):
    return jnp.minimum(x, 0.0) - jnp.log(1.0 + jnp.exp(-jnp.abs(x)))


LOG2E = 1.4426950408889634


def _log2_sigmoid(x2):
    return jnp.minimum(x2, 0.0) - jnp.log2(1.0 + jnp.exp2(-jnp.abs(x2)))


def _silu(x):
    return x * (1.0 / (1.0 + jnp.exp(-x)))


def _split2(x):
    hi = x.astype(BF16)
    lo = (x - hi.astype(F32)).astype(BF16)
    return hi, lo


def _split3(x):
    hi = x.astype(BF16)
    r = x - hi.astype(F32)
    mid = r.astype(BF16)
    lo = (r - mid.astype(F32)).astype(BF16)
    return hi, mid, lo


def _dot(a, b):
    return jnp.dot(a, b, preferred_element_type=F32)


def _dot_nt(a, b):
    return lax.dot_general(a, b, (((1,), (1,)), ((), ())), preferred_element_type=F32)


def _dot_tn(a, b):
    return lax.dot_general(a, b, (((0,), (0,)), ((), ())), preferred_element_type=F32)


def _mm_plain_kernel(a_ref, w_ref, b_ref, o_ref, *, silu_a):
    a = a_ref[...]
    if silu_a:
        a = _silu(a.astype(F32))
    o_ref[...] = _dot(a.astype(BF16), w_ref[...].astype(BF16)) + b_ref[...]


def matmul_bias(a, w, b, *, tm, tn, silu_a=False):
    nl, m, k = a.shape
    n = w.shape[2]
    return pl.pallas_call(
        functools.partial(_mm_plain_kernel, silu_a=silu_a),
        grid=(nl, m // tm, n // tn),
        in_specs=[pl.BlockSpec((None, tm, k), lambda l, i, j: (l, i, 0)),
                  pl.BlockSpec((None, k, tn), lambda l, i, j: (l, 0, j)),
                  pl.BlockSpec((None, 1, tn), lambda l, i, j: (l, 0, j))],
        out_specs=pl.BlockSpec((None, tm, tn), lambda l, i, j: (l, i, j)),
        out_shape=jax.ShapeDtypeStruct((nl, m, n), F32),
        compiler_params=_cp(("arbitrary", "arbitrary", "arbitrary"), VMEM_BIG),
        name="matmul_bias",
    )(a, w, b)


PROJ_TN = 1024
ROW_SLOTS = 8


def _project_in_kernel(*refs, emit_rows, tm):
    a_ref, w_ref = refs[:2]
    acc = _dot(a_ref[...], w_ref[...])
    if not emit_rows:
        refs[2][...] = acc
        return
    z_ref, nsa_ref, sb_ref = refs[-3:]
    z_ref[...] = acc
    j = pl.program_id(1)

    def write_rows(rows_ref):
        for s in range(ROW_SLOTS):
            rows_ref[pl.ds(s, tm, stride=ROW_SLOTS), :] = acc[:, s * D_HEAD:(s + 1) * D_HEAD]

    @pl.when(j == C_KV // PROJ_TN)
    def _():
        write_rows(nsa_ref)

    @pl.when(j == (C_SB + H_SB * D_HEAD) // PROJ_TN)
    def _():
        write_rows(sb_ref)


def project_in(a, w_all, layer, rows_prev=None, *, tm, emit_rows):
    m, k = a.shape
    depth, _, n = w_all.shape
    in_specs = [pl.BlockSpec((tm, k), lambda i, j: (i, 0)),
                pl.BlockSpec((None, k, PROJ_TN), lambda i, j: (layer, 0, j))]
    z_spec = pl.BlockSpec((tm, PROJ_TN), lambda i, j: (i, j))
    z_shape = jax.ShapeDtypeStruct((m, n), F32)
    if not emit_rows:
        return pl.pallas_call(
            functools.partial(_project_in_kernel, emit_rows=False, tm=tm),
            grid=(m // tm, n // PROJ_TN), in_specs=in_specs, out_specs=z_spec, out_shape=z_shape,
            compiler_params=_cp(("arbitrary", "arbitrary"), VMEM_BIG), name="project_in_small",
        )(a, w_all)
    rows_shape = jax.ShapeDtypeStruct((depth, m * ROW_SLOTS, D_HEAD), F32)
    rows_spec = pl.BlockSpec((None, tm * ROW_SLOTS, D_HEAD), lambda i, j: (layer, i, 0))
    return pl.pallas_call(
        functools.partial(_project_in_kernel, emit_rows=True, tm=tm),
        grid=(m // tm, n // PROJ_TN), in_specs=in_specs + [pl.BlockSpec(memory_space=pl.ANY)] * 2,
        out_specs=[z_spec, rows_spec, rows_spec], out_shape=[z_shape, rows_shape, rows_shape],
        input_output_aliases={2: 1, 3: 2},
        compiler_params=_cp(("arbitrary", "arbitrary"), VMEM_BIG), name="project_in",
    )(a, w_all, *rows_prev)


def _mm_swiglu_kernel(a_ref, w1_ref, w3_ref, o_ref):
    a = a_ref[...]
    u = _dot(a, w1_ref[...].astype(BF16))
    v = _dot(a, w3_ref[...].astype(BF16))
    o_ref[...] = (_silu(u) * v).astype(o_ref.dtype)


def matmul_swiglu(a, w1, w3, layer, *, tm, tn):
    m, k = a.shape
    n = w1.shape[2]
    return pl.pallas_call(
        _mm_swiglu_kernel,
        grid=(m // tm, n // tn),
        in_specs=[pl.BlockSpec((tm, k), lambda i, j: (i, 0)),
                  pl.BlockSpec((None, k, tn), lambda i, j: (layer, 0, j)),
                  pl.BlockSpec((None, k, tn), lambda i, j: (layer, 0, j))],
        out_specs=pl.BlockSpec((tm, tn), lambda i, j: (i, j)),
        out_shape=jax.ShapeDtypeStruct((m, n), BF16),
        compiler_params=_cp(("arbitrary", "arbitrary"), VMEM_BIG),
        name="matmul_swiglu",
    )(a, w1, w3)


def _mm_resid_kernel(*refs, a_widths, nk):
    n_a = len(a_widths)
    a_refs = refs[:n_a]
    w_ref, x_ref, g_ref, o_ref = refs[n_a:n_a + 4]
    part = None
    off = 0
    for a_ref, kw in zip(a_refs, a_widths):
        d = _dot(a_ref[...], w_ref[off:off + kw, :].astype(BF16))
        part = d if part is None else part + d
        off += kw
    if nk == 1:
        o_ref[...] = x_ref[...] + g_ref[...] * part
        return
    acc_ref = refs[n_a + 4]
    k = pl.program_id(2)

    @pl.when(k == 0)
    def _():
        acc_ref[...] = part

    @pl.when(k > 0)
    def _():
        acc_ref[...] += part

    @pl.when(k == nk - 1)
    def _():
        o_ref[...] = x_ref[...] + g_ref[...] * acc_ref[...]


def matmul_resid(a_list, w, layer, x, gate, *, tm, tn, tk=None):
    m, n = x.shape
    k_total = w.shape[1]
    a_widths = tuple(a.shape[1] for a in a_list)
    if tk is None:
        tk = k_total
    nk = k_total // tk
    assert nk == 1 or len(a_list) == 1
    g_cnt, g_rows, _ = gate.shape
    rows_per_gate = m // g_cnt
    if nk == 1:
        a_specs = [pl.BlockSpec((tm, kw), lambda i, j, k: (i, 0)) for kw in a_widths]
        kernel_widths = a_widths
    else:
        a_specs = [pl.BlockSpec((tm, tk), lambda i, j, k: (i, k))]
        kernel_widths = (tk,)
    g_blk = 1 if g_rows == 1 else tm
    return pl.pallas_call(
        functools.partial(_mm_resid_kernel, a_widths=kernel_widths, nk=nk),
        grid=(m // tm, n // tn, nk),
        in_specs=a_specs + [
            pl.BlockSpec((None, tk, tn), lambda i, j, k: (layer, k, j)),
            pl.BlockSpec((tm, tn), lambda i, j, k: (i, j)),
            pl.BlockSpec((None, g_blk, tn), lambda i, j, k: ((i * tm) // rows_per_gate, 0, j)),
        ],
        out_specs=pl.BlockSpec((tm, tn), lambda i, j, k: (i, j)),
        out_shape=jax.ShapeDtypeStruct((m, n), F32),
        scratch_shapes=[pltpu.VMEM((tm, tn), F32)] if nk > 1 else [],
        compiler_params=_cp(("arbitrary", "arbitrary", "arbitrary"), VMEM_BIG),
        name="matmul_resid",
    )(*a_list, w, x, gate)


def _norm_mod_kernel(x_ref, g_ref, sc_ref, sh_ref, o_ref):
    x = x_ref[...]
    y = x * lax.rsqrt(jnp.mean(x * x, axis=-1, keepdims=True) + EPS) * g_ref[...]
    o_ref[...] = (y * (1.0 + sc_ref[...]) + sh_ref[...]).astype(o_ref.dtype)


def norm_mod(x, g, sc, sh, *, tm):
    m, d = x.shape
    g_cnt, g_rows, _ = sc.shape
    rows_per_gate = m // g_cnt
    g_blk = 1 if g_rows == 1 else tm
    mod_spec = pl.BlockSpec((None, g_blk, d), lambda i: ((i * tm) // rows_per_gate, 0, 0))
    return pl.pallas_call(
        _norm_mod_kernel,
        grid=(m // tm,),
        in_specs=[pl.BlockSpec((tm, d), lambda i: (i, 0)),
                  pl.BlockSpec((1, d), lambda i: (0, 0)), mod_spec, mod_spec],
        out_specs=pl.BlockSpec((tm, d), lambda i: (i, 0)),
        out_shape=jax.ShapeDtypeStruct((m, d), BF16),
        compiler_params=_cp(("arbitrary",)),
        name="norm_mod",
    )(x, g, sc, sh)


def _rmsnorm_kernel(x_ref, g_ref, o_ref):
    x = x_ref[...]
    o_ref[...] = x * lax.rsqrt(jnp.mean(x * x, axis=-1, keepdims=True) + EPS) * g_ref[...]


def rmsnorm_rows(x, g, *, tm):
    m, d = x.shape
    return pl.pallas_call(
        _rmsnorm_kernel,
        grid=(m // tm,),
        in_specs=[pl.BlockSpec((tm, d), lambda i: (i, 0)), pl.BlockSpec((1, d), lambda i: (0, 0))],
        out_specs=pl.BlockSpec((tm, d), lambda i: (i, 0)),
        out_shape=jax.ShapeDtypeStruct((m, d), F32),
        compiler_params=_cp(("arbitrary",)),
        name="rmsnorm_rows",
    )(x, g)


def _compress_kernel(s_ref, w1_ref, pe_ref, w2_ref, o_ref, *, token_major):
    if token_major:
        n_seg = s_ref.shape[0] // CMP_STRIDE
        s = jnp.concatenate([s_ref[pl.ds(p, n_seg, stride=CMP_STRIDE), :] for p in range(CMP_STRIDE)],
                            axis=1).astype(BF16)
    else:
        n_seg = s_ref.shape[0]
        s = s_ref[...].astype(BF16)
    half = CMP_STRIDE * D_HEAD
    w_lo = w1_ref[0].astype(BF16)
    w_hi = w1_ref[1].astype(BF16)
    both = _dot(s, jnp.concatenate([w_lo, w_hi], axis=1))
    lo = both[:, :D_HEAD]
    hi = both[:, D_HEAD:]
    pe = pe_ref[...].astype(BF16)
    bias = _dot(pe[:, :half], w_lo) + _dot(pe[:, half:], w_hi)
    pre = lo + pltpu.roll(hi, n_seg - 1, 0) + bias[0:1, :]
    out = _dot(_silu(pre).astype(BF16), w2_ref[...].astype(BF16))
    row = lax.broadcasted_iota(jnp.int32, out.shape, 0)
    o_ref[...] = jnp.where(row < n_seg - 1, out, 0.0)


def compress_segments(seg, w1, pe, w2, *, token_major=False):
    half = CMP_STRIDE * D_HEAD
    if token_major:
        b, t_len, _ = seg.shape
        n_seg = t_len // CMP_STRIDE
        seg_spec = pl.BlockSpec((None, t_len, D_HEAD), lambda i, j: (i, 0, C_KV // D_HEAD + j))
    else:
        b, _, n_seg, _ = seg.shape
        seg_spec = pl.BlockSpec((None, None, n_seg, half), lambda i, j: (i, j, 0, 0))
    w1r = w1.reshape(2, 2, half, D_HEAD)
    pe8 = jnp.broadcast_to(pe.reshape(2, 1, 2 * half), (2, 8, 2 * half))
    return pl.pallas_call(
        functools.partial(_compress_kernel, token_major=token_major),
        grid=(b, 4),
        in_specs=[seg_spec,
                  pl.BlockSpec((None, 2, half, D_HEAD), lambda i, j: (j // 2, 0, 0, 0)),
                  pl.BlockSpec((None, 8, 2 * half), lambda i, j: (j // 2, 0, 0)),
                  pl.BlockSpec((None, D_HEAD, D_HEAD), lambda i, j: (j // 2, 0, 0))],
        out_specs=pl.BlockSpec((None, None, n_seg, D_HEAD), lambda i, j: (i, j, 0, 0)),
        out_shape=jax.ShapeDtypeStruct((b, 4, n_seg, D_HEAD), F32),
        compiler_params=_cp(("arbitrary", "arbitrary"), VMEM_BIG),
        name="compress_segments",
    )(seg, w1r, pe8, w2)


def _slope_column(g, rows_per_head):
    n = HPG * rows_per_head
    h = lax.broadcasted_iota(jnp.int32, (n, 1), 0) // rows_per_head + g * HPG
    out = jnp.zeros((n, 1), F32)
    for hh in range(H_NSA):
        out = jnp.where(h == hh, 2.0 ** (-8.0 * (hh + 1) / H_NSA), out)
    return out


def _masked_softmax_rows(s, valid):
    s = jnp.where(valid, s, NEG)
    m = jnp.max(s, axis=-1, keepdims=True)
    e = jnp.where(valid, jnp.exp(s - m), 0.0)
    d = jnp.sum(e, axis=-1, keepdims=True)
    return e / jnp.where(d > 0.0, d, 1.0)


def _select_blocks(score, n_blocks):
    lane = lax.broadcasted_iota(jnp.int32, score.shape, 1)
    rank = jnp.zeros(score.shape, jnp.int32)
    for i in range(n_blocks):
        c = score[:, i:i + 1]
        ahead = (c > score) | ((c == score) & (lane > i))
        rank = rank + ahead.astype(jnp.int32)
    return (rank < N_SELECT) & (score > -jnp.inf)


def _slc_matrix(n_cmp_rows, nc, n_lanes):
    i = np.arange(n_cmp_rows)[:, None]
    j = np.arange(n_lanes)[None, :]
    m = (i >= 4 * j - 1) & (i <= 4 * j + 3) & (i < nc)
    return jnp.asarray(m.astype(np.float32), dtype=BF16)


def _head_rms(o):
    return o * lax.rsqrt(jnp.mean(o * o, axis=-1, keepdims=True) + EPS)


def _nsa_prompt_kernel(q_ref, sm_ref, selk_ref, selv_ref, wink_ref, winv_ref, cmpk_ref, cmpv_ref,
                       kaug_ref, caug_ref, mslc_ref, wbias_ref, gm_ref, o_ref, *, qb, tk):
    g = pl.program_id(1)
    qi = pl.program_id(2)
    q0 = qi * qb
    rows = HPG * qb
    n_seg = cmpk_ref.shape[0]
    nc = n_seg - 1
    t_len = selk_ref.shape[0]
    ns = t_len // SEL_BLOCK

    q = q_ref[...] * SCALE
    qs = jnp.concatenate([q[:, h * D_HEAD:(h + 1) * D_HEAD] for h in range(HPG)], axis=0)
    qs_b = qs.astype(BF16)
    slope = _slope_column(g, qb)
    rpos = q0 + lax.broadcasted_iota(jnp.int32, (rows, 1), 0) % qb
    lane = lax.broadcasted_iota(jnp.int32, (rows, LANE), 1)
    ones_col = jnp.where(lax.broadcasted_iota(jnp.int32, (tk, LANE), 1) == 0, 1.0, 0.0).astype(BF16)

    qa_c = jnp.concatenate([qs_b, jnp.where(lane == 0, slope * CMP_STRIDE, 0.0).astype(BF16)], axis=1)
    ka_c = jnp.concatenate([cmpk_ref[...].astype(BF16), caug_ref[...]], axis=1)
    ci = lax.broadcasted_iota(jnp.int32, (1, n_seg), 1)
    c_valid = (ci * CMP_STRIDE + (CMP_BLOCK - 1) <= rpos) & (ci < nc)
    p_cmp = _masked_softmax_rows(_dot_nt(qa_c, ka_c), c_valid)
    o_cmp = _dot(p_cmp.astype(BF16), cmpv_ref[...].astype(BF16))

    imp = p_cmp[0:qb]
    for h in range(1, HPG):
        imp = imp + p_cmp[h * qb:(h + 1) * qb]
    p_slc = None
    for part in _split3(imp):
        d = _dot_nt(mslc_ref[...], part)
        p_slc = d if p_slc is None else p_slc + d
    p_slc = p_slc[0:SEL_BLOCK]
    blk = lax.broadcasted_iota(jnp.int32, (SEL_BLOCK, qb), 0)
    cur = (q0 + lax.broadcasted_iota(jnp.int32, (1, qb), 1)) // SEL_BLOCK
    forced = (blk == 0) | (blk > cur - N_LOCAL)
    score = jnp.where(forced, FORCE_SCORE, p_slc)
    score = jnp.where((blk <= cur) & (blk < ns), score, -jnp.inf)
    sub = 8
    pieces = [score[v * sub:(v + 1) * sub] for v in range(SEL_BLOCK // sub)]
    ranks = [jnp.zeros((sub, qb), jnp.int32) for _ in pieces]
    sub_row = lax.broadcasted_iota(jnp.int32, (sub, qb), 0)
    for i in range(ns):
        c = pieces[i // sub][i % sub:i % sub + 1, :]
        for v, piece in enumerate(pieces):
            if v < i // sub:
                ahead = (c > piece).astype(jnp.int32)
            elif v > i // sub:
                ahead = (c >= piece).astype(jnp.int32)
            else:
                ahead = jnp.where(sub_row > i % sub, (c >= piece).astype(jnp.int32), (c > piece).astype(jnp.int32))
            ranks[v] = ranks[v] + ahead
    rank = jnp.concatenate(ranks, axis=0)
    sel = (rank < N_SELECT) & (score > -jnp.inf)
    pen_t = jnp.where(sel, 0.0, PEN)
    pen = jnp.transpose(jnp.concatenate([pen_t, jnp.zeros((LANE - SEL_BLOCK, qb), F32)], axis=0))
    pen4 = jnp.concatenate([pen] * HPG, axis=0)
    q_aug = jnp.where(lane < SEL_BLOCK, pen4,
                      jnp.where(lane == SEL_BLOCK, slope * SEL_BLOCK,
                                jnp.where(lane == SEL_BLOCK + 1, slope, 0.0)))
    qa = jnp.concatenate([qs_b, q_aug.astype(BF16)], axis=1)

    def sel_scores(k0):
        ka = jnp.concatenate([selk_ref[pl.ds(k0, tk), :].astype(BF16), kaug_ref[pl.ds(k0, tk), :]], axis=1)
        return _dot_nt(qa, ka)

    def sel_update(st, k0, m_run, acc):
        va = jnp.concatenate([selv_ref[pl.ds(k0, tk), :].astype(BF16), ones_col], axis=1)
        m_new = jnp.maximum(m_run, jnp.max(st, axis=-1, keepdims=True))
        p = jnp.exp(st - m_new).astype(BF16)
        return m_new, jnp.exp(m_run - m_new) * acc + _dot(p, va)

    def sel_tile(kt, carry):
        m_run, acc, st = carry
        k0 = pl.multiple_of(kt * tk, tk)
        st_next = sel_scores(k0 + tk)
        m_new, acc = sel_update(st, k0, m_run, acc)
        return m_new, acc, st_next

    n_full = q0 // tk
    init = (jnp.full((rows, 1), NEG, F32), jnp.zeros((rows, 2 * LANE), F32), sel_scores(0))
    m_run, acc_sel, st = lax.fori_loop(0, n_full, sel_tile, init)
    k0 = pl.multiple_of(n_full * tk, tk)
    st = jnp.where(k0 + lax.broadcasted_iota(jnp.int32, (1, tk), 1) <= rpos, st, NEG)
    _, acc_sel = sel_update(st, k0, m_run, acc_sel)
    o_sel = acc_sel[:, 0:D_HEAD] / acc_sel[:, D_HEAD:D_HEAD + 1]

    wk = WINDOW + qb
    ks = pl.multiple_of(jnp.maximum(q0 - WINDOW, 0), qb)
    s = _dot_nt(qs_b, wink_ref[pl.ds(ks, wk), :].astype(BF16)) + wbias_ref[...]
    m = jnp.max(s, axis=-1, keepdims=True)
    p = jnp.exp(s - m).astype(BF16)
    ones_w = jnp.where(lax.broadcasted_iota(jnp.int32, (wk, LANE), 1) == 0, 1.0, 0.0).astype(BF16)
    acc_win = _dot(p, jnp.concatenate([winv_ref[pl.ds(ks, wk), :].astype(BF16), ones_w], axis=1))
    o_win = acc_win[:, 0:D_HEAD] / acc_win[:, D_HEAD:D_HEAD + 1]

    gates = 1.0 / (1.0 + jnp.exp(-sm_ref[...]))

    def gate_col(branch):
        cols = []
        for h in range(HPG):
            c0 = SM_GATE + h * 3 + branch
            c1 = SM_GATE + (HPG + h) * 3 + branch
            cols.append(jnp.where(g == 0, gates[:, c0:c0 + 1], gates[:, c1:c1 + 1]))
        return jnp.concatenate(cols, axis=0)

    o = gate_col(0) * o_cmp + gate_col(1) * o_sel + gate_col(2) * o_win
    o = _head_rms(o)
    for h in range(HPG):
        o_ref[:, h * D_HEAD:(h + 1) * D_HEAD] = (
            o[h * qb:(h + 1) * qb] * gm_ref[:, h * D_HEAD:(h + 1) * D_HEAD]).astype(o_ref.dtype)


def _key_aug(t_len):
    pos = np.arange(t_len)[:, None]
    lane = np.arange(LANE)[None, :]
    a = np.where(lane < SEL_BLOCK, (pos // SEL_BLOCK == lane).astype(np.float32),
                 np.where(lane == SEL_BLOCK, (pos // SEL_BLOCK).astype(np.float32),
                          np.where(lane == SEL_BLOCK + 1, (pos % SEL_BLOCK).astype(np.float32), 0.0)))
    return jnp.asarray(a, dtype=BF16)


def _cmp_aug(n_seg):
    a = np.zeros((n_seg, LANE), np.float32)
    a[:, 0] = np.arange(n_seg)
    return jnp.asarray(a, dtype=BF16)


def _window_bias(qb):
    n_pat = WINDOW // qb + 1
    r = np.arange(HPG * qb)
    head = r // qb
    dist = (np.arange(n_pat)[:, None, None] * qb + (r % qb)[None, :, None]
            - np.arange(WINDOW + qb)[None, None, :])
    valid = (dist >= 0) & (dist <= WINDOW)
    out = np.empty((G_NSA,) + dist.shape, np.float32)
    for g in range(G_NSA):
        slope = 2.0 ** (-8.0 * (g * HPG + head + 1) / H_NSA)
        out[g] = np.where(valid, -slope[None, :, None] * dist, NEG)
    return jnp.asarray(out)


def nsa_prompt(z, cmp_kv, g_mix, *, qb=256, tk=512):
    b, t_len, _ = z.shape
    assert t_len // SEL_BLOCK <= SEL_BLOCK and t_len % tk == 0 and tk % qb == 0 and WINDOW % qb == 0
    n_seg = cmp_kv.shape[2]
    assert n_seg <= 256
    gw = HPG * D_HEAD
    kv_blk = C_KV // D_HEAD
    n_pat = WINDOW // qb
    wbias = _window_bias(qb)

    def kv_spec(slot):
        return pl.BlockSpec((None, t_len, D_HEAD), lambda i, g, q: (i, 0, kv_blk + 2 * slot + g))

    return pl.pallas_call(
        functools.partial(_nsa_prompt_kernel, qb=qb, tk=tk),
        grid=(b, G_NSA, t_len // qb),
        in_specs=[pl.BlockSpec((None, qb, gw), lambda i, g, q: (i, q, g)),
                  pl.BlockSpec((None, qb, LANE), lambda i, g, q: (i, q, C_SM // LANE)),
                  kv_spec(2), kv_spec(3), kv_spec(4), kv_spec(5),
                  pl.BlockSpec((None, None, n_seg, D_HEAD), lambda i, g, q: (i, g, 0, 0)),
                  pl.BlockSpec((None, None, n_seg, D_HEAD), lambda i, g, q: (i, 2 + g, 0, 0)),
                  pl.BlockSpec((t_len, LANE), lambda i, g, q: (0, 0)),
                  pl.BlockSpec((n_seg, LANE), lambda i, g, q: (0, 0)),
                  pl.BlockSpec((LANE, n_seg), lambda i, g, q: (0, 0)),
                  pl.BlockSpec((None, None, HPG * qb, WINDOW + qb),
                               lambda i, g, q: (g, jnp.minimum(q, n_pat), 0, 0)),
                  pl.BlockSpec((1, gw), lambda i, g, q: (0, g))],
        out_specs=pl.BlockSpec((None, qb, gw), lambda i, g, q: (i, q, g)),
        out_shape=jax.ShapeDtypeStruct((b, t_len, H_NSA * D_HEAD), BF16),
        compiler_params=_cp(("arbitrary", "arbitrary", "arbitrary"), VMEM_BIG),
        name="nsa_prompt",
    )(z, z, z, z, z, z, cmp_kv, cmp_kv, _key_aug(t_len), _cmp_aug(n_seg),
      jnp.transpose(_slc_matrix(n_seg, n_seg - 1, LANE)), wbias, g_mix)


def _sb_prompt_kernel(q_ref, k_ref, v_ref, gm_ref, o_ref, *, qb, tk):
    qi = pl.program_id(2)
    q0 = qi * qb
    q = (q_ref[...] * (SCALE * LOG2E)).astype(BF16)
    rpos = q0 + lax.broadcasted_iota(jnp.int32, (qb, 1), 0)
    row = lax.broadcasted_iota(jnp.int32, (2 * tk, tk), 0) % tk
    tri2 = jnp.where(row > lax.broadcasted_iota(jnp.int32, (2 * tk, tk), 1), 1.0, 0.0).astype(BF16)
    n_diag = qb // tk

    def tile(k0, carry, masked):
        run, acc = carry
        z = _dot_nt(q, k_ref[pl.ds(k0, tk), :].astype(BF16))
        lb = _log2_sigmoid(z)
        lr = lb - z
        if masked:
            mask = (k0 + lax.broadcasted_iota(jnp.int32, (1, tk), 1)) < rpos
            lr = jnp.where(mask, lr, 0.0)
        hi, lo = _split2(lr)
        after = _dot(jnp.concatenate([hi, lo], axis=1), tri2) + run
        a = jnp.exp2(lb + after)
        if masked:
            a = jnp.where(mask, a, 0.0)
        acc = acc + _dot(a.astype(BF16), v_ref[pl.ds(k0, tk), :].astype(BF16))
        return after[:, 0:1] + lr[:, 0:1], acc

    carry = (jnp.zeros((qb, 1), F32), jnp.zeros((qb, D_HEAD), F32))
    for it in range(n_diag):
        carry = tile(pl.multiple_of(q0 + (n_diag - 1 - it) * tk, tk), carry, True)

    def full_tiles(it, carry):
        for u in range(n_diag):
            carry = tile(pl.multiple_of(q0 - (it * n_diag + u + 1) * tk, tk), carry, False)
        return carry

    _, acc = lax.fori_loop(0, qi, full_tiles, carry)
    o_ref[...] = (_head_rms(acc) * gm_ref[...]).astype(o_ref.dtype)


def sb_prompt(z, g_mix, *, qb=1024, tk=256):
    b, t_len, _ = z.shape
    assert qb % tk == 0 and t_len % qb == 0
    blk = C_SB // D_HEAD
    gm_blk = (H_NSA * D_HEAD) // D_HEAD
    return pl.pallas_call(
        functools.partial(_sb_prompt_kernel, qb=qb, tk=tk),
        grid=(b, H_SB, t_len // qb),
        in_specs=[pl.BlockSpec((None, qb, D_HEAD), lambda i, h, q: (i, q, blk + h)),
                  pl.BlockSpec((None, t_len, D_HEAD), lambda i, h, q: (i, 0, blk + H_SB + h)),
                  pl.BlockSpec((None, t_len, D_HEAD), lambda i, h, q: (i, 0, blk + 2 * H_SB + h)),
                  pl.BlockSpec((1, D_HEAD), lambda i, h, q: (0, gm_blk + h))],
        out_specs=pl.BlockSpec((None, qb, D_HEAD), lambda i, h, q: (i, q, h)),
        out_shape=jax.ShapeDtypeStruct((b, t_len, H_SB * D_HEAD), BF16),
        compiler_params=_cp(("arbitrary", "arbitrary", "arbitrary")),
        name="sb_prompt",
    )(z, z, z, g_mix)


N_LEVELS = 7


def _gla_level_tables():
    c = GLA_CHUNK
    idx = np.arange(c)
    sums = np.zeros(((N_LEVELS + 1) * c, c), np.float32)
    pair = np.zeros((N_LEVELS + 1, c, c), np.float32)
    sums[:c] = (idx[None, :] <= idx[:, None])
    pair[0] = np.eye(c)
    for lv in range(N_LEVELS):
        mid = ((idx >> (lv + 1)) << (lv + 1)) + (1 << lv)
        upper = idx >= mid
        j = idx[None, :]
        in_up = upper[:, None] & (j >= mid[:, None]) & (j <= idx[:, None])
        in_lo = (~upper)[:, None] & (j > idx[:, None]) & (j <= mid[:, None] - 1)
        sums[(lv + 1) * c:(lv + 2) * c] = in_up | in_lo
        same = (idx[:, None] >> (lv + 1)) == (idx[None, :] >> (lv + 1))
        pair[lv + 1] = same & upper[:, None] & (~upper)[None, :]
    return jnp.asarray(sums, dtype=BF16), jnp.asarray(pair, dtype=F32)


def _gla_prompt_kernel(qg_ref, kg_ref, vg_ref, og_ref, sm_ref, wa2_ref, ba_ref, sums_ref, pair_ref,
                       gm_ref, o_ref, st_ref, state_ref, *, tb):
    nt = pl.program_id(1)
    c = GLA_CHUNK

    @pl.when(nt == 0)
    def _():
        state_ref[...] = jnp.zeros_like(state_ref)

    wa2 = wa2_ref[...].astype(BF16)
    sums = sums_ref[...]

    def chunk(ci, _):
        r0 = pl.multiple_of(ci * c, c)
        rows = pl.ds(r0, c)
        x = _dot(sm_ref[rows, :].astype(BF16), wa2) + ba_ref[...]
        loga = _log_sigmoid(x) * (1.0 / GLA_TAU)
        hi, lo = _split2(loga)
        dsum = _dot(sums, hi) + _dot(sums, lo)
        cb = dsum[0:c]
        q = qg_ref[rows, :] * (GLA_DK ** -0.5)
        k = kg_ref[rows, :]
        v = vg_ref[rows, :].astype(BF16)
        qf = [q.astype(BF16)]
        kf = [k.astype(BF16)]
        for lv in range(N_LEVELS):
            e = jnp.exp(dsum[(lv + 1) * c:(lv + 2) * c])
            qf.append((q * e).astype(BF16))
            kf.append((k * e).astype(BF16))
        c_last = cb[c - 1:c, :]
        q_in = (q * jnp.exp(cb)).astype(BF16)
        k_out = (k * jnp.exp(c_last - cb)).astype(BF16)
        decay = jnp.exp(c_last)
        for h in range(H_GLA):
            ks = slice(h * GLA_DK, (h + 1) * GLA_DK)
            vs = slice(h * GLA_DV, (h + 1) * GLA_DV)
            att = None
            for lv in range(N_LEVELS + 1):
                term = pair_ref[lv] * _dot_nt(qf[lv][:, ks], kf[lv][:, ks])
                att = term if att is None else att + term
            s_t = state_ref[h]
            o_h = _dot(att.astype(BF16), v[:, vs]) + _dot_nt(q_in[:, ks], s_t.astype(BF16))
            state_ref[h] = s_t * decay[:, ks] + _dot_tn(v[:, vs], k_out[:, ks])
            og = og_ref[rows, vs]
            o_ref[rows, vs] = (_head_rms(o_h) * _silu(og) * gm_ref[:, vs]).astype(o_ref.dtype)
        return 0

    lax.fori_loop(0, tb // c, chunk, 0, unroll=4)

    @pl.when(nt == pl.num_programs(1) - 1)
    def _():
        st_ref[...] = state_ref[...]


def _wa2_padded(gla_wa2):
    w = jnp.zeros((LANE, H_GLA * GLA_DK), F32)
    return w.at[SM_ALR:SM_ALR + GLA_RANK].set(gla_wa2)


def gla_prompt(z, gla_wa2, gla_ba, g_mix, *, tb=512):
    b, t_len, _ = z.shape
    kw = H_GLA * GLA_DK
    vw = H_GLA * GLA_DV
    sums, pair = _gla_level_tables()
    n_rows = sums.shape[0]
    return pl.pallas_call(
        functools.partial(_gla_prompt_kernel, tb=tb),
        grid=(b, t_len // tb),
        in_specs=[pl.BlockSpec((None, tb, kw), lambda i, n: (i, n, C_QG // kw)),
                  pl.BlockSpec((None, tb, kw), lambda i, n: (i, n, C_KG // kw)),
                  pl.BlockSpec((None, tb, vw), lambda i, n: (i, n, C_VG // vw)),
                  pl.BlockSpec((None, tb, vw), lambda i, n: (i, n, C_OG // vw)),
                  pl.BlockSpec((None, tb, LANE), lambda i, n: (i, n, C_SM // LANE)),
                  pl.BlockSpec((LANE, kw), lambda i, n: (0, 0)),
                  pl.BlockSpec((1, kw), lambda i, n: (0, 0)),
                  pl.BlockSpec((n_rows, GLA_CHUNK), lambda i, n: (0, 0)),
                  pl.BlockSpec((N_LEVELS + 1, GLA_CHUNK, GLA_CHUNK), lambda i, n: (0, 0, 0)),
                  pl.BlockSpec((1, vw), lambda i, n: (0, (H_NSA + H_SB) * D_HEAD // vw))],
        out_specs=[pl.BlockSpec((None, tb, vw), lambda i, n: (i, n, 0)),
                   pl.BlockSpec((None, H_GLA, GLA_DV, GLA_DK), lambda i, n: (i, 0, 0, 0))],
        out_shape=[jax.ShapeDtypeStruct((b, t_len, vw), BF16),
                   jax.ShapeDtypeStruct((b, H_GLA, GLA_DV, GLA_DK), F32)],
        scratch_shapes=[pltpu.VMEM((H_GLA, GLA_DV, GLA_DK), F32)],
        compiler_params=_cp(("arbitrary", "arbitrary")),
        name="gla_prompt",
    )(z, z, z, z, z, _wa2_padded(gla_wa2), gla_ba.reshape(1, kw), sums, pair, g_mix)


def _gather_pages(page_refs, o_ref):
    slots = 4 * G_NSA
    seg_per_page = page_refs[0].shape[0] // (slots * CMP_STRIDE)
    for pair in range(len(page_refs) // 2):
        r0 = pair * 2 * seg_per_page
        for cg in range(4):
            for p in range(CMP_STRIDE):
                rows = [page_refs[2 * pair + u][pl.ds(p * slots + cg, seg_per_page, stride=CMP_STRIDE * slots), :]
                        for u in range(2)]
                o_ref[cg, r0:r0 + 2 * seg_per_page, p * D_HEAD:(p + 1) * D_HEAD] = (
                    jnp.concatenate(rows, axis=0).astype(o_ref.dtype))


def _rows_view(cache):
    depth, n_pool, page, a, b, d = cache.shape
    return cache.reshape(depth, n_pool, page * a * b, d)


def _ffn_down_gather_kernel(pt_ref, *refs, n_pg):
    page_refs = refs[:n_pg]
    a_ref, w_ref, x_ref, g_ref, o_ref, seg_ref = refs[n_pg:]
    o_ref[...] = x_ref[...] + g_ref[...] * _dot(a_ref[...], w_ref[...].astype(BF16))
    _gather_pages(page_refs, seg_ref)


def ffn_down_gather(a, w, layer, x, gate, cache_nsa, page_table, *, tm, tn):
    m, n = x.shape
    k = w.shape[1]
    page = cache_nsa.shape[2]
    b, n_pages = page_table.shape
    cache = _rows_view(cache_nsa)
    n_i, n_j = m // tm, n // tn
    assert n_i == b and n_pages % n_j == 0 and gate.shape[1] == 1
    n_pg = n_pages // n_j
    seg_per_page = page // CMP_STRIDE
    rows_per_gate = m // gate.shape[0]

    def page_spec(u):
        return pl.BlockSpec((None, None, cache.shape[2], D_HEAD),
                            lambda i, j, pt: (layer, pt[i * n_pages + j * n_pg + u], 0, 0))

    return pl.pallas_call(
        functools.partial(_ffn_down_gather_kernel, n_pg=n_pg),
        grid_spec=pltpu.PrefetchScalarGridSpec(
            num_scalar_prefetch=1,
            grid=(n_i, n_j),
            in_specs=[page_spec(u) for u in range(n_pg)] + [
                pl.BlockSpec((tm, k), lambda i, j, pt: (i, 0), pipeline_mode=pl.Buffered(1)),
                pl.BlockSpec((None, k, tn), lambda i, j, pt: (layer, 0, j)),
                pl.BlockSpec((tm, tn), lambda i, j, pt: (i, j)),
                pl.BlockSpec((None, 1, tn), lambda i, j, pt: ((i * tm) // rows_per_gate, 0, j))],
            out_specs=[pl.BlockSpec((tm, tn), lambda i, j, pt: (i, j)),
                       pl.BlockSpec((None, 4, n_pg * seg_per_page, CMP_STRIDE * D_HEAD),
                                    lambda i, j, pt: (i, 0, j, 0))],
        ),
        out_shape=[jax.ShapeDtypeStruct((m, n), F32),
                   jax.ShapeDtypeStruct((b, 4, n_pages * seg_per_page, CMP_STRIDE * D_HEAD), BF16)],
        compiler_params=_cp(("arbitrary", "arbitrary"), VMEM_BIG),
        name="ffn_down_gather",
    )(page_table.reshape(-1), *([cache] * n_pg), a, w, x, gate)


def _nsa_sample_select_kernel(q_ref, cmp_ref, mslc_ref, ocmp_ref, idx_ref, *, pos, n_blk_lanes):
    n_seg = cmp_ref.shape[1]
    nc = n_seg - 1
    ns = pos // SEL_BLOCK + 1
    cur = pos // SEL_BLOCK
    q = (q_ref[...] * SCALE).astype(BF16)
    row = lax.broadcasted_iota(jnp.int32, (H_NSA, 1), 0)
    slope = _slope_column(0, 1)
    slope = jnp.concatenate([slope, _slope_column(1, 1)], axis=0)
    s = jnp.where(row < HPG, _dot_nt(q, cmp_ref[0].astype(BF16)), _dot_nt(q, cmp_ref[1].astype(BF16)))
    ci = lax.broadcasted_iota(jnp.int32, (1, n_seg), 1)
    c_dist = pos - (ci * CMP_STRIDE + (CMP_BLOCK - 1))
    p = _masked_softmax_rows(s - slope * c_dist.astype(F32), (c_dist >= 0) & (ci < nc))
    pb = p.astype(BF16)
    ocmp_ref[...] = jnp.where(row < HPG, _dot(pb, cmp_ref[2].astype(BF16)), _dot(pb, cmp_ref[3].astype(BF16)))

    imp = jnp.concatenate([jnp.sum(p[g * HPG:(g + 1) * HPG], axis=0, keepdims=True) for g in range(G_NSA)]
                          + [jnp.zeros((H_NSA - G_NSA, n_seg), F32)], axis=0)
    p_slc = None
    for part in _split3(imp):
        d = _dot(part, mslc_ref[...])
        p_slc = d if p_slc is None else p_slc + d
    blk = lax.broadcasted_iota(jnp.int32, p_slc.shape, 1)
    forced = (blk == 0) | (blk > cur - N_LOCAL)
    score = jnp.where(forced, FORCE_SCORE, p_slc)
    score = jnp.where(blk <= cur, score, -jnp.inf)
    sel = _select_blocks(score, ns)
    upper = (lax.broadcasted_iota(jnp.int32, (n_blk_lanes, n_blk_lanes), 0)
             < lax.broadcasted_iota(jnp.int32, (n_blk_lanes, n_blk_lanes), 1))
    before = _dot(jnp.where(sel, 1.0, 0.0).astype(BF16), jnp.where(upper, 1.0, 0.0).astype(BF16))
    blk_f = blk.astype(F32)
    out_lane = lax.broadcasted_iota(jnp.int32, (H_NSA, LANE), 1)
    out = jnp.zeros((H_NSA, LANE), F32)
    for n in range(N_SELECT):
        v = jnp.sum(jnp.where(sel & (before == float(n)), blk_f, 0.0), axis=-1, keepdims=True)
        out = jnp.where(out_lane == n, v, out)
    idx_ref[...] = out.astype(jnp.int32)


def nsa_sample_select(q8, cmp_kv, *, pos):
    b = q8.shape[0]
    n_seg = cmp_kv.shape[2]
    ns = pos // SEL_BLOCK + 1
    assert ns >= N_SELECT
    n_blk_lanes = -(-ns // LANE) * LANE
    return pl.pallas_call(
        functools.partial(_nsa_sample_select_kernel, pos=pos, n_blk_lanes=n_blk_lanes),
        grid=(b,),
        in_specs=[pl.BlockSpec((None, H_NSA, D_HEAD), lambda i: (i, 0, 0)),
                  pl.BlockSpec((None, 4, n_seg, D_HEAD), lambda i: (i, 0, 0, 0)),
                  pl.BlockSpec((n_seg, n_blk_lanes), lambda i: (0, 0))],
        out_specs=[pl.BlockSpec((None, H_NSA, D_HEAD), lambda i: (i, 0, 0)),
                   pl.BlockSpec((None, H_NSA, LANE), lambda i: (i, 0, 0))],
        out_shape=[jax.ShapeDtypeStruct((b, H_NSA, D_HEAD), F32),
                   jax.ShapeDtypeStruct((b, H_NSA, LANE), jnp.int32)],
        compiler_params=_cp(("arbitrary",)),
        name="nsa_sample_select",
    )(q8, cmp_kv, _slc_matrix(n_seg, n_seg - 1, n_blk_lanes))


def _softmax_with_new(s, valid, s_new, v_mat, v_new):
    s = jnp.where(valid, s, NEG)
    m = jnp.maximum(jnp.max(s, axis=-1, keepdims=True), s_new)
    e = jnp.where(valid, jnp.exp(s - m), 0.0)
    e_new = jnp.exp(s_new - m)
    d = jnp.sum(e, axis=-1, keepdims=True) + e_new
    return (_dot(e.astype(BF16), v_mat) + e_new.astype(BF16).astype(F32) * v_new) / d


def _nsa_sample_attend_kernel(idx_ref, pt_ref, *refs, pos):
    sel_refs = refs[:N_SELECT]
    (q_ref, nk_ref, nv_ref, nwk_ref, nwv_ref, sm_ref, win_ref, ocmp_ref, gm_ref, o_ref) = refs[N_SELECT:]
    b = pl.program_id(0)
    g = pl.program_id(1)
    n_past_blocks = pos // SEL_BLOCK
    q = (q_ref[...] * SCALE).astype(BF16)
    qf = q.astype(F32)
    slope = jnp.concatenate([_slope_column(0, 1), _slope_column(1, 1)], axis=0)

    def new_row(ref):
        return ref[pl.ds(b, 1), :].astype(BF16)

    def slot_rows(ref, slot, n_tok, n_slots):
        both = [ref[pl.ds(slot * G_NSA + gg, n_tok, stride=n_slots), :] for gg in range(G_NSA)]
        return jnp.where(g == 0, both[0], both[1]).astype(BF16)

    k_all = jnp.concatenate([slot_rows(r, 2, SEL_BLOCK, 4 * G_NSA) for r in sel_refs], axis=0)
    v_all = jnp.concatenate([slot_rows(r, 3, SEL_BLOCK, 4 * G_NSA) for r in sel_refs], axis=0)
    width = N_SELECT * SEL_BLOCK
    lane = lax.broadcasted_iota(jnp.int32, (1, width), 1)
    tok = jnp.zeros((1, width), jnp.int32)
    in_cache = jnp.zeros((1, width), jnp.bool_)
    for n in range(N_SELECT):
        blk_id = idx_ref[(b * G_NSA + g) * N_SELECT + n]
        here = (lane // SEL_BLOCK) == n
        tok = jnp.where(here, blk_id * SEL_BLOCK + lane % SEL_BLOCK, tok)
        in_cache = in_cache | (here & (blk_id < n_past_blocks))
    dist = pos - tok
    s = _dot_nt(q, k_all) - slope * dist.astype(F32)
    k_new = new_row(nk_ref)
    s_new = jnp.sum(qf * k_new.astype(F32), axis=-1, keepdims=True)
    o_sel = _softmax_with_new(s, in_cache & (dist >= 0), s_new, v_all, new_row(nv_ref).astype(F32))

    n_win = win_ref.shape[0] // (2 * G_NSA)
    wi = lax.broadcasted_iota(jnp.int32, (1, n_win), 1)
    w_dist = n_win - wi
    s = _dot_nt(q, slot_rows(win_ref, 0, n_win, 2 * G_NSA)) - slope * w_dist.astype(F32)
    s_new = jnp.sum(qf * new_row(nwk_ref).astype(F32), axis=-1, keepdims=True)
    o_win = _softmax_with_new(s, w_dist <= WINDOW, s_new, slot_rows(win_ref, 1, n_win, 2 * G_NSA),
                              new_row(nwv_ref).astype(F32))

    gates = 1.0 / (1.0 + jnp.exp(-sm_ref[pl.ds(b, 1), :]))
    hrow = lax.broadcasted_iota(jnp.int32, (H_NSA, LANE), 0)
    glane = lax.broadcasted_iota(jnp.int32, (H_NSA, LANE), 1)

    def gate_col(branch):
        return jnp.sum(jnp.where(glane == SM_GATE + 3 * hrow + branch, gates, 0.0), axis=-1, keepdims=True)

    o = gate_col(0) * ocmp_ref[...] + gate_col(1) * o_sel + gate_col(2) * o_win
    o = _head_rms(o) * gm_ref[...]
    o_ref[...] = jnp.where(g == 0, o[0:HPG], o[HPG:H_NSA])


def nsa_sample_attend(zs, q8, o_cmp, idx, cache_nsa, state_win, page_table, gm8, layer, *, pos):
    page = cache_nsa.shape[2]
    b, n_pages = page_table.shape
    halves = page // SEL_BLOCK
    cache = _rows_view(cache_nsa)
    blk_rows = cache.shape[2] // halves
    win = _rows_view(state_win)
    last_blk = pos // SEL_BLOCK - 1
    kv_blk = C_KV // D_HEAD

    def sel_spec(n):
        def index_map(i, g, idx_ref, pt_ref):
            blk_id = jnp.minimum(idx_ref[(i * G_NSA + g) * N_SELECT + n], last_blk)
            return (layer, pt_ref[i * n_pages + blk_id // halves], blk_id % halves, 0)
        return pl.BlockSpec((None, None, blk_rows, D_HEAD), index_map)

    def zs_spec(slot):
        return pl.BlockSpec((MS, D_HEAD), lambda i, g, a, c: (0, kv_blk + 2 * slot + g))

    in_specs = ([sel_spec(n) for n in range(N_SELECT)]
                + [pl.BlockSpec((None, H_NSA, D_HEAD), lambda i, g, a, c: (i, 0, 0)),
                   zs_spec(2), zs_spec(3), zs_spec(4), zs_spec(5),
                   pl.BlockSpec((MS, LANE), lambda i, g, a, c: (0, C_SM // LANE)),
                   pl.BlockSpec((None, None, win.shape[2], D_HEAD), lambda i, g, a, c: (layer, i, 0, 0)),
                   pl.BlockSpec((None, H_NSA, D_HEAD), lambda i, g, a, c: (i, 0, 0)),
                   pl.BlockSpec((H_NSA, D_HEAD), lambda i, g, a, c: (0, 0))])
    return pl.pallas_call(
        functools.partial(_nsa_sample_attend_kernel, pos=pos),
        grid_spec=pltpu.PrefetchScalarGridSpec(
            num_scalar_prefetch=2,
            grid=(b, G_NSA),
            in_specs=in_specs,
            out_specs=pl.BlockSpec((None, None, HPG, D_HEAD), lambda i, g, a, c: (i, g, 0, 0)),
        ),
        out_shape=jax.ShapeDtypeStruct((b, G_NSA, HPG, D_HEAD), F32),
        compiler_params=_cp(("arbitrary", "arbitrary")),
        name="nsa_sample_attend",
    )(idx, page_table.reshape(-1), *([cache] * N_SELECT), q8, zs, zs, zs, zs, zs, win, o_cmp, gm8)


SB_PAGES_PER_STEP = 16


def _sb_sample_kernel(pt_ref, *refs):
    page_refs = refs[:SB_PAGES_PER_STEP]
    q_ref, gm_ref, o_ref, run_ref, acc_ref = refs[SB_PAGES_PER_STEP:]
    b = pl.program_id(0)
    s_idx = pl.program_id(1)
    slots = 2 * H_SB
    page = page_refs[0].shape[0] // slots
    kw = H_SB * D_HEAD
    n_u = SB_PAGES_PER_STEP

    @pl.when(s_idx == 0)
    def _():
        run_ref[...] = jnp.zeros_like(run_ref)
        acc_ref[...] = jnp.zeros_like(acc_ref)

    def heads_on_lanes(ref, first_slot):
        return jnp.concatenate([ref[pl.ds(first_slot + h, page, stride=slots), :].astype(BF16)
                                for h in range(H_SB)], axis=1)

    qrow = q_ref[pl.ds(b, 1), :] * SCALE
    hrow = lax.broadcasted_iota(jnp.int32, (8, kw), 0)
    hlane = lax.broadcasted_iota(jnp.int32, (8, kw), 1) // D_HEAD
    qm = jnp.where(hrow == hlane, qrow, 0.0).astype(BF16)
    tri = (lax.broadcasted_iota(jnp.int32, (page, page), 0) > lax.broadcasted_iota(jnp.int32, (page, page), 1))
    tri = jnp.where(tri, 1.0, 0.0).astype(BF16)

    k_all = jnp.concatenate([heads_on_lanes(r, 0) for r in page_refs], axis=0)
    z = _dot_nt(qm, k_all)
    lb = _log_sigmoid(z)
    lr = lb - z
    lr_rows = jnp.concatenate([lr[:, u * page:(u + 1) * page] for u in range(n_u)], axis=0)
    hi, lo = _split2(lr_rows)
    local = _dot(hi, tri) + _dot(lo, tri)
    total = local[:, 0:1] + lr_rows[:, 0:1]
    run = run_ref[:, 0:1]
    offs = [None] * n_u
    for u in range(n_u - 1, -1, -1):
        offs[u] = run
        run = run + total[8 * u:8 * u + 8]
    run_ref[...] = jnp.broadcast_to(run, run_ref.shape)
    after = jnp.concatenate([local[8 * u:8 * u + 8] + offs[u] for u in range(n_u)], axis=1)
    a = jnp.exp(lb + after)
    v_all = jnp.concatenate([heads_on_lanes(r, H_SB) for r in page_refs], axis=0)
    acc = acc_ref[...] + _dot(a.astype(BF16), v_all)
    acc_ref[...] = acc

    @pl.when(s_idx == pl.num_programs(1) - 1)
    def _():
        o = jnp.concatenate([acc[h:h + 1, h * D_HEAD:(h + 1) * D_HEAD] for h in range(H_SB)], axis=0)
        o_ref[...] = _head_rms(o) * gm_ref[...]


def sb_sample(zs, cache_sb, page_table, gm4, layer):
    b, n_pages = page_table.shape
    cache = _rows_view(cache_sb)
    steps = n_pages // SB_PAGES_PER_STEP

    def page_spec(u):
        return pl.BlockSpec(
            (None, None, cache.shape[2], D_HEAD),
            lambda i, s, pt: (layer, pt[i * n_pages + (steps - 1 - s) * SB_PAGES_PER_STEP + u], 0, 0))

    return pl.pallas_call(
        _sb_sample_kernel,
        grid_spec=pltpu.PrefetchScalarGridSpec(
            num_scalar_prefetch=1,
            grid=(b, steps),
            in_specs=[page_spec(u) for u in range(SB_PAGES_PER_STEP)]
            + [pl.BlockSpec((MS, H_SB * D_HEAD), lambda i, s, pt: (0, C_SB // (H_SB * D_HEAD))),
               pl.BlockSpec((H_SB, D_HEAD), lambda i, s, pt: (0, 0))],
            out_specs=pl.BlockSpec((None, H_SB, D_HEAD), lambda i, s, pt: (i, 0, 0)),
            scratch_shapes=[pltpu.VMEM((8, LANE), F32), pltpu.VMEM((8, H_SB * D_HEAD), F32)],
        ),
        out_shape=jax.ShapeDtypeStruct((b, H_SB, D_HEAD), F32),
        compiler_params=_cp(("arbitrary", "arbitrary"), VMEM_BIG),
        name="sb_sample",
    )(page_table.reshape(-1), *([cache] * SB_PAGES_PER_STEP), zs, gm4)


def _gla_sample_kernel(qg_ref, kg_ref, vg_ref, og_ref, sm_ref, wa2_ref, ba_ref, st_ref, gm_ref,
                       o_ref, ns_ref, *, n_b):
    x = _dot(sm_ref[...].astype(BF16), wa2_ref[...].astype(BF16)) + ba_ref[...]
    decay = jnp.exp(_log_sigmoid(x) * (1.0 / GLA_TAU))
    eye = (lax.broadcasted_iota(jnp.int32, (GLA_DK, GLA_DK), 0)
           == lax.broadcasted_iota(jnp.int32, (GLA_DK, GLA_DK), 1))

    def column(row):
        return jnp.sum(jnp.where(eye, row, 0.0), axis=1, keepdims=True)

    o_ref[...] = jnp.zeros_like(o_ref)
    for b in range(n_b):
        for h in range(H_GLA):
            ks = slice(h * GLA_DK, (h + 1) * GLA_DK)
            vs = slice(h * GLA_DV, (h + 1) * GLA_DV)
            s_new = (column(decay[b:b + 1, ks]) * st_ref[b, h]
                     + column(kg_ref[b:b + 1, ks]) * vg_ref[b:b + 1, vs])
            ns_ref[b, h] = s_new
            q_col = column(qg_ref[b:b + 1, ks] * (GLA_DK ** -0.5))
            o = jnp.sum(q_col * s_new, axis=0, keepdims=True)
            o_ref[b:b + 1, vs] = _head_rms(o) * _silu(og_ref[b:b + 1, vs]) * gm_ref[:, vs]


def gla_sample(zs, gla_wa2, gla_ba, state, g_mix):
    n_b = state.shape[0]
    kw = H_GLA * GLA_DK
    vw = H_GLA * GLA_DV
    return pl.pallas_call(
        functools.partial(_gla_sample_kernel, n_b=n_b),
        grid=(1,),
        in_specs=[pl.BlockSpec((MS, kw), lambda i: (0, C_QG // kw)),
                  pl.BlockSpec((MS, kw), lambda i: (0, C_KG // kw)),
                  pl.BlockSpec((MS, vw), lambda i: (0, C_VG // vw)),
                  pl.BlockSpec((MS, vw), lambda i: (0, C_OG // vw)),
                  pl.BlockSpec((MS, LANE), lambda i: (0, C_SM // LANE)),
                  pl.BlockSpec((LANE, kw), lambda i: (0, 0)),
                  pl.BlockSpec((1, kw), lambda i: (0, 0)),
                  pl.BlockSpec(state.shape, lambda i: (0, 0, 0, 0)),
                  pl.BlockSpec((1, vw), lambda i: (0, (H_NSA + H_SB) * D_HEAD // vw))],
        out_specs=[pl.BlockSpec((MS, vw), lambda i: (0, 0)),
                   pl.BlockSpec(state.shape, lambda i: (0, 0, 0, 0))],
        out_shape=[jax.ShapeDtypeStruct((MS, vw), F32), jax.ShapeDtypeStruct(state.shape, F32)],
        compiler_params=_cp(("arbitrary",)),
        name="gla_sample",
    )(zs, zs, zs, zs, zs, _wa2_padded(gla_wa2), gla_ba.reshape(1, kw), state, g_mix)


def _reorder_w_in(w_in):
    o = np.cumsum([0, 1024, 24, 1536, 1536, 256, 256, 512, 16, 512])
    seg = [w_in[..., o[i]:o[i + 1]] for i in range(9)]
    q_n, gate, kv, sb, qg, kg, vg, alr, og = seg
    pad = jnp.zeros(w_in.shape[:-1] + (N_Z - C_SM - 40,), w_in.dtype)
    return jnp.concatenate([q_n, kv, sb, qg, kg, vg, og, gate, alr, pad], axis=-1).astype(BF16)


def kernel(x_prompt, x_sample, cache_nsa, cache_sb, state_win, state_gla, page_table, c_prompt, c_sample, norm1, norm2, w_ada, b_ada, w_in, gla_wa2, gla_ba, cmp_pe, cmp_w1, cmp_w2, g_mix, w_out, ffn_w1, ffn_w3, ffn_w2, final_norm):
    depth = w_in.shape[0]
    bp, t_len, d = x_prompt.shape
    bs = x_sample.shape[0]
    n_pages = page_table.shape[1]
    page = cache_nsa.shape[2]
    past = n_pages * page
    wbuf = state_win.shape[2]
    mp = bp * t_len
    nsa_w = 4 * G_NSA * D_HEAD
    sb_w = 2 * H_SB * D_HEAD
    win_w = 2 * G_NSA * D_HEAD
    assert bs <= MS and x_sample.shape[1] == 1 and wbuf == WINDOW and t_len >= wbuf

    c_all = jnp.zeros((MS, d), F32).at[:bs].set(c_sample).at[bs:bs + bp].set(c_prompt)
    mod = matmul_bias(jnp.broadcast_to(c_all, (depth, MS, d)), w_ada, b_ada[:, None, :],
                      tm=MS, tn=1024, silu_a=True)
    w_in_r = _reorder_w_in(w_in)

    xp = x_prompt.reshape(mp, d)
    xs = jnp.zeros((MS, d), F32).at[:bs].set(x_sample[:, 0])
    outs = {k: [] for k in ("nsa_s", "sb_s", "win_p", "win_s", "gla_p", "gla_s")}
    rows_p = (jnp.zeros((depth, mp * ROW_SLOTS, D_HEAD), F32),) * 2

    for l in range(depth):
        def mod_p(k):
            return mod[l, bs:bs + bp, None, k * d:(k + 1) * d]

        def mod_s(k):
            return mod[l, None, :, k * d:(k + 1) * d]

        gm = g_mix[l][None, :]
        h = norm_mod(xp, norm1[l][None, :], mod_p(1), mod_p(0), tm=512)
        z, rows_nsa, rows_sb = project_in(h, w_in_r, l, rows_p, tm=1024, emit_rows=True)
        rows_p = (rows_nsa, rows_sb)
        z3 = z.reshape(bp, t_len, N_Z)
        cmp_kv = compress_segments(z3, cmp_w1[l], cmp_pe[l], cmp_w2[l], token_major=True)
        o_nsa = nsa_prompt(z3, cmp_kv, gm)
        o_sb = sb_prompt(z3, gm)
        o_gla, st_t = gla_prompt(z3, gla_wa2[l], gla_ba[l], gm)
        xp = matmul_resid([o_nsa.reshape(mp, -1), o_sb.reshape(mp, -1), o_gla.reshape(mp, -1)],
                          w_out, l, xp, mod_p(2), tm=2048, tn=512)
        h = norm_mod(xp, norm2[l][None, :], mod_p(4), mod_p(3), tm=512)
        hid = matmul_swiglu(h, ffn_w1, ffn_w3, l, tm=2048, tn=512)
        xp, seg_s = ffn_down_gather(hid, ffn_w2, l, xp, mod_p(5), cache_nsa, page_table,
                                    tm=mp // bs, tn=256)
        outs["win_p"].append(z3[:, t_len - wbuf:, C_KV + nsa_w:C_KV + nsa_w + win_w]
                             .reshape(bp, wbuf, 2, G_NSA, D_HEAD))
        outs["gla_p"].append(jnp.swapaxes(st_t, 2, 3))

        hs = norm_mod(xs, norm1[l][None, :], mod_s(1), mod_s(0), tm=MS)
        zs = project_in(hs, w_in_r, l, tm=MS, emit_rows=False)
        cmp_s = compress_segments(seg_s, cmp_w1[l], cmp_pe[l], cmp_w2[l])
        q8 = zs[:bs, C_Q:C_Q + H_NSA * D_HEAD].reshape(bs, H_NSA, D_HEAD)
        o_cmp, idx = nsa_sample_select(q8, cmp_s, pos=past)
        idx_flat = idx[:, :G_NSA, :N_SELECT].reshape(-1)
        o_nsa_s = nsa_sample_attend(zs, q8, o_cmp, idx_flat, cache_nsa, state_win, page_table,
                                    gm[0, :H_NSA * D_HEAD].reshape(H_NSA, D_HEAD), l, pos=past)
        o_sb_s = sb_sample(zs, cache_sb, page_table,
                           gm[0, H_NSA * D_HEAD:(H_NSA + H_SB) * D_HEAD].reshape(H_SB, D_HEAD), l)
        o_gla_s, st_new = gla_sample(zs, gla_wa2[l], gla_ba[l], state_gla[l], gm)

        def pad_rows(a):
            return jnp.zeros((MS, a.shape[1]), BF16).at[:bs].set(a.astype(BF16))

        xs = matmul_resid([pad_rows(o_nsa_s.reshape(bs, -1)), pad_rows(o_sb_s.reshape(bs, -1)),
                           o_gla_s.astype(BF16)], w_out, l, xs, mod_s(2), tm=MS, tn=512)
        hs = norm_mod(xs, norm2[l][None, :], mod_s(4), mod_s(3), tm=MS)
        hid_s = matmul_swiglu(hs, ffn_w1, ffn_w3, l, tm=MS, tn=512)
        xs = matmul_resid([hid_s], ffn_w2, l, xs, mod_s(5), tm=MS, tn=512, tk=2816)
        outs["nsa_s"].append(zs[:bs, C_KV:C_KV + nsa_w].reshape(bs, 1, 4, G_NSA, D_HEAD))
        outs["sb_s"].append(zs[:bs, C_SB + H_SB * D_HEAD:C_SB + H_SB * D_HEAD + sb_w].reshape(bs, 1, 2, H_SB, D_HEAD))
        win_new = zs[:bs, C_KV + nsa_w:C_KV + nsa_w + win_w].reshape(bs, 1, 2, G_NSA, D_HEAD)
        outs["win_s"].append(jnp.concatenate([state_win[l][:, 1:], win_new], axis=1))
        outs["gla_s"].append(st_new.astype(state_gla.dtype))

    y_prompt = rmsnorm_rows(xp, final_norm[None, :], tm=512).reshape(bp, t_len, d)
    y_sample = rmsnorm_rows(xs, final_norm[None, :], tm=MS)[:bs].reshape(bs, 1, d)
    nsa_p = rows_p[0].reshape(depth, bp, t_len, 4, G_NSA, D_HEAD)
    sb_p = rows_p[1].reshape(depth, bp, t_len, 2, H_SB, D_HEAD)
    return (y_prompt, y_sample, nsa_p, jnp.stack(outs["nsa_s"]), sb_p,
            jnp.stack(outs["sb_s"]), jnp.stack(outs["win_p"]), jnp.stack(outs["win_s"]),
            jnp.stack(outs["gla_p"]), jnp.stack(outs["gla_s"]))
```

```python
import functools

import numpy as np
import jax
import jax.numpy as jnp
from jax import lax
from jax.experimental import pallas as pl
from jax.experimental.pallas import tpu as pltpu

F32 = jnp.float32
BF16 = jnp.bfloat16

EPS = 1e-6
D_HEAD = 128
H_NSA = 8
G_NSA = 2
HPG = 4
H_SB = 4
H_GLA = 4
GLA_DK = 64
GLA_DV = 128
GLA_RANK = 16
GLA_TAU = 16.0
GLA_CHUNK = 128
CMP_BLOCK = 32
CMP_STRIDE = 16
SEL_BLOCK = 64
N_SELECT = 16
N_LOCAL = 2
WINDOW = 512
FORCE_SCORE = 1e6
SCALE = D_HEAD ** -0.5
LANE = 128
MS = 16

C_Q = 0
C_KV = 1024
C_SB = 2560
C_QG = 4096
C_KG = 4352
C_VG = 4608
C_OG = 5120
C_SM = 5632
N_Z = 6144
SM_GATE = 0
SM_ALR = 24

NEG = -1e30
PEN = -(2.0 ** 100)
PEN_TEST = -(2.0 ** 90)

VMEM_BIG = 56 * 1024 * 1024
VMEM_PROJ = 60 * 1024 * 1024


def _cp(sem, vmem=None):
    return pltpu.CompilerParams(dimension_semantics=sem, vmem_limit_bytes=vmem)


def _log_sigmoid(x):
    return jnp.minimum(x, 0.0) - jnp.log(1.0 + jnp.exp(-jnp.abs(x)))


LOG2E = 1.4426950408889634


def _log2_sigmoid(x2):
    return jnp.minimum(x2, 0.0) - jnp.log2(1.0 + jnp.exp2(-jnp.abs(x2)))


def _silu(x):
    return x * (1.0 / (1.0 + jnp.exp(-x)))


def _split2(x):
    hi = x.astype(BF16)
    lo = (x - hi.astype(F32)).astype(BF16)
    return hi, lo


def _split3(x):
    hi = x.astype(BF16)
    r = x - hi.astype(F32)
    mid = r.astype(BF16)
    lo = (r - mid.astype(F32)).astype(BF16)
    return hi, mid, lo


def _dot(a, b):
    return jnp.dot(a, b, preferred_element_type=F32)


def _dot_nt(a, b):
    return lax.dot_general(a, b, (((1,), (1,)), ((), ())), preferred_element_type=F32)


def _dot_tn(a, b):
    return lax.dot_general(a, b, (((0,), (0,)), ((), ())), preferred_element_type=F32)


def _mm_plain_kernel(a_ref, w_ref, b_ref, o_ref, *, silu_a):
    a = a_ref[...]
    if silu_a:
        a = _silu(a.astype(F32))
    o_ref[...] = _dot(a.astype(BF16), w_ref[...].astype(BF16)) + b_ref[...]


def matmul_bias(a, w, b, *, tm, tn, silu_a=False):
    nl, m, k = a.shape
    n = w.shape[2]
    return pl.pallas_call(
        functools.partial(_mm_plain_kernel, silu_a=silu_a),
        grid=(nl, m // tm, n // tn),
        in_specs=[pl.BlockSpec((None, tm, k), lambda l, i, j: (l, i, 0)),
                  pl.BlockSpec((None, k, tn), lambda l, i, j: (l, 0, j)),
                  pl.BlockSpec((None, 1, tn), lambda l, i, j: (l, 0, j))],
        out_specs=pl.BlockSpec((None, tm, tn), lambda l, i, j: (l, i, j)),
        out_shape=jax.ShapeDtypeStruct((nl, m, n), F32),
        compiler_params=_cp(("arbitrary", "arbitrary", "arbitrary"), VMEM_BIG),
        name="matmul_bias",
    )(a, w, b)


PROJ_TN = 1024
ROW_SLOTS = 8


def _project_in_kernel(*refs, emit_rows, tm):
    x_ref, ng_ref, sc_ref, sh_ref, w_ref = refs[:5]
    h_ref = refs[-1]
    j = pl.program_id(1)

    @pl.when(j == 0)
    def _():
        x = x_ref[...]
        y = x * lax.rsqrt(jnp.mean(x * x, axis=-1, keepdims=True) + EPS) * ng_ref[...]
        h_ref[...] = (y * (1.0 + sc_ref[...]) + sh_ref[...]).astype(h_ref.dtype)

    acc = _dot(h_ref[...], w_ref[...])
    if not emit_rows:
        refs[5][...] = acc
        return
    z_ref, nsa_ref, sb_ref = refs[-4:-1]
    z_ref[...] = acc

    def write_rows(rows_ref):
        for s in range(ROW_SLOTS):
            rows_ref[pl.ds(s, tm, stride=ROW_SLOTS), :] = acc[:, s * D_HEAD:(s + 1) * D_HEAD]

    @pl.when(j == C_KV // PROJ_TN)
    def _():
        write_rows(nsa_ref)

    @pl.when(j == (C_SB + H_SB * D_HEAD) // PROJ_TN)
    def _():
        write_rows(sb_ref)


def project_in(x, g, sc, sh, w_all, layer, rows_prev=None, *, tm, emit_rows):
    m, k = x.shape
    depth, _, n = w_all.shape
    g_cnt, g_rows, _ = sc.shape
    rows_per_gate = m // g_cnt
    mod_spec = pl.BlockSpec((None, 1 if g_rows == 1 else tm, k), lambda i, j: ((i * tm) // rows_per_gate, 0, 0))
    in_specs = [pl.BlockSpec((tm, k), lambda i, j: (i, 0)),
                pl.BlockSpec((1, k), lambda i, j: (0, 0)), mod_spec, mod_spec,
                pl.BlockSpec((None, k, PROJ_TN), lambda i, j: (layer, 0, j))]
    z_spec = pl.BlockSpec((tm, PROJ_TN), lambda i, j: (i, j))
    z_shape = jax.ShapeDtypeStruct((m, n), F32)
    scratch = [pltpu.VMEM((tm, k), BF16)]
    if not emit_rows:
        return pl.pallas_call(
            functools.partial(_project_in_kernel, emit_rows=False, tm=tm),
            grid=(m // tm, n // PROJ_TN), in_specs=in_specs, out_specs=z_spec, out_shape=z_shape,
            scratch_shapes=scratch,
            compiler_params=_cp(("arbitrary", "arbitrary"), VMEM_BIG), name="project_in_small",
        )(x, g, sc, sh, w_all)
    rows_shape = jax.ShapeDtypeStruct((depth, m * ROW_SLOTS, D_HEAD), F32)
    rows_spec = pl.BlockSpec((None, tm * ROW_SLOTS, D_HEAD), lambda i, j: (layer, i, 0),
                             pipeline_mode=pl.Buffered(1))
    return pl.pallas_call(
        functools.partial(_project_in_kernel, emit_rows=True, tm=tm),
        grid=(m // tm, n // PROJ_TN), in_specs=in_specs + [pl.BlockSpec(memory_space=pl.ANY)] * 2,
        out_specs=[z_spec, rows_spec, rows_spec], out_shape=[z_shape, rows_shape, rows_shape],
        scratch_shapes=scratch,
        input_output_aliases={5: 1, 6: 2},
        compiler_params=_cp(("arbitrary", "arbitrary"), VMEM_PROJ), name="project_in",
    )(x, g, sc, sh, w_all, *rows_prev)


def _mm_swiglu_kernel(a_ref, w1_ref, w3_ref, o_ref):
    a = a_ref[...]
    u = _dot(a, w1_ref[...].astype(BF16))
    v = _dot(a, w3_ref[...].astype(BF16))
    o_ref[...] = (_silu(u) * v).astype(o_ref.dtype)


def matmul_swiglu(a, w1, w3, layer, *, tm, tn):
    m, k = a.shape
    n = w1.shape[2]
    return pl.pallas_call(
        _mm_swiglu_kernel,
        grid=(m // tm, n // tn),
        in_specs=[pl.BlockSpec((tm, k), lambda i, j: (i, 0)),
                  pl.BlockSpec((None, k, tn), lambda i, j: (layer, 0, j)),
                  pl.BlockSpec((None, k, tn), lambda i, j: (layer, 0, j))],
        out_specs=pl.BlockSpec((tm, tn), lambda i, j: (i, j)),
        out_shape=jax.ShapeDtypeStruct((m, n), BF16),
        compiler_params=_cp(("arbitrary", "arbitrary"), VMEM_BIG),
        name="matmul_swiglu",
    )(a, w1, w3)


def _mm_resid_kernel(*refs, a_widths, nk):
    n_a = len(a_widths)
    a_refs = refs[:n_a]
    w_ref, x_ref, g_ref, o_ref = refs[n_a:n_a + 4]
    part = None
    off = 0
    for a_ref, kw in zip(a_refs, a_widths):
        d = _dot(a_ref[...], w_ref[off:off + kw, :].astype(BF16))
        part = d if part is None else part + d
        off += kw
    if nk == 1:
        o_ref[...] = x_ref[...] + g_ref[...] * part
        return
    acc_ref = refs[n_a + 4]
    k = pl.program_id(2)

    @pl.when(k == 0)
    def _():
        acc_ref[...] = part

    @pl.when(k > 0)
    def _():
        acc_ref[...] += part

    @pl.when(k == nk - 1)
    def _():
        o_ref[...] = x_ref[...] + g_ref[...] * acc_ref[...]


def matmul_resid(a_list, w, layer, x, gate, *, tm, tn, tk=None):
    m, n = x.shape
    k_total = w.shape[1]
    a_widths = tuple(a.shape[1] for a in a_list)
    if tk is None:
        tk = k_total
    nk = k_total // tk
    assert nk == 1 or len(a_list) == 1
    g_cnt, g_rows, _ = gate.shape
    rows_per_gate = m // g_cnt
    if nk == 1:
        a_specs = [pl.BlockSpec((tm, kw), lambda i, j, k: (i, 0)) for kw in a_widths]
        kernel_widths = a_widths
    else:
        a_specs = [pl.BlockSpec((tm, tk), lambda i, j, k: (i, k))]
        kernel_widths = (tk,)
    g_blk = 1 if g_rows == 1 else tm
    return pl.pallas_call(
        functools.partial(_mm_resid_kernel, a_widths=kernel_widths, nk=nk),
        grid=(m // tm, n // tn, nk),
        in_specs=a_specs + [
            pl.BlockSpec((None, tk, tn), lambda i, j, k: (layer, k, j)),
            pl.BlockSpec((tm, tn), lambda i, j, k: (i, j)),
            pl.BlockSpec((None, g_blk, tn), lambda i, j, k: ((i * tm) // rows_per_gate, 0, j)),
        ],
        out_specs=pl.BlockSpec((tm, tn), lambda i, j, k: (i, j)),
        out_shape=jax.ShapeDtypeStruct((m, n), F32),
        scratch_shapes=[pltpu.VMEM((tm, tn), F32)] if nk > 1 else [],
        compiler_params=_cp(("arbitrary", "arbitrary", "arbitrary"), VMEM_BIG),
        name="matmul_resid",
    )(*a_list, w, x, gate)


def _norm_mod_kernel(x_ref, g_ref, sc_ref, sh_ref, o_ref):
    x = x_ref[...]
    y = x * lax.rsqrt(jnp.mean(x * x, axis=-1, keepdims=True) + EPS) * g_ref[...]
    o_ref[...] = (y * (1.0 + sc_ref[...]) + sh_ref[...]).astype(o_ref.dtype)


def norm_mod(x, g, sc, sh, *, tm):
    m, d = x.shape
    g_cnt, g_rows, _ = sc.shape
    rows_per_gate = m // g_cnt
    g_blk = 1 if g_rows == 1 else tm
    mod_spec = pl.BlockSpec((None, g_blk, d), lambda i: ((i * tm) // rows_per_gate, 0, 0))
    return pl.pallas_call(
        _norm_mod_kernel,
        grid=(m // tm,),
        in_specs=[pl.BlockSpec((tm, d), lambda i: (i, 0)),
                  pl.BlockSpec((1, d), lambda i: (0, 0)), mod_spec, mod_spec],
        out_specs=pl.BlockSpec((tm, d), lambda i: (i, 0)),
        out_shape=jax.ShapeDtypeStruct((m, d), BF16),
        compiler_params=_cp(("arbitrary",)),
        name="norm_mod",
    )(x, g, sc, sh)


def _rmsnorm_kernel(x_ref, g_ref, o_ref):
    x = x_ref[...]
    o_ref[...] = x * lax.rsqrt(jnp.mean(x * x, axis=-1, keepdims=True) + EPS) * g_ref[...]


def rmsnorm_rows(x, g, *, tm):
    m, d = x.shape
    return pl.pallas_call(
        _rmsnorm_kernel,
        grid=(m // tm,),
        in_specs=[pl.BlockSpec((tm, d), lambda i: (i, 0)), pl.BlockSpec((1, d), lambda i: (0, 0))],
        out_specs=pl.BlockSpec((tm, d), lambda i: (i, 0)),
        out_shape=jax.ShapeDtypeStruct((m, d), F32),
        compiler_params=_cp(("arbitrary",)),
        name="rmsnorm_rows",
    )(x, g)


def _compress_kernel(s_ref, w1_ref, pe_ref, w2_ref, o_ref, *, token_major):
    if token_major:
        n_seg = s_ref.shape[0] // CMP_STRIDE
        s = jnp.concatenate([s_ref[pl.ds(p, n_seg, stride=CMP_STRIDE), :] for p in range(CMP_STRIDE)],
                            axis=1).astype(BF16)
    else:
        n_seg = s_ref.shape[0]
        s = s_ref[...].astype(BF16)
    half = CMP_STRIDE * D_HEAD
    w_lo = w1_ref[0].astype(BF16)
    w_hi = w1_ref[1].astype(BF16)
    both = _dot(s, jnp.concatenate([w_lo, w_hi], axis=1))
    lo = both[:, :D_HEAD]
    hi = both[:, D_HEAD:]
    pe = pe_ref[...].astype(BF16)
    bias = _dot(pe[:, :half], w_lo) + _dot(pe[:, half:], w_hi)
    pre = lo + pltpu.roll(hi, n_seg - 1, 0) + bias[0:1, :]
    out = _dot(_silu(pre).astype(BF16), w2_ref[...].astype(BF16))
    row = lax.broadcasted_iota(jnp.int32, out.shape, 0)
    o_ref[...] = jnp.where(row < n_seg - 1, out, 0.0)


def compress_segments(seg, w1, pe, w2, *, token_major=False):
    half = CMP_STRIDE * D_HEAD
    if token_major:
        b, t_len, _ = seg.shape
        n_seg = t_len // CMP_STRIDE
        seg_spec = pl.BlockSpec((None, t_len, D_HEAD), lambda i, j: (i, 0, C_KV // D_HEAD + j))
    else:
        b, _, n_seg, _ = seg.shape
        seg_spec = pl.BlockSpec((None, None, n_seg, half), lambda i, j: (i, j, 0, 0))
    w1r = w1.reshape(2, 2, half, D_HEAD)
    pe8 = jnp.broadcast_to(pe.reshape(2, 1, 2 * half), (2, 8, 2 * half))
    return pl.pallas_call(
        functools.partial(_compress_kernel, token_major=token_major),
        grid=(b, 4),
        in_specs=[seg_spec,
                  pl.BlockSpec((None, 2, half, D_HEAD), lambda i, j: (j // 2, 0, 0, 0)),
                  pl.BlockSpec((None, 8, 2 * half), lambda i, j: (j // 2, 0, 0)),
                  pl.BlockSpec((None, D_HEAD, D_HEAD), lambda i, j: (j // 2, 0, 0))],
        out_specs=pl.BlockSpec((None, None, n_seg, D_HEAD), lambda i, j: (i, j, 0, 0)),
        out_shape=jax.ShapeDtypeStruct((b, 4, n_seg, D_HEAD), F32),
        compiler_params=_cp(("arbitrary", "arbitrary"), VMEM_BIG),
        name="compress_segments",
    )(seg, w1r, pe8, w2)


def _slope_column(g, rows_per_head):
    n = HPG * rows_per_head
    h = lax.broadcasted_iota(jnp.int32, (n, 1), 0) // rows_per_head + g * HPG
    out = jnp.zeros((n, 1), F32)
    for hh in range(H_NSA):
        out = jnp.where(h == hh, 2.0 ** (-8.0 * (hh + 1) / H_NSA), out)
    return out


def _masked_softmax_rows(s, valid):
    s = jnp.where(valid, s, NEG)
    m = jnp.max(s, axis=-1, keepdims=True)
    e = jnp.where(valid, jnp.exp(s - m), 0.0)
    d = jnp.sum(e, axis=-1, keepdims=True)
    return e / jnp.where(d > 0.0, d, 1.0)


def _select_blocks(score, n_blocks):
    lane = lax.broadcasted_iota(jnp.int32, score.shape, 1)
    rank = jnp.zeros(score.shape, jnp.int32)
    for i in range(n_blocks):
        c = score[:, i:i + 1]
        ahead = (c > score) | ((c == score) & (lane > i))
        rank = rank + ahead.astype(jnp.int32)
    return (rank < N_SELECT) & (score > -jnp.inf)


def _slc_matrix(n_cmp_rows, nc, n_lanes):
    i = np.arange(n_cmp_rows)[:, None]
    j = np.arange(n_lanes)[None, :]
    m = (i >= 4 * j - 1) & (i <= 4 * j + 3) & (i < nc)
    return jnp.asarray(m.astype(np.float32), dtype=BF16)


def _head_rms(o):
    return o * lax.rsqrt(jnp.mean(o * o, axis=-1, keepdims=True) + EPS)


def _nsa_prompt_kernel(q_ref, sm_ref, selk_ref, selv_ref, wink_ref, winv_ref, cmpk_ref, cmpv_ref,
                       kaug_ref, caug_ref, mslc_ref, wbias_ref, gm_ref, o_ref, *, qb, tk):
    g = pl.program_id(1)
    qi = pl.program_id(2)
    q0 = qi * qb
    rows = HPG * qb
    n_seg = cmpk_ref.shape[0]
    nc = n_seg - 1
    t_len = selk_ref.shape[0]
    ns = t_len // SEL_BLOCK

    q = q_ref[...] * SCALE
    qs = jnp.concatenate([q[:, h * D_HEAD:(h + 1) * D_HEAD] for h in range(HPG)], axis=0)
    qs_b = qs.astype(BF16)
    slope = _slope_column(g, qb)
    rpos = q0 + lax.broadcasted_iota(jnp.int32, (rows, 1), 0) % qb
    lane = lax.broadcasted_iota(jnp.int32, (rows, LANE), 1)
    ones_col = jnp.where(lax.broadcasted_iota(jnp.int32, (tk, LANE), 1) == 0, 1.0, 0.0).astype(BF16)

    qa_c = jnp.concatenate([qs_b, jnp.where(lane == 0, slope * CMP_STRIDE, 0.0).astype(BF16)], axis=1)
    ka_c = jnp.concatenate([cmpk_ref[...].astype(BF16), caug_ref[...]], axis=1)
    ci = lax.broadcasted_iota(jnp.int32, (1, n_seg), 1)
    c_valid = (ci * CMP_STRIDE + (CMP_BLOCK - 1) <= rpos) & (ci < nc)
    p_cmp = _masked_softmax_rows(_dot_nt(qa_c, ka_c), c_valid)
    o_cmp = _dot(p_cmp.astype(BF16), cmpv_ref[...].astype(BF16))

    imp = p_cmp[0:qb]
    for h in range(1, HPG):
        imp = imp + p_cmp[h * qb:(h + 1) * qb]
    p_slc = None
    for part in _split3(imp):
        d = _dot_nt(mslc_ref[...], part)
        p_slc = d if p_slc is None else p_slc + d
    p_slc = p_slc[0:SEL_BLOCK]
    blk = lax.broadcasted_iota(jnp.int32, (SEL_BLOCK, qb), 0)
    cur = (q0 + lax.broadcasted_iota(jnp.int32, (1, qb), 1)) // SEL_BLOCK
    forced = (blk == 0) | (blk > cur - N_LOCAL)
    score = jnp.where(forced, FORCE_SCORE, p_slc)
    score = jnp.where((blk <= cur) & (blk < ns), score, -jnp.inf)
    sub = 8
    pieces = [score[v * sub:(v + 1) * sub] for v in range(SEL_BLOCK // sub)]
    ranks = [jnp.zeros((sub, qb), jnp.int32) for _ in pieces]
    sub_row = lax.broadcasted_iota(jnp.int32, (sub, qb), 0)
    for i in range(ns):
        c = pieces[i // sub][i % sub:i % sub + 1, :]
        for v, piece in enumerate(pieces):
            if v < i // sub:
                ahead = (c > piece).astype(jnp.int32)
            elif v > i // sub:
                ahead = (c >= piece).astype(jnp.int32)
            else:
                ahead = jnp.where(sub_row > i % sub, (c >= piece).astype(jnp.int32), (c > piece).astype(jnp.int32))
            ranks[v] = ranks[v] + ahead
    rank = jnp.concatenate(ranks, axis=0)
    sel = (rank < N_SELECT) & (score > -jnp.inf)
    pen_t = jnp.where(sel, 0.0, PEN)
    pen = jnp.transpose(jnp.concatenate([pen_t, jnp.zeros((LANE - SEL_BLOCK, qb), F32)], axis=0))
    pen4 = jnp.concatenate([pen] * HPG, axis=0)
    q_aug = jnp.where(lane < SEL_BLOCK, pen4,
                      jnp.where(lane == SEL_BLOCK, slope * SEL_BLOCK,
                                jnp.where(lane == SEL_BLOCK + 1, slope, 0.0)))
    qa = jnp.concatenate([qs_b, q_aug.astype(BF16)], axis=1)

    def sel_scores(k0):
        ka = jnp.concatenate([selk_ref[pl.ds(k0, tk), :].astype(BF16), kaug_ref[pl.ds(k0, tk), :]], axis=1)
        return _dot_nt(qa, ka)

    def sel_update(st, k0, m_run, acc):
        va = jnp.concatenate([selv_ref[pl.ds(k0, tk), :].astype(BF16), ones_col], axis=1)
        m_new = jnp.maximum(m_run, jnp.max(st, axis=-1, keepdims=True))
        p = jnp.exp(st - m_new).astype(BF16)
        return m_new, jnp.exp(m_run - m_new) * acc + _dot(p, va)

    def sel_tile(kt, carry):
        m_run, acc, st = carry
        k0 = pl.multiple_of(kt * tk, tk)
        st_next = sel_scores(k0 + tk)
        m_new, acc = sel_update(st, k0, m_run, acc)
        return m_new, acc, st_next

    n_full = q0 // tk
    init = (jnp.full((rows, 1), NEG, F32), jnp.zeros((rows, 2 * LANE), F32), sel_scores(0))
    m_run, acc_sel, st = lax.fori_loop(0, n_full, sel_tile, init)
    k0 = pl.multiple_of(n_full * tk, tk)
    st = jnp.where(k0 + lax.broadcasted_iota(jnp.int32, (1, tk), 1) <= rpos, st, NEG)
    _, acc_sel = sel_update(st, k0, m_run, acc_sel)
    o_sel = acc_sel[:, 0:D_HEAD] / acc_sel[:, D_HEAD:D_HEAD + 1]

    wk = WINDOW + qb
    ks = pl.multiple_of(jnp.maximum(q0 - WINDOW, 0), qb)
    s = _dot_nt(qs_b, wink_ref[pl.ds(ks, wk), :].astype(BF16)) + wbias_ref[...]
    m = jnp.max(s, axis=-1, keepdims=True)
    p = jnp.exp(s - m).astype(BF16)
    ones_w = jnp.where(lax.broadcasted_iota(jnp.int32, (wk, LANE), 1) == 0, 1.0, 0.0).astype(BF16)
    acc_win = _dot(p, jnp.concatenate([winv_ref[pl.ds(ks, wk), :].astype(BF16), ones_w], axis=1))
    o_win = acc_win[:, 0:D_HEAD] / acc_win[:, D_HEAD:D_HEAD + 1]

    gates = 1.0 / (1.0 + jnp.exp(-sm_ref[...]))

    def gate_col(branch):
        cols = []
        for h in range(HPG):
            c0 = SM_GATE + h * 3 + branch
            c1 = SM_GATE + (HPG + h) * 3 + branch
            cols.append(jnp.where(g == 0, gates[:, c0:c0 + 1], gates[:, c1:c1 + 1]))
        return jnp.concatenate(cols, axis=0)

    o = gate_col(0) * o_cmp + gate_col(1) * o_sel + gate_col(2) * o_win
    o = _head_rms(o)
    for h in range(HPG):
        o_ref[:, h * D_HEAD:(h + 1) * D_HEAD] = (
            o[h * qb:(h + 1) * qb] * gm_ref[:, h * D_HEAD:(h + 1) * D_HEAD]).astype(o_ref.dtype)


def _key_aug(t_len):
    pos = np.arange(t_len)[:, None]
    lane = np.arange(LANE)[None, :]
    a = np.where(lane < SEL_BLOCK, (pos // SEL_BLOCK == lane).astype(np.float32),
                 np.where(lane == SEL_BLOCK, (pos // SEL_BLOCK).astype(np.float32),
                          np.where(lane == SEL_BLOCK + 1, (pos % SEL_BLOCK).astype(np.float32), 0.0)))
    return jnp.asarray(a, dtype=BF16)


def _cmp_aug(n_seg):
    a = np.zeros((n_seg, LANE), np.float32)
    a[:, 0] = np.arange(n_seg)
    return jnp.asarray(a, dtype=BF16)


def _window_bias(qb):
    n_pat = WINDOW // qb + 1
    r = np.arange(HPG * qb)
    head = r // qb
    dist = (np.arange(n_pat)[:, None, None] * qb + (r % qb)[None, :, None]
            - np.arange(WINDOW + qb)[None, None, :])
    valid = (dist >= 0) & (dist <= WINDOW)
    out = np.empty((G_NSA,) + dist.shape, np.float32)
    for g in range(G_NSA):
        slope = 2.0 ** (-8.0 * (g * HPG + head + 1) / H_NSA)
        out[g] = np.where(valid, -slope[None, :, None] * dist, NEG)
    return jnp.asarray(out)


def nsa_prompt(z, cmp_kv, g_mix, *, qb=256, tk=512):
    b, t_len, _ = z.shape
    assert t_len // SEL_BLOCK <= SEL_BLOCK and t_len % tk == 0 and tk % qb == 0 and WINDOW % qb == 0
    n_seg = cmp_kv.shape[2]
    assert n_seg <= 256
    gw = HPG * D_HEAD
    kv_blk = C_KV // D_HEAD
    n_pat = WINDOW // qb
    wbias = _window_bias(qb)

    def kv_spec(slot):
        return pl.BlockSpec((None, t_len, D_HEAD), lambda i, g, q: (i, 0, kv_blk + 2 * slot + g))

    return pl.pallas_call(
        functools.partial(_nsa_prompt_kernel, qb=qb, tk=tk),
        grid=(b, G_NSA, t_len // qb),
        in_specs=[pl.BlockSpec((None, qb, gw), lambda i, g, q: (i, q, g)),
                  pl.BlockSpec((None, qb, LANE), lambda i, g, q: (i, q, C_SM // LANE)),
                  kv_spec(2), kv_spec(3), kv_spec(4), kv_spec(5),
                  pl.BlockSpec((None, None, n_seg, D_HEAD), lambda i, g, q: (i, g, 0, 0)),
                  pl.BlockSpec((None, None, n_seg, D_HEAD), lambda i, g, q: (i, 2 + g, 0, 0)),
                  pl.BlockSpec((t_len, LANE), lambda i, g, q: (0, 0)),
                  pl.BlockSpec((n_seg, LANE), lambda i, g, q: (0, 0)),
                  pl.BlockSpec((LANE, n_seg), lambda i, g, q: (0, 0)),
                  pl.BlockSpec((None, None, HPG * qb, WINDOW + qb),
                               lambda i, g, q: (g, jnp.minimum(q, n_pat), 0, 0)),
                  pl.BlockSpec((1, gw), lambda i, g, q: (0, g))],
        out_specs=pl.BlockSpec((None, qb, gw), lambda i, g, q: (i, q, g)),
        out_shape=jax.ShapeDtypeStruct((b, t_len, H_NSA * D_HEAD), BF16),
        compiler_params=_cp(("arbitrary", "arbitrary", "arbitrary"), VMEM_BIG),
        name="nsa_prompt",
    )(z, z, z, z, z, z, cmp_kv, cmp_kv, _key_aug(t_len), _cmp_aug(n_seg),
      jnp.transpose(_slc_matrix(n_seg, n_seg - 1, LANE)), wbias, g_mix)


def _sb_prompt_kernel(q_ref, k_ref, v_ref, gm_ref, o_ref, *, qb, tk):
    qi = pl.program_id(2)
    q0 = qi * qb
    q = (q_ref[...] * (SCALE * LOG2E)).astype(BF16)
    rpos = q0 + lax.broadcasted_iota(jnp.int32, (qb, 1), 0)
    row = lax.broadcasted_iota(jnp.int32, (2 * tk, tk), 0) % tk
    tri2 = jnp.where(row > lax.broadcasted_iota(jnp.int32, (2 * tk, tk), 1), 1.0, 0.0).astype(BF16)
    n_diag = qb // tk

    def tile(k0, carry, masked):
        run, acc = carry
        z = _dot_nt(q, k_ref[pl.ds(k0, tk), :].astype(BF16))
        lb = _log2_sigmoid(z)
        lr = lb - z
        if masked:
            mask = (k0 + lax.broadcasted_iota(jnp.int32, (1, tk), 1)) < rpos
            lr = jnp.where(mask, lr, 0.0)
        hi, lo = _split2(lr)
        after = _dot(jnp.concatenate([hi, lo], axis=1), tri2) + run
        a = jnp.exp2(lb + after)
        if masked:
            a = jnp.where(mask, a, 0.0)
        acc = acc + _dot(a.astype(BF16), v_ref[pl.ds(k0, tk), :].astype(BF16))
        return after[:, 0:1] + lr[:, 0:1], acc

    carry = (jnp.zeros((qb, 1), F32), jnp.zeros((qb, D_HEAD), F32))
    for it in range(n_diag):
        carry = tile(pl.multiple_of(q0 + (n_diag - 1 - it) * tk, tk), carry, True)

    def full_tiles(it, carry):
        for u in range(n_diag):
            carry = tile(pl.multiple_of(q0 - (it * n_diag + u + 1) * tk, tk), carry, False)
        return carry

    _, acc = lax.fori_loop(0, qi, full_tiles, carry)
    o_ref[...] = (_head_rms(acc) * gm_ref[...]).astype(o_ref.dtype)


def sb_prompt(z, g_mix, *, qb=1024, tk=256):
    b, t_len, _ = z.shape
    assert qb % tk == 0 and t_len % qb == 0
    blk = C_SB // D_HEAD
    gm_blk = (H_NSA * D_HEAD) // D_HEAD
    return pl.pallas_call(
        functools.partial(_sb_prompt_kernel, qb=qb, tk=tk),
        grid=(b, H_SB, t_len // qb),
        in_specs=[pl.BlockSpec((None, qb, D_HEAD), lambda i, h, q: (i, q, blk + h)),
                  pl.BlockSpec((None, t_len, D_HEAD), lambda i, h, q: (i, 0, blk + H_SB + h)),
                  pl.BlockSpec((None, t_len, D_HEAD), lambda i, h, q: (i, 0, blk + 2 * H_SB + h)),
                  pl.BlockSpec((1, D_HEAD), lambda i, h, q: (0, gm_blk + h))],
        out_specs=pl.BlockSpec((None, qb, D_HEAD), lambda i, h, q: (i, q, h)),
        out_shape=jax.ShapeDtypeStruct((b, t_len, H_SB * D_HEAD), BF16),
        compiler_params=_cp(("arbitrary", "arbitrary", "arbitrary")),
        name="sb_prompt",
    )(z, z, z, g_mix)


N_LEVELS = 7


def _gla_level_tables():
    c = GLA_CHUNK
    idx = np.arange(c)
    sums = np.zeros(((N_LEVELS + 1) * c, c), np.float32)
    pair = np.zeros((N_LEVELS + 1, c, c), np.float32)
    sums[:c] = (idx[None, :] <= idx[:, None])
    pair[0] = np.eye(c)
    for lv in range(N_LEVELS):
        mid = ((idx >> (lv + 1)) << (lv + 1)) + (1 << lv)
        upper = idx >= mid
        j = idx[None, :]
        in_up = upper[:, None] & (j >= mid[:, None]) & (j <= idx[:, None])
        in_lo = (~upper)[:, None] & (j > idx[:, None]) & (j <= mid[:, None] - 1)
        sums[(lv + 1) * c:(lv + 2) * c] = in_up | in_lo
        same = (idx[:, None] >> (lv + 1)) == (idx[None, :] >> (lv + 1))
        pair[lv + 1] = same & upper[:, None] & (~upper)[None, :]
    return jnp.asarray(sums, dtype=BF16), jnp.asarray(pair, dtype=F32)


def _gla_prompt_kernel(qg_ref, kg_ref, vg_ref, og_ref, sm_ref, wa2_ref, ba_ref, sums_ref, pair_ref,
                       gm_ref, o_ref, st_ref, state_ref, *, tb):
    nt = pl.program_id(1)
    c = GLA_CHUNK

    @pl.when(nt == 0)
    def _():
        state_ref[...] = jnp.zeros_like(state_ref)

    wa2 = wa2_ref[...].astype(BF16)
    sums = sums_ref[...]

    def chunk(ci, _):
        r0 = pl.multiple_of(ci * c, c)
        rows = pl.ds(r0, c)
        x = _dot(sm_ref[rows, :].astype(BF16), wa2) + ba_ref[...]
        loga = _log_sigmoid(x) * (1.0 / GLA_TAU)
        hi, lo = _split2(loga)
        dsum = _dot(sums, hi) + _dot(sums, lo)
        cb = dsum[0:c]
        q = qg_ref[rows, :] * (GLA_DK ** -0.5)
        k = kg_ref[rows, :]
        v = vg_ref[rows, :].astype(BF16)
        qf = [q.astype(BF16)]
        kf = [k.astype(BF16)]
        for lv in range(N_LEVELS):
            e = jnp.exp(dsum[(lv + 1) * c:(lv + 2) * c])
            qf.append((q * e).astype(BF16))
            kf.append((k * e).astype(BF16))
        c_last = cb[c - 1:c, :]
        q_in = (q * jnp.exp(cb)).astype(BF16)
        k_out = (k * jnp.exp(c_last - cb)).astype(BF16)
        decay = jnp.exp(c_last)
        for h in range(H_GLA):
            ks = slice(h * GLA_DK, (h + 1) * GLA_DK)
            vs = slice(h * GLA_DV, (h + 1) * GLA_DV)
            att = None
            for lv in range(N_LEVELS + 1):
                term = pair_ref[lv] * _dot_nt(qf[lv][:, ks], kf[lv][:, ks])
                att = term if att is None else att + term
            s_t = state_ref[h]
            o_h = _dot(att.astype(BF16), v[:, vs]) + _dot_nt(q_in[:, ks], s_t.astype(BF16))
            state_ref[h] = s_t * decay[:, ks] + _dot_tn(v[:, vs], k_out[:, ks])
            og = og_ref[rows, vs]
            o_ref[rows, vs] = (_head_rms(o_h) * _silu(og) * gm_ref[:, vs]).astype(o_ref.dtype)
        return 0

    lax.fori_loop(0, tb // c, chunk, 0, unroll=4)

    @pl.when(nt == pl.num_programs(1) - 1)
    def _():
        st_ref[...] = state_ref[...]


def _wa2_padded(gla_wa2):
    w = jnp.zeros((LANE, H_GLA * GLA_DK), F32)
    return w.at[SM_ALR:SM_ALR + GLA_RANK].set(gla_wa2)


def gla_prompt(z, gla_wa2, gla_ba, g_mix, *, tb=512):
    b, t_len, _ = z.shape
    kw = H_GLA * GLA_DK
    vw = H_GLA * GLA_DV
    sums, pair = _gla_level_tables()
    n_rows = sums.shape[0]
    return pl.pallas_call(
        functools.partial(_gla_prompt_kernel, tb=tb),
        grid=(b, t_len // tb),
        in_specs=[pl.BlockSpec((None, tb, kw), lambda i, n: (i, n, C_QG // kw)),
                  pl.BlockSpec((None, tb, kw), lambda i, n: (i, n, C_KG // kw)),
                  pl.BlockSpec((None, tb, vw), lambda i, n: (i, n, C_VG // vw)),
                  pl.BlockSpec((None, tb, vw), lambda i, n: (i, n, C_OG // vw)),
                  pl.BlockSpec((None, tb, LANE), lambda i, n: (i, n, C_SM // LANE)),
                  pl.BlockSpec((LANE, kw), lambda i, n: (0, 0)),
                  pl.BlockSpec((1, kw), lambda i, n: (0, 0)),
                  pl.BlockSpec((n_rows, GLA_CHUNK), lambda i, n: (0, 0)),
                  pl.BlockSpec((N_LEVELS + 1, GLA_CHUNK, GLA_CHUNK), lambda i, n: (0, 0, 0)),
                  pl.BlockSpec((1, vw), lambda i, n: (0, (H_NSA + H_SB) * D_HEAD // vw))],
        out_specs=[pl.BlockSpec((None, tb, vw), lambda i, n: (i, n, 0)),
                   pl.BlockSpec((None, H_GLA, GLA_DV, GLA_DK), lambda i, n: (i, 0, 0, 0))],
        out_shape=[jax.ShapeDtypeStruct((b, t_len, vw), BF16),
                   jax.ShapeDtypeStruct((b, H_GLA, GLA_DV, GLA_DK), F32)],
        scratch_shapes=[pltpu.VMEM((H_GLA, GLA_DV, GLA_DK), F32)],
        compiler_params=_cp(("arbitrary", "arbitrary")),
        name="gla_prompt",
    )(z, z, z, z, z, _wa2_padded(gla_wa2), gla_ba.reshape(1, kw), sums, pair, g_mix)


def _gather_pages(page_refs, o_ref):
    slots = 4 * G_NSA
    seg_per_page = page_refs[0].shape[0] // (slots * CMP_STRIDE)
    for pair in range(len(page_refs) // 2):
        r0 = pair * 2 * seg_per_page
        for cg in range(4):
            for p in range(CMP_STRIDE):
                rows = [page_refs[2 * pair + u][pl.ds(p * slots + cg, seg_per_page, stride=CMP_STRIDE * slots), :]
                        for u in range(2)]
                o_ref[cg, r0:r0 + 2 * seg_per_page, p * D_HEAD:(p + 1) * D_HEAD] = (
                    jnp.concatenate(rows, axis=0).astype(o_ref.dtype))


def _rows_view(cache):
    depth, n_pool, page, a, b, d = cache.shape
    return cache.reshape(depth, n_pool, page * a * b, d)


def _ffn_down_gather_kernel(pt_ref, *refs, n_pg):
    page_refs = refs[:n_pg]
    a_ref, w_ref, x_ref, g_ref, o_ref, seg_ref = refs[n_pg:]
    o_ref[...] = x_ref[...] + g_ref[...] * _dot(a_ref[...], w_ref[...].astype(BF16))
    _gather_pages(page_refs, seg_ref)


def ffn_down_gather(a, w, layer, x, gate, cache_nsa, page_table, *, tm, tn):
    m, n = x.shape
    k = w.shape[1]
    page = cache_nsa.shape[2]
    b, n_pages = page_table.shape
    cache = _rows_view(cache_nsa)
    n_i, n_j = m // tm, n // tn
    assert n_i == b and n_pages % n_j == 0 and gate.shape[1] == 1
    n_pg = n_pages // n_j
    seg_per_page = page // CMP_STRIDE
    rows_per_gate = m // gate.shape[0]

    def page_spec(u):
        return pl.BlockSpec((None, None, cache.shape[2], D_HEAD),
                            lambda i, j, pt: (layer, pt[i * n_pages + j * n_pg + u], 0, 0))

    return pl.pallas_call(
        functools.partial(_ffn_down_gather_kernel, n_pg=n_pg),
        grid_spec=pltpu.PrefetchScalarGridSpec(
            num_scalar_prefetch=1,
            grid=(n_i, n_j),
            in_specs=[page_spec(u) for u in range(n_pg)] + [
                pl.BlockSpec((tm, k), lambda i, j, pt: (i, 0), pipeline_mode=pl.Buffered(1)),
                pl.BlockSpec((None, k, tn), lambda i, j, pt: (layer, 0, j)),
                pl.BlockSpec((tm, tn), lambda i, j, pt: (i, j)),
                pl.BlockSpec((None, 1, tn), lambda i, j, pt: ((i * tm) // rows_per_gate, 0, j))],
            out_specs=[pl.BlockSpec((tm, tn), lambda i, j, pt: (i, j)),
                       pl.BlockSpec((None, 4, n_pg * seg_per_page, CMP_STRIDE * D_HEAD),
                                    lambda i, j, pt: (i, 0, j, 0))],
        ),
        out_shape=[jax.ShapeDtypeStruct((m, n), F32),
                   jax.ShapeDtypeStruct((b, 4, n_pages * seg_per_page, CMP_STRIDE * D_HEAD), BF16)],
        compiler_params=_cp(("arbitrary", "arbitrary"), VMEM_BIG),
        name="ffn_down_gather",
    )(page_table.reshape(-1), *([cache] * n_pg), a, w, x, gate)


def _nsa_sample_select_kernel(q_ref, cmp_ref, mslc_ref, ocmp_ref, idx_ref, *, pos, n_blk_lanes):
    n_seg = cmp_ref.shape[1]
    nc = n_seg - 1
    ns = pos // SEL_BLOCK + 1
    cur = pos // SEL_BLOCK
    q = (q_ref[...] * SCALE).astype(BF16)
    row = lax.broadcasted_iota(jnp.int32, (H_NSA, 1), 0)
    slope = _slope_column(0, 1)
    slope = jnp.concatenate([slope, _slope_column(1, 1)], axis=0)
    s = jnp.where(row < HPG, _dot_nt(q, cmp_ref[0].astype(BF16)), _dot_nt(q, cmp_ref[1].astype(BF16)))
    ci = lax.broadcasted_iota(jnp.int32, (1, n_seg), 1)
    c_dist = pos - (ci * CMP_STRIDE + (CMP_BLOCK - 1))
    p = _masked_softmax_rows(s - slope * c_dist.astype(F32), (c_dist >= 0) & (ci < nc))
    pb = p.astype(BF16)
    ocmp_ref[...] = jnp.where(row < HPG, _dot(pb, cmp_ref[2].astype(BF16)), _dot(pb, cmp_ref[3].astype(BF16)))

    imp = jnp.concatenate([jnp.sum(p[g * HPG:(g + 1) * HPG], axis=0, keepdims=True) for g in range(G_NSA)]
                          + [jnp.zeros((H_NSA - G_NSA, n_seg), F32)], axis=0)
    p_slc = None
    for part in _split3(imp):
        d = _dot(part, mslc_ref[...])
        p_slc = d if p_slc is None else p_slc + d
    blk = lax.broadcasted_iota(jnp.int32, p_slc.shape, 1)
    forced = (blk == 0) | (blk > cur - N_LOCAL)
    score = jnp.where(forced, FORCE_SCORE, p_slc)
    score = jnp.where(blk <= cur, score, -jnp.inf)
    sel = _select_blocks(score, ns)
    upper = (lax.broadcasted_iota(jnp.int32, (n_blk_lanes, n_blk_lanes), 0)
             < lax.broadcasted_iota(jnp.int32, (n_blk_lanes, n_blk_lanes), 1))
    before = _dot(jnp.where(sel, 1.0, 0.0).astype(BF16), jnp.where(upper, 1.0, 0.0).astype(BF16))
    blk_f = blk.astype(F32)
    out_lane = lax.broadcasted_iota(jnp.int32, (H_NSA, LANE), 1)
    out = jnp.zeros((H_NSA, LANE), F32)
    for n in range(N_SELECT):
        v = jnp.sum(jnp.where(sel & (before == float(n)), blk_f, 0.0), axis=-1, keepdims=True)
        out = jnp.where(out_lane == n, v, out)
    idx_ref[...] = out.astype(jnp.int32)


def nsa_sample_select(q8, cmp_kv, *, pos):
    b = q8.shape[0]
    n_seg = cmp_kv.shape[2]
    ns = pos // SEL_BLOCK + 1
    assert ns >= N_SELECT
    n_blk_lanes = -(-ns // LANE) * LANE
    return pl.pallas_call(
        functools.partial(_nsa_sample_select_kernel, pos=pos, n_blk_lanes=n_blk_lanes),
        grid=(b,),
        in_specs=[pl.BlockSpec((None, H_NSA, D_HEAD), lambda i: (i, 0, 0)),
                  pl.BlockSpec((None, 4, n_seg, D_HEAD), lambda i: (i, 0, 0, 0)),
                  pl.BlockSpec((n_seg, n_blk_lanes), lambda i: (0, 0))],
        out_specs=[pl.BlockSpec((None, H_NSA, D_HEAD), lambda i: (i, 0, 0)),
                   pl.BlockSpec((None, H_NSA, LANE), lambda i: (i, 0, 0))],
        out_shape=[jax.ShapeDtypeStruct((b, H_NSA, D_HEAD), F32),
                   jax.ShapeDtypeStruct((b, H_NSA, LANE), jnp.int32)],
        compiler_params=_cp(("arbitrary",)),
        name="nsa_sample_select",
    )(q8, cmp_kv, _slc_matrix(n_seg, n_seg - 1, n_blk_lanes))


def _softmax_with_new(s, valid, s_new, v_mat, v_new):
    s = jnp.where(valid, s, NEG)
    m = jnp.maximum(jnp.max(s, axis=-1, keepdims=True), s_new)
    e = jnp.where(valid, jnp.exp(s - m), 0.0)
    e_new = jnp.exp(s_new - m)
    d = jnp.sum(e, axis=-1, keepdims=True) + e_new
    return (_dot(e.astype(BF16), v_mat) + e_new.astype(BF16).astype(F32) * v_new) / d


def _nsa_sample_attend_kernel(idx_ref, pt_ref, *refs, pos):
    sel_refs = refs[:N_SELECT]
    (q_ref, nk_ref, nv_ref, nwk_ref, nwv_ref, sm_ref, win_ref, ocmp_ref, gm_ref, o_ref) = refs[N_SELECT:]
    b = pl.program_id(0)
    g = pl.program_id(1)
    n_past_blocks = pos // SEL_BLOCK
    q = (q_ref[...] * SCALE).astype(BF16)
    qf = q.astype(F32)
    slope = jnp.concatenate([_slope_column(0, 1), _slope_column(1, 1)], axis=0)

    def new_row(ref):
        return ref[pl.ds(b, 1), :].astype(BF16)

    def slot_rows(ref, slot, n_tok, n_slots):
        both = [ref[pl.ds(slot * G_NSA + gg, n_tok, stride=n_slots), :] for gg in range(G_NSA)]
        return jnp.where(g == 0, both[0], both[1]).astype(BF16)

    k_all = jnp.concatenate([slot_rows(r, 2, SEL_BLOCK, 4 * G_NSA) for r in sel_refs], axis=0)
    v_all = jnp.concatenate([slot_rows(r, 3, SEL_BLOCK, 4 * G_NSA) for r in sel_refs], axis=0)
    width = N_SELECT * SEL_BLOCK
    lane = lax.broadcasted_iota(jnp.int32, (1, width), 1)
    tok = jnp.zeros((1, width), jnp.int32)
    in_cache = jnp.zeros((1, width), jnp.bool_)
    for n in range(N_SELECT):
        blk_id = idx_ref[(b * G_NSA + g) * N_SELECT + n]
        here = (lane // SEL_BLOCK) == n
        tok = jnp.where(here, blk_id * SEL_BLOCK + lane % SEL_BLOCK, tok)
        in_cache = in_cache | (here & (blk_id < n_past_blocks))
    dist = pos - tok
    s = _dot_nt(q, k_all) - slope * dist.astype(F32)
    k_new = new_row(nk_ref)
    s_new = jnp.sum(qf * k_new.astype(F32), axis=-1, keepdims=True)
    o_sel = _softmax_with_new(s, in_cache & (dist >= 0), s_new, v_all, new_row(nv_ref).astype(F32))

    n_win = win_ref.shape[0] // (2 * G_NSA)
    wi = lax.broadcasted_iota(jnp.int32, (1, n_win), 1)
    w_dist = n_win - wi
    s = _dot_nt(q, slot_rows(win_ref, 0, n_win, 2 * G_NSA)) - slope * w_dist.astype(F32)
    s_new = jnp.sum(qf * new_row(nwk_ref).astype(F32), axis=-1, keepdims=True)
    o_win = _softmax_with_new(s, w_dist <= WINDOW, s_new, slot_rows(win_ref, 1, n_win, 2 * G_NSA),
                              new_row(nwv_ref).astype(F32))

    gates = 1.0 / (1.0 + jnp.exp(-sm_ref[pl.ds(b, 1), :]))
    hrow = lax.broadcasted_iota(jnp.int32, (H_NSA, LANE), 0)
    glane = lax.broadcasted_iota(jnp.int32, (H_NSA, LANE), 1)

    def gate_col(branch):
        return jnp.sum(jnp.where(glane == SM_GATE + 3 * hrow + branch, gates, 0.0), axis=-1, keepdims=True)

    o = gate_col(0) * ocmp_ref[...] + gate_col(1) * o_sel + gate_col(2) * o_win
    o = _head_rms(o) * gm_ref[...]
    o_ref[...] = jnp.where(g == 0, o[0:HPG], o[HPG:H_NSA])


def nsa_sample_attend(zs, q8, o_cmp, idx, cache_nsa, state_win, page_table, gm8, layer, *, pos):
    page = cache_nsa.shape[2]
    b, n_pages = page_table.shape
    halves = page // SEL_BLOCK
    cache = _rows_view(cache_nsa)
    blk_rows = cache.shape[2] // halves
    win = _rows_view(state_win)
    last_blk = pos // SEL_BLOCK - 1
    kv_blk = C_KV // D_HEAD

    def sel_spec(n):
        def index_map(i, g, idx_ref, pt_ref):
            blk_id = jnp.minimum(idx_ref[(i * G_NSA + g) * N_SELECT + n], last_blk)
            return (layer, pt_ref[i * n_pages + blk_id // halves], blk_id % halves, 0)
        return pl.BlockSpec((None, None, blk_rows, D_HEAD), index_map)

    def zs_spec(slot):
        return pl.BlockSpec((MS, D_HEAD), lambda i, g, a, c: (0, kv_blk + 2 * slot + g))

    in_specs = ([sel_spec(n) for n in range(N_SELECT)]
                + [pl.BlockSpec((None, H_NSA, D_HEAD), lambda i, g, a, c: (i, 0, 0)),
                   zs_spec(2), zs_spec(3), zs_spec(4), zs_spec(5),
                   pl.BlockSpec((MS, LANE), lambda i, g, a, c: (0, C_SM // LANE)),
                   pl.BlockSpec((None, None, win.shape[2], D_HEAD), lambda i, g, a, c: (layer, i, 0, 0)),
                   pl.BlockSpec((None, H_NSA, D_HEAD), lambda i, g, a, c: (i, 0, 0)),
                   pl.BlockSpec((H_NSA, D_HEAD), lambda i, g, a, c: (0, 0))])
    return pl.pallas_call(
        functools.partial(_nsa_sample_attend_kernel, pos=pos),
        grid_spec=pltpu.PrefetchScalarGridSpec(
            num_scalar_prefetch=2,
            grid=(b, G_NSA),
            in_specs=in_specs,
            out_specs=pl.BlockSpec((None, None, HPG, D_HEAD), lambda i, g, a, c: (i, g, 0, 0)),
        ),
        out_shape=jax.ShapeDtypeStruct((b, G_NSA, HPG, D_HEAD), F32),
        compiler_params=_cp(("arbitrary", "arbitrary")),
        name="nsa_sample_attend",
    )(idx, page_table.reshape(-1), *([cache] * N_SELECT), q8, zs, zs, zs, zs, zs, win, o_cmp, gm8)


SB_PAGES_PER_STEP = 16


def _sb_sample_kernel(pt_ref, *refs):
    page_refs = refs[:SB_PAGES_PER_STEP]
    q_ref, gm_ref, o_ref, run_ref, acc_ref = refs[SB_PAGES_PER_STEP:]
    b = pl.program_id(0)
    s_idx = pl.program_id(1)
    slots = 2 * H_SB
    page = page_refs[0].shape[0] // slots
    kw = H_SB * D_HEAD
    n_u = SB_PAGES_PER_STEP

    @pl.when(s_idx == 0)
    def _():
        run_ref[...] = jnp.zeros_like(run_ref)
        acc_ref[...] = jnp.zeros_like(acc_ref)

    def heads_on_lanes(ref, first_slot):
        return jnp.concatenate([ref[pl.ds(first_slot + h, page, stride=slots), :].astype(BF16)
                                for h in range(H_SB)], axis=1)

    qrow = q_ref[pl.ds(b, 1), :] * SCALE
    hrow = lax.broadcasted_iota(jnp.int32, (8, kw), 0)
    hlane = lax.broadcasted_iota(jnp.int32, (8, kw), 1) // D_HEAD
    qm = jnp.where(hrow == hlane, qrow, 0.0).astype(BF16)
    tri = (lax.broadcasted_iota(jnp.int32, (page, page), 0) > lax.broadcasted_iota(jnp.int32, (page, page), 1))
    tri = jnp.where(tri, 1.0, 0.0).astype(BF16)

    k_all = jnp.concatenate([heads_on_lanes(r, 0) for r in page_refs], axis=0)
    z = _dot_nt(qm, k_all)
    lb = _log_sigmoid(z)
    lr = lb - z
    lr_rows = jnp.concatenate([lr[:, u * page:(u + 1) * page] for u in range(n_u)], axis=0)
    hi, lo = _split2(lr_rows)
    local = _dot(hi, tri) + _dot(lo, tri)
    total = local[:, 0:1] + lr_rows[:, 0:1]
    run = run_ref[:, 0:1]
    offs = [None] * n_u
    for u in range(n_u - 1, -1, -1):
        offs[u] = run
        run = run + total[8 * u:8 * u + 8]
    run_ref[...] = jnp.broadcast_to(run, run_ref.shape)
    after = jnp.concatenate([local[8 * u:8 * u + 8] + offs[u] for u in range(n_u)], axis=1)
    a = jnp.exp(lb + after)
    v_all = jnp.concatenate([heads_on_lanes(r, H_SB) for r in page_refs], axis=0)
    acc = acc_ref[...] + _dot(a.astype(BF16), v_all)
    acc_ref[...] = acc

    @pl.when(s_idx == pl.num_programs(1) - 1)
    def _():
        o = jnp.concatenate([acc[h:h + 1, h * D_HEAD:(h + 1) * D_HEAD] for h in range(H_SB)], axis=0)
        o_ref[...] = _head_rms(o) * gm_ref[...]


def sb_sample(zs, cache_sb, page_table, gm4, layer):
    b, n_pages = page_table.shape
    cache = _rows_view(cache_sb)
    steps = n_pages // SB_PAGES_PER_STEP

    def page_spec(u):
        return pl.BlockSpec(
            (None, None, cache.shape[2], D_HEAD),
            lambda i, s, pt: (layer, pt[i * n_pages + (steps - 1 - s) * SB_PAGES_PER_STEP + u], 0, 0))

    return pl.pallas_call(
        _sb_sample_kernel,
        grid_spec=pltpu.PrefetchScalarGridSpec(
            num_scalar_prefetch=1,
            grid=(b, steps),
            in_specs=[page_spec(u) for u in range(SB_PAGES_PER_STEP)]
            + [pl.BlockSpec((MS, H_SB * D_HEAD), lambda i, s, pt: (0, C_SB // (H_SB * D_HEAD))),
               pl.BlockSpec((H_SB, D_HEAD), lambda i, s, pt: (0, 0))],
            out_specs=pl.BlockSpec((None, H_SB, D_HEAD), lambda i, s, pt: (i, 0, 0)),
            scratch_shapes=[pltpu.VMEM((8, LANE), F32), pltpu.VMEM((8, H_SB * D_HEAD), F32)],
        ),
        out_shape=jax.ShapeDtypeStruct((b, H_SB, D_HEAD), F32),
        compiler_params=_cp(("arbitrary", "arbitrary"), VMEM_BIG),
        name="sb_sample",
    )(page_table.reshape(-1), *([cache] * SB_PAGES_PER_STEP), zs, gm4)


def _gla_sample_kernel(qg_ref, kg_ref, vg_ref, og_ref, sm_ref, wa2_ref, ba_ref, st_ref, gm_ref,
                       o_ref, ns_ref, *, n_b):
    x = _dot(sm_ref[...].astype(BF16), wa2_ref[...].astype(BF16)) + ba_ref[...]
    decay = jnp.exp(_log_sigmoid(x) * (1.0 / GLA_TAU))
    eye = (lax.broadcasted_iota(jnp.int32, (GLA_DK, GLA_DK), 0)
           == lax.broadcasted_iota(jnp.int32, (GLA_DK, GLA_DK), 1))

    def column(row):
        return jnp.sum(jnp.where(eye, row, 0.0), axis=1, keepdims=True)

    o_ref[...] = jnp.zeros_like(o_ref)
    for b in range(n_b):
        for h in range(H_GLA):
            ks = slice(h * GLA_DK, (h + 1) * GLA_DK)
            vs = slice(h * GLA_DV, (h + 1) * GLA_DV)
            s_new = (column(decay[b:b + 1, ks]) * st_ref[b, h]
                     + column(kg_ref[b:b + 1, ks]) * vg_ref[b:b + 1, vs])
            ns_ref[b, h] = s_new
            q_col = column(qg_ref[b:b + 1, ks] * (GLA_DK ** -0.5))
            o = jnp.sum(q_col * s_new, axis=0, keepdims=True)
            o_ref[b:b + 1, vs] = _head_rms(o) * _silu(og_ref[b:b + 1, vs]) * gm_ref[:, vs]


def gla_sample(zs, gla_wa2, gla_ba, state, g_mix):
    n_b = state.shape[0]
    kw = H_GLA * GLA_DK
    vw = H_GLA * GLA_DV
    return pl.pallas_call(
        functools.partial(_gla_sample_kernel, n_b=n_b),
        grid=(1,),
        in_specs=[pl.BlockSpec((MS, kw), lambda i: (0, C_QG // kw)),
                  pl.BlockSpec((MS, kw), lambda i: (0, C_KG // kw)),
                  pl.BlockSpec((MS, vw), lambda i: (0, C_VG // vw)),
                  pl.BlockSpec((MS, vw), lambda i: (0, C_OG // vw)),
                  pl.BlockSpec((MS, LANE), lambda i: (0, C_SM // LANE)),
                  pl.BlockSpec((LANE, kw), lambda i: (0, 0)),
                  pl.BlockSpec((1, kw), lambda i: (0, 0)),
                  pl.BlockSpec(state.shape, lambda i: (0, 0, 0, 0)),
                  pl.BlockSpec((1, vw), lambda i: (0, (H_NSA + H_SB) * D_HEAD // vw))],
        out_specs=[pl.BlockSpec((MS, vw), lambda i: (0, 0)),
                   pl.BlockSpec(state.shape, lambda i: (0, 0, 0, 0))],
        out_shape=[jax.ShapeDtypeStruct((MS, vw), F32), jax.ShapeDtypeStruct(state.shape, F32)],
        compiler_params=_cp(("arbitrary",)),
        name="gla_sample",
    )(zs, zs, zs, zs, zs, _wa2_padded(gla_wa2), gla_ba.reshape(1, kw), state, g_mix)


def _reorder_w_in(w_in):
    o = np.cumsum([0, 1024, 24, 1536, 1536, 256, 256, 512, 16, 512])
    seg = [w_in[..., o[i]:o[i + 1]] for i in range(9)]
    q_n, gate, kv, sb, qg, kg, vg, alr, og = seg
    pad = jnp.zeros(w_in.shape[:-1] + (N_Z - C_SM - 40,), w_in.dtype)
    return jnp.concatenate([q_n, kv, sb, qg, kg, vg, og, gate, alr, pad], axis=-1).astype(BF16)


def kernel(x_prompt, x_sample, cache_nsa, cache_sb, state_win, state_gla, page_table, c_prompt, c_sample, norm1, norm2, w_ada, b_ada, w_in, gla_wa2, gla_ba, cmp_pe, cmp_w1, cmp_w2, g_mix, w_out, ffn_w1, ffn_w3, ffn_w2, final_norm):
    depth = w_in.shape[0]
    bp, t_len, d = x_prompt.shape
    bs = x_sample.shape[0]
    n_pages = page_table.shape[1]
    page = cache_nsa.shape[2]
    past = n_pages * page
    wbuf = state_win.shape[2]
    mp = bp * t_len
    nsa_w = 4 * G_NSA * D_HEAD
    sb_w = 2 * H_SB * D_HEAD
    win_w = 2 * G_NSA * D_HEAD
    assert bs <= MS and x_sample.shape[1] == 1 and wbuf == WINDOW and t_len >= wbuf

    c_all = jnp.zeros((MS, d), F32).at[:bs].set(c_sample).at[bs:bs + bp].set(c_prompt)
    mod = matmul_bias(jnp.broadcast_to(c_all, (depth, MS, d)), w_ada, b_ada[:, None, :],
                      tm=MS, tn=1024, silu_a=True)
    w_in_r = _reorder_w_in(w_in)

    xp = x_prompt.reshape(mp, d)
    xs = jnp.zeros((MS, d), F32).at[:bs].set(x_sample[:, 0])
    outs = {k: [] for k in ("nsa_s", "sb_s", "win_p", "win_s", "gla_p", "gla_s")}
    rows_p = (jnp.zeros((depth, mp * ROW_SLOTS, D_HEAD), F32),) * 2

    for l in range(depth):
        def mod_p(k):
            return mod[l, bs:bs + bp, None, k * d:(k + 1) * d]

        def mod_s(k):
            return mod[l, None, :, k * d:(k + 1) * d]

        gm = g_mix[l][None, :]
        z, rows_nsa, rows_sb = project_in(xp, norm1[l][None, :], mod_p(1), mod_p(0), w_in_r, l, rows_p,
                                          tm=1024, emit_rows=True)
        rows_p = (rows_nsa, rows_sb)
        z3 = z.reshape(bp, t_len, N_Z)
        cmp_kv = compress_segments(z3, cmp_w1[l], cmp_pe[l], cmp_w2[l], token_major=True)
        o_nsa = nsa_prompt(z3, cmp_kv, gm)
        o_sb = sb_prompt(z3, gm)
        o_gla, st_t = gla_prompt(z3, gla_wa2[l], gla_ba[l], gm)
        xp = matmul_resid([o_nsa.reshape(mp, -1), o_sb.reshape(mp, -1), o_gla.reshape(mp, -1)],
                          w_out, l, xp, mod_p(2), tm=2048, tn=512)
        h = norm_mod(xp, norm2[l][None, :], mod_p(4), mod_p(3), tm=512)
        hid = matmul_swiglu(h, ffn_w1, ffn_w3, l, tm=2048, tn=512)
        xp, seg_s = ffn_down_gather(hid, ffn_w2, l, xp, mod_p(5), cache_nsa, page_table,
                                    tm=mp // bs, tn=256)
        outs["win_p"].append(z3[:, t_len - wbuf:, C_KV + nsa_w:C_KV + nsa_w + win_w]
                             .reshape(bp, wbuf, 2, G_NSA, D_HEAD))
        outs["gla_p"].append(jnp.swapaxes(st_t, 2, 3))

        zs = project_in(xs, norm1[l][None, :], mod_s(1), mod_s(0), w_in_r, l,
                        tm=MS, emit_rows=False)
        cmp_s = compress_segments(seg_s, cmp_w1[l], cmp_pe[l], cmp_w2[l])
        q8 = zs[:bs, C_Q:C_Q + H_NSA * D_HEAD].reshape(bs, H_NSA, D_HEAD)
        o_cmp, idx = nsa_sample_select(q8, cmp_s, pos=past)
        idx_flat = idx[:, :G_NSA, :N_SELECT].reshape(-1)
        o_nsa_s = nsa_sample_attend(zs, q8, o_cmp, idx_flat, cache_nsa, state_win, page_table,
                                    gm[0, :H_NSA * D_HEAD].reshape(H_NSA, D_HEAD), l, pos=past)
        o_sb_s = sb_sample(zs, cache_sb, page_table,
                           gm[0, H_NSA * D_HEAD:(H_NSA + H_SB) * D_HEAD].reshape(H_SB, D_HEAD), l)
        o_gla_s, st_new = gla_sample(zs, gla_wa2[l], gla_ba[l], state_gla[l], gm)

        def pad_rows(a):
            return jnp.zeros((MS, a.shape[1]), BF16).at[:bs].set(a.astype(BF16))

        xs = matmul_resid([pad_rows(o_nsa_s.reshape(bs, -1)), pad_rows(o_sb_s.reshape(bs, -1)),
                           o_gla_s.astype(BF16)], w_out, l, xs, mod_s(2), tm=MS, tn=512)
        hs = norm_mod(xs, norm2[l][None, :], mod_s(4), mod_s(3), tm=MS)
        hid_s = matmul_swiglu(hs, ffn_w1, ffn_w3, l, tm=MS, tn=512)
        xs = matmul_resid([hid_s], ffn_w2, l, xs, mod_s(5), tm=MS, tn=512, tk=2816)
        outs["nsa_s"].append(zs[:bs, C_KV:C_KV + nsa_w].reshape(bs, 1, 4, G_NSA, D_HEAD))
        outs["sb_s"].append(zs[:bs, C_SB + H_SB * D_HEAD:C_SB + H_SB * D_HEAD + sb_w].reshape(bs, 1, 2, H_SB, D_HEAD))
        win_new = zs[:bs, C_KV + nsa_w:C_KV + nsa_w + win_w].reshape(bs, 1, 2, G_NSA, D_HEAD)
        outs["win_s"].append(jnp.concatenate([state_win[l][:, 1:], win_new], axis=1))
        outs["gla_s"].append(st_new.astype(state_gla.dtype))

    y_prompt = rmsnorm_rows(xp, final_norm[None, :], tm=512).reshape(bp, t_len, d)
    y_sample = rmsnorm_rows(xs, final_norm[None, :], tm=MS)[:bs].reshape(bs, 1, d)
    nsa_p = rows_p[0].reshape(depth, bp, t_len, 4, G_NSA, D_HEAD)
    sb_p = rows_p[1].reshape(depth, bp, t_len, 2, H_SB, D_HEAD)
    return (y_prompt, y_sample, nsa_p, jnp.stack(outs["nsa_s"]), sb_p,
            jnp.stack(outs["sb_s"]), jnp.stack(outs["win_p"]), jnp.stack(outs["win_s"]),
            jnp.stack(outs["gla_p"]), jnp.stack(outs["gla_s"]))
```
